```python
import math
import jax
import jax.numpy as jnp
from jax import lax
import numpy as np

D_MODEL = 1024
BATCH = 4
SEQ = 4096
DEPTH = 2

HEAD_DIM = 64
Q_BLOCK = 128
NORM_EPS = 1e-6
NEG_INF = -1e30
FORCE_SCORE = 1e4

DIFF_HEADS = D_MODEL // (2 * HEAD_DIM)
DIFF_QK_DIM = HEAD_DIM // 2
DIFF_V_DIM = HEAD_DIM

NSA_HEADS = D_MODEL // (2 * HEAD_DIM)
NSA_KV_GROUPS = 2
NSA_HEADS_PER_GROUP = NSA_HEADS // NSA_KV_GROUPS
NSA_CMP_BLOCK = 32
NSA_CMP_STRIDE = 16
NSA_SEL_BLOCK = 64
NSA_TOP_N = 16
NSA_WINDOW = 512
NSA_BRANCHES = 3

DIFF_QK_COLS = DIFF_HEADS * 2 * DIFF_QK_DIM
DIFF_V_COLS = DIFF_HEADS * DIFF_V_DIM
NSA_Q_COLS = NSA_HEADS * HEAD_DIM
NSA_KV_COLS = NSA_BRANCHES * 2 * NSA_KV_GROUPS * HEAD_DIM
NSA_GATE_COLS = NSA_BRANCHES * NSA_HEADS
IN_COLS = 2 * DIFF_QK_COLS + DIFF_V_COLS + NSA_Q_COLS + NSA_KV_COLS + NSA_GATE_COLS
ATTN_OUT = DIFF_V_COLS + NSA_Q_COLS

RWKV_HEAD = 64
RWKV_HEADS = D_MODEL // RWKV_HEAD
RWKV_DECAY_LORA = 64
RWKV_ICLR_LORA = 64
RWKV_GATE_LORA = 160
RWKV_LNX_EPS = 64e-5

N_GROUPS = 4
EXPERTS_PER_GROUP = 8
N_EXPERTS = N_GROUPS * EXPERTS_PER_GROUP
TOP_K_IN_GROUP = 2
D_EXPERT = 256
MOE_BLOCK = 128

kernel_name = 'hybrid_diff_nsa_rwkv7_hmoe'


def rms_norm(x, gain, eps=NORM_EPS):
    xf = x.astype(jnp.float32)
    y = xf * lax.rsqrt(jnp.mean(xf * xf, axis=-1, keepdims=True) + eps)
    return (y * gain.astype(jnp.float32)).astype(x.dtype)


def masked_softmax(scores, mask):
    return jax.nn.softmax(jnp.where(mask, scores.astype(jnp.float32), NEG_INF), axis=-1)


def diff_attention(q, k, v, q_gain, k_gain, lam_p, subln_gain, lambda_init):
    B, S, H, _, dq = q.shape
    q = rms_norm(q, q_gain)
    k = rms_norm(k, k_gain)
    lp = lam_p.astype(jnp.float32)
    lam = jnp.exp(jnp.sum(lp[0] * lp[1])) - jnp.exp(jnp.sum(lp[2] * lp[3])) + lambda_init
    scale = dq ** -0.5
    outs = []
    for blk in range(S // Q_BLOCK):
        q0, q1 = blk * Q_BLOCK, (blk + 1) * Q_BLOCK
        s = jnp.einsum('bqhmd,bkhmd->bhmqk', q[:, q0:q1], k[:, :q1]) * scale
        causal = jnp.arange(q0, q1)[:, None] >= jnp.arange(q1)[None, :]
        p = masked_softmax(s, causal)
        p_diff = p[:, :, 0] - lam * p[:, :, 1]
        outs.append(jnp.einsum('bhqk,bkhd->bqhd', p_diff.astype(v.dtype), v[:, :q1]))
    o = jnp.concatenate(outs, axis=1)
    o = rms_norm(o, subln_gain) * (1.0 - lambda_init)
    return o.reshape(B, S, H * v.shape[-1])


def nsa_attention(q, kv, gates, q_gain, k_gain, cmp_pe, cmp_w1, cmp_w2):
    B, S, H, dh = q.shape
    G, HPG = NSA_KV_GROUPS, NSA_HEADS_PER_GROUP
    scale = dh ** -0.5
    pos = jnp.arange(S)
    n_qb = S // Q_BLOCK
    q = rms_norm(q, q_gain).reshape(B, S, G, HPG, dh)

    n_cmp = (S - NSA_CMP_BLOCK) // NSA_CMP_STRIDE + 1
    c_start = jnp.arange(n_cmp) * NSA_CMP_STRIDE
    c_idx = c_start[:, None] + jnp.arange(NSA_CMP_BLOCK)[None, :]

    def compress(t, pe, w1, w2):
        blocks = t[:, c_idx] + pe[:, None, :]
        flat = jnp.swapaxes(blocks, 2, 3).reshape(B, n_cmp, G, NSA_CMP_BLOCK * dh)
        return jax.nn.gelu(flat @ w1) @ w2

    k_cmp = rms_norm(compress(kv[:, :, 0, 0], cmp_pe[0], cmp_w1[0], cmp_w2[0]), k_gain[0])
    v_cmp = compress(kv[:, :, 0, 1], cmp_pe[1], cmp_w1[1], cmp_w2[1])
    cmp_ok = ((c_start + NSA_CMP_BLOCK - 1)[None, :] <= pos[:, None])[None, :, None, None, :]
    s_cmp = jnp.einsum('bsghd,bcgd->bsghc', q, k_cmp) * scale
    p_cmp = masked_softmax(s_cmp, cmp_ok) * cmp_ok
    o_cmp = jnp.einsum('bsghc,bcgd->bsghd', p_cmp.astype(v_cmp.dtype), v_cmp)

    n_sel = S // NSA_SEL_BLOCK
    top_n = min(NSA_TOP_N, n_sel)
    s_start = jnp.arange(n_sel) * NSA_SEL_BLOCK
    cover = ((c_start[:, None] < s_start[None, :] + NSA_SEL_BLOCK)
             & (c_start[:, None] + NSA_CMP_BLOCK > s_start[None, :])).astype(jnp.float32)
    importance = jnp.einsum('bsghc,cj->bsgj', p_cmp, cover)
    blk = jnp.arange(n_sel)[None, :]
    cur = (pos // NSA_SEL_BLOCK)[:, None]
    forced = ((blk == cur) | (blk == 0))[None, :, None, :]
    future = (blk > cur)[None, :, None, :]
    importance = jnp.where(forced, FORCE_SCORE, jnp.where(future, NEG_INF, importance))
    _, sel_idx = lax.top_k(importance, top_n)

    def to_blocks(t):
        return t.reshape(B, n_sel, NSA_SEL_BLOCK, G, dh).transpose(0, 3, 1, 2, 4)

    k_blk = to_blocks(rms_norm(kv[:, :, 1, 0], k_gain[1]))
    v_blk = to_blocks(kv[:, :, 1, 1])
    b_ix = jnp.arange(B)[:, None, None, None]
    g_ix = jnp.arange(G)[None, None, :, None]
    in_blk = jnp.arange(NSA_SEL_BLOCK)

    def select_block(args):
        q_c, idx_c, pos_c = args
        k_g = k_blk[b_ix, g_ix, idx_c]
        v_g = v_blk[b_ix, g_ix, idx_c]
        s = jnp.einsum('bqghd,bqgnkd->bqghnk', q_c, k_g) * scale
        ok = (idx_c[..., None] * NSA_SEL_BLOCK + in_blk) <= pos_c[None, :, None, None, None]
        s = s.reshape(B, Q_BLOCK, G, HPG, top_n * NSA_SEL_BLOCK)
        ok = ok.reshape(B, Q_BLOCK, G, 1, top_n * NSA_SEL_BLOCK)
        p = masked_softmax(s, ok).reshape(B, Q_BLOCK, G, HPG, top_n, NSA_SEL_BLOCK)
        return jnp.einsum('bqghnk,bqgnkd->bqghd', p.astype(v_g.dtype), v_g)

    def chunk(t):
        return jnp.swapaxes(t.reshape(B, n_qb, Q_BLOCK, *t.shape[2:]), 0, 1)

    o_sel = lax.map(select_block, (chunk(q), chunk(sel_idx), pos.reshape(n_qb, Q_BLOCK)))
    o_sel = jnp.swapaxes(o_sel, 0, 1).reshape(B, S, G, HPG, dh)

    pad = ((0, 0), (NSA_WINDOW, 0), (0, 0), (0, 0))
    span = Q_BLOCK + NSA_WINDOW
    band = (jnp.arange(n_qb) * Q_BLOCK)[:, None] + jnp.arange(span)[None, :]
    k_band = jnp.pad(rms_norm(kv[:, :, 2, 0], k_gain[2]), pad)[:, band]
    v_band = jnp.pad(kv[:, :, 2, 1], pad)[:, band]
    q_b = q.reshape(B, n_qb, Q_BLOCK, G, HPG, dh)
    s_win = jnp.einsum('bnqghd,bnkgd->bnghqk', q_b, k_band) * scale
    k_pos = (band - NSA_WINDOW)[:, None, :]
    q_pos = pos.reshape(n_qb, Q_BLOCK)[:, :, None]
    win_ok = (k_pos >= 0) & (k_pos <= q_pos) & (k_pos > q_pos - NSA_WINDOW)
    p_win = masked_softmax(s_win, win_ok[None, :, None, None])
    o_win = jnp.einsum('bnghqk,bnkgd->bnqghd', p_win.astype(v_band.dtype), v_band)
    o_win = o_win.reshape(B, S, G, HPG, dh)

    g = gates.reshape(B, S, G, HPG, NSA_BRANCHES, 1)
    o = g[..., 0, :] * o_cmp + g[..., 1, :] * o_sel + g[..., 2, :] * o_win
    return o.reshape(B, S, H * dh)


def hybrid_attention_mixer(xn, layer, w_in, gate_bias, diff_q_gain, diff_k_gain, diff_lam,
                           diff_subln, nsa_q_gain, nsa_k_gain, cmp_pe, cmp_w1, cmp_w2, w_out):
    B, S, _ = xn.shape
    h = xn @ w_in
    offs = np.cumsum([0, DIFF_QK_COLS, DIFF_QK_COLS, DIFF_V_COLS, NSA_Q_COLS, NSA_KV_COLS, NSA_GATE_COLS])
    d_q, d_k, d_v, n_q, n_kv, n_g = (h[..., int(offs[j]):int(offs[j + 1])] for j in range(6))
    lambda_init = 0.8 - 0.6 * math.exp(-0.3 * layer)
    diff_out = diff_attention(
        d_q.reshape(B, S, DIFF_HEADS, 2, DIFF_QK_DIM),
        d_k.reshape(B, S, DIFF_HEADS, 2, DIFF_QK_DIM),
        d_v.reshape(B, S, DIFF_HEADS, DIFF_V_DIM),
        diff_q_gain, diff_k_gain, diff_lam, diff_subln, lambda_init)
    gates = jax.nn.sigmoid(n_g + gate_bias).reshape(B, S, NSA_HEADS, NSA_BRANCHES)
    nsa_out = nsa_attention(
        n_q.reshape(B, S, NSA_HEADS, HEAD_DIM),
        n_kv.reshape(B, S, NSA_BRANCHES, 2, NSA_KV_GROUPS, HEAD_DIM),
        gates, nsa_q_gain, nsa_k_gain, cmp_pe, cmp_w1, cmp_w2)
    return jnp.concatenate([diff_out, nsa_out], axis=-1) @ w_out


def rwkv7_time_mix(xn, mix, w_r, w_k, w_v, w0, w1, w2, a0, a1, a2, g1, g2, k_k, k_a, r_k,
                   ln_w, ln_b, w_o):
    B, S, D = xn.shape
    H, N = RWKV_HEADS, RWKV_HEAD
    f32 = jnp.float32
    dx = jnp.pad(xn, ((0, 0), (1, 0), (0, 0)))[:, :-1] - xn
    xr, xw, xk, xv, xa, xg = (xn + dx * mix[j] for j in range(6))
    r = xr @ w_r
    k = xk @ w_k
    v = xv @ w_v
    w = -jax.nn.softplus(-(w0 + jnp.tanh(xw @ w1) @ w2).astype(f32)) - 0.5
    a = jax.nn.sigmoid((a0 + (xa @ a1) @ a2).astype(f32))
    g = jax.nn.sigmoid(xg @ g1) @ g2

    def heads(t):
        return t.astype(f32).reshape(B, S, H, N)

    kk = heads(k * k_k)
    kk = kk / jnp.maximum(jnp.linalg.norm(kk, axis=-1, keepdims=True), 1e-12)
    k = k.astype(f32) * (1.0 + (a - 1.0) * k_a.astype(f32))
    r_h, k_h, v_h, a_h = heads(r), heads(k), heads(v), heads(a)
    decay = jnp.exp(-jnp.exp(heads(w)))

    def step(state, inp):
        r_t, d_t, k_t, v_t, kk_t, a_t = inp
        s_kk = jnp.einsum('bhvk,bhk->bhv', state, -kk_t)
        state = (state * d_t[:, :, None, :]
                 + s_kk[..., None] * (kk_t * a_t)[:, :, None, :]
                 + v_t[..., None] * k_t[:, :, None, :])
        return state, jnp.einsum('bhvk,bhk->bhv', state, r_t)

    def tm(t):
        return jnp.moveaxis(t, 1, 0)

    _, y = lax.scan(step, jnp.zeros((B, H, N, N), f32),
                    (tm(r_h), tm(decay), tm(k_h), tm(v_h), tm(kk), tm(a_h)))
    y = jnp.moveaxis(y, 0, 1)
    mu = jnp.mean(y, axis=-1, keepdims=True)
    var = jnp.mean(jnp.square(y - mu), axis=-1, keepdims=True)
    y = ((y - mu) * lax.rsqrt(var + RWKV_LNX_EPS)).reshape(B, S, D) * ln_w + ln_b
    bonus = jnp.sum(r_h * k_h * r_k.astype(f32).reshape(H, N), axis=-1, keepdims=True) * v_h
    y = y + bonus.reshape(B, S, D)
    return (y * g.astype(f32)).astype(xn.dtype) @ w_o


def hier_moe(xn, wg, bg, we, be, e_gate, e_up, e_down):
    B, S, D = xn.shape
    f32 = jnp.float32
    xf = xn.reshape(-1, D)
    n_tok = xf.shape[0]
    grp_prob = jax.nn.softmax((xf @ wg + bg).astype(f32), axis=-1)
    grp_p, grp_idx = lax.top_k(grp_prob, 1)
    exp_logits = (xf @ we + be).astype(f32).reshape(n_tok, N_GROUPS, EXPERTS_PER_GROUP)
    in_grp = jnp.take_along_axis(exp_logits, grp_idx[:, :, None], axis=1)[:, 0]
    top_val, top_idx = lax.top_k(in_grp, TOP_K_IN_GROUP)
    gate = grp_p * jax.nn.softmax(top_val, axis=-1)
    expert = grp_idx * EXPERTS_PER_GROUP + top_idx

    n_pairs = n_tok * TOP_K_IN_GROUP
    flat_e = expert.reshape(-1)
    order = jnp.argsort(flat_e)
    e_sorted = flat_e[order]
    tok_sorted = order // TOP_K_IN_GROUP
    gate_sorted = gate.reshape(-1)[order]
    counts = jnp.bincount(flat_e, length=N_EXPERTS)
    padded = (counts + MOE_BLOCK - 1) // MOE_BLOCK * MOE_BLOCK
    pad_end = jnp.cumsum(padded)
    pad_start = pad_end - padded
    start = jnp.cumsum(counts) - counts
    dest = pad_start[e_sorted] + jnp.arange(n_pairs) - start[e_sorted]
    n_blocks = -(-n_pairs // MOE_BLOCK) + N_EXPERTS
    slot_tok = jnp.full((n_blocks * MOE_BLOCK,), n_tok, jnp.int32).at[dest].set(tok_sorted)
    block_e = jnp.minimum(jnp.searchsorted(pad_end, jnp.arange(n_blocks) * MOE_BLOCK, side='right'),
                          N_EXPERTS - 1)
    x_pad = jnp.concatenate([xf, jnp.zeros((1, D), xf.dtype)], axis=0)
    xb = x_pad[slot_tok].reshape(n_blocks, MOE_BLOCK, D)

    def run_block(args):
        x_blk, e = args
        hid = jax.nn.silu(x_blk @ e_gate[e]) * (x_blk @ e_up[e])
        return hid @ e_down[e]

    y_slots = lax.map(run_block, (xb, block_e)).reshape(n_blocks * MOE_BLOCK, D)
    y = jnp.zeros_like(xf).at[tok_sorted].add(y_slots[dest] * gate_sorted[:, None].astype(xf.dtype))
    return y.reshape(B, S, D)


def setup_inputs(seed: int = 0) -> dict:
    key = jax.random.key(seed)
    keys = jax.random.split(key, 48)
    ks = iter([keys[j] for j in range(48)])
    n_a = (DEPTH + 1) // 2
    n_r = DEPTH // 2
    D = D_MODEL
    L = NSA_CMP_BLOCK

    def nrm(shape, scale):
        return jax.random.normal(next(ks), shape, jnp.float32) * scale

    def gain(shape):
        return 1.0 + nrm(shape, 0.05)

    def unif(shape, lo, hi):
        return jax.random.uniform(next(ks), shape, jnp.float32, lo, hi)

    return {
        'x': nrm((BATCH, SEQ, D), 1.0),
        'mix_norm': gain((DEPTH, D)),
        'attn_w_in': nrm((n_a, D, IN_COLS), D ** -0.5),
        'attn_gate_bias': nrm((n_a, NSA_GATE_COLS), 0.1),
        'diff_q_norm': gain((n_a, DIFF_QK_DIM)),
        'diff_k_norm': gain((n_a, DIFF_QK_DIM)),
        'diff_lambda': nrm((n_a, 4, DIFF_QK_DIM), 0.1),
        'diff_subln': gain((n_a, DIFF_V_DIM)),
        'nsa_q_norm': gain((n_a, HEAD_DIM)),
        'nsa_k_norm': gain((n_a, NSA_BRANCHES, HEAD_DIM)),
        'nsa_cmp_pe': nrm((n_a, 2, L, HEAD_DIM), 0.1),
        'nsa_cmp_w1': nrm((n_a, 2, L * HEAD_DIM, HEAD_DIM), (L * HEAD_DIM) ** -0.5),
        'nsa_cmp_w2': nrm((n_a, 2, HEAD_DIM, HEAD_DIM), HEAD_DIM ** -0.5),
        'attn_w_out': nrm((n_a, ATTN_OUT, D), ATTN_OUT ** -0.5),
        'rwkv_mix': unif((n_r, 6, D), 0.0, 1.0),
        'rwkv_w_r': nrm((n_r, D, D), D ** -0.5),
        'rwkv_w_k': nrm((n_r, D, D), D ** -0.5),
        'rwkv_w_v': nrm((n_r, D, D), D ** -0.5),
        'rwkv_decay_w0': unif((n_r, D), -5.0, 0.0),
        'rwkv_decay_w1': nrm((n_r, D, RWKV_DECAY_LORA), D ** -0.5),
        'rwkv_decay_w2': nrm((n_r, RWKV_DECAY_LORA, D), 0.1 * RWKV_DECAY_LORA ** -0.5),
        'rwkv_iclr_a0': nrm((n_r, D), 0.1),
        'rwkv_iclr_a1': nrm((n_r, D, RWKV_ICLR_LORA), D ** -0.5),
        'rwkv_iclr_a2': nrm((n_r, RWKV_ICLR_LORA, D), 0.1 * RWKV_ICLR_LORA ** -0.5),
        'rwkv_gate_g1': nrm((n_r, D, RWKV_GATE_LORA), D ** -0.5),
        'rwkv_gate_g2': nrm((n_r, RWKV_GATE_LORA, D), RWKV_GATE_LORA ** -0.5),
        'rwkv_k_k': 0.85 + nrm((n_r, D), 0.05),
        'rwkv_k_a': gain((n_r, D)),
        'rwkv_r_k': nrm((n_r, D), 0.1),
        'rwkv_ln_w': gain((n_r, D)),
        'rwkv_ln_b': nrm((n_r, D), 0.01),
        'rwkv_w_o': nrm((n_r, D, D), D ** -0.5),
        'ffn_norm': gain((DEPTH, D)),
        'router_group_w': nrm((DEPTH, D, N_GROUPS), D ** -0.5),
        'router_group_b': nrm((DEPTH, N_GROUPS), 0.01),
        'router_expert_w': nrm((DEPTH, D, N_EXPERTS), D ** -0.5),
        'router_expert_b': nrm((DEPTH, N_EXPERTS), 0.01),
        'expert_w_gate': nrm((DEPTH, N_EXPERTS, D, D_EXPERT), D ** -0.5),
        'expert_w_up': nrm((DEPTH, N_EXPERTS, D, D_EXPERT), D ** -0.5),
        'expert_w_down': nrm((DEPTH, N_EXPERTS, D_EXPERT, D), D_EXPERT ** -0.5),
    }


def reference(x, mix_norm, attn_w_in, attn_gate_bias, diff_q_norm, diff_k_norm, diff_lambda,
              diff_subln, nsa_q_norm, nsa_k_norm, nsa_cmp_pe, nsa_cmp_w1, nsa_cmp_w2, attn_w_out,
              rwkv_mix, rwkv_w_r, rwkv_w_k, rwkv_w_v, rwkv_decay_w0, rwkv_decay_w1, rwkv_decay_w2,
              rwkv_iclr_a0, rwkv_iclr_a1, rwkv_iclr_a2, rwkv_gate_g1, rwkv_gate_g2, rwkv_k_k,
              rwkv_k_a, rwkv_r_k, rwkv_ln_w, rwkv_ln_b, rwkv_w_o, ffn_norm, router_group_w,
              router_group_b, router_expert_w, router_expert_b, expert_w_gate, expert_w_up,
              expert_w_down):
    for layer in range(DEPTH):
        i = layer // 2
        xn = rms_norm(x, mix_norm[layer])
        if layer % 2 == 0:
            x = x + hybrid_attention_mixer(
                xn, layer, attn_w_in[i], attn_gate_bias[i], diff_q_norm[i], diff_k_norm[i],
                diff_lambda[i], diff_subln[i], nsa_q_norm[i], nsa_k_norm[i], nsa_cmp_pe[i],
                nsa_cmp_w1[i], nsa_cmp_w2[i], attn_w_out[i])
        else:
            x = x + rwkv7_time_mix(
                xn, rwkv_mix[i], rwkv_w_r[i], rwkv_w_k[i], rwkv_w_v[i], rwkv_decay_w0[i],
                rwkv_decay_w1[i], rwkv_decay_w2[i], rwkv_iclr_a0[i], rwkv_iclr_a1[i],
                rwkv_iclr_a2[i], rwkv_gate_g1[i], rwkv_gate_g2[i], rwkv_k_k[i], rwkv_k_a[i],
                rwkv_r_k[i], rwkv_ln_w[i], rwkv_ln_b[i], rwkv_w_o[i])
        x = x + hier_moe(
            rms_norm(x, ffn_norm[layer]), router_group_w[layer], router_group_b[layer],
            router_expert_w[layer], router_expert_b[layer], expert_w_gate[layer],
            expert_w_up[layer], expert_w_down[layer])
    return x
```

```python
import functools
import math

import jax
import jax.numpy as jnp
from jax import lax
from jax.experimental import pallas as pl
from jax.experimental.pallas import tpu as pltpu

F32 = jnp.float32
BF16 = jnp.bfloat16

D_MODEL = 1024
HEAD_DIM = 64
NORM_EPS = 1e-6
NEG_INF = -1e30
FORCE_SCORE = 1e4

DIFF_HEADS = 8
DIFF_QK_DIM = 32
NSA_HEADS = 8
NSA_KV_GROUPS = 2
NSA_HPG = 4
NSA_CMP_BLOCK = 32
NSA_CMP_STRIDE = 16
NSA_SEL_BLOCK = 64
NSA_TOP_N = 16
NSA_WINDOW = 512
IN_COLS = 2840
IN_COLS_PAD = 2944
GATE_COLS = 24

RWKV_HEAD = 64
RWKV_LNX_EPS = 64e-5
RWKV_CHUNK = 64

N_GROUPS = 4
EXPERTS_PER_GROUP = 8
N_EXPERTS = 32
D_EXPERT = 256
MOE_ROWS = 256

LANES = 128
VMEM_LIMIT = 56 * 1024 * 1024


def _bf(x):
    return x.astype(BF16)


def _dot(a, b):
    return jnp.dot(a, b, preferred_element_type=F32)


def _dot_nt(a, b):
    return lax.dot_general(a, b, (((1,), (1,)), ((), ())), preferred_element_type=F32)


def _dot_tn(a, b):
    return lax.dot_general(a, b, (((0,), (0,)), ((), ())), preferred_element_type=F32)


def _split2(x):
    hi = _bf(x)
    lo = _bf(x - hi.astype(F32))
    return hi, lo


def _params(sem):
    return pltpu.CompilerParams(dimension_semantics=sem, vmem_limit_bytes=VMEM_LIMIT)


def _block_diag(n, group):
    r = jnp.arange(n) // group
    return ((r[:, None] == r[None, :]).astype(F32) / group).astype(BF16)


def _rms(x, gain):
    return x * lax.rsqrt(jnp.mean(x * x, axis=-1, keepdims=True) + NORM_EPS) * gain


def _inproj_kernel(x_ref, g_ref, w_ref, b_ref, bd32_ref, bd64_ref, gq_ref, gk_ref, gnq_ref, gnk_ref,
                   dq_ref, dk_ref, dv_ref, nq_ref, nkv_ref, gt_ref):
    xn = _rms(x_ref[...], g_ref[...])
    h = _dot(_bf(xn), w_ref[...])

    def gnorm(seg, bd, gain):
        hi, lo = _split2(seg * seg)
        ms = _dot(hi, bd) + _dot(lo, bd)
        return seg * lax.rsqrt(ms + NORM_EPS) * gain

    dq_ref[...] = _bf(gnorm(h[:, 0:512], bd32_ref[...], gq_ref[...]))
    dk_ref[...] = _bf(gnorm(h[:, 512:1024], bd32_ref[...], gk_ref[...]))
    dv_ref[...] = _bf(h[:, 1024:1536])
    nq_ref[...] = _bf(gnorm(h[:, 1536:2048], bd64_ref[...], gnq_ref[...]))
    kv = h[:, 2048:2816]
    ksw = jnp.concatenate([kv[:, 256:384], kv[:, 512:640]], axis=1)
    kswn = gnorm(ksw, bd64_ref[0:256, 0:256], gnk_ref[...])
    nkv_ref[:, 0:256] = _bf(kv[:, 0:256])
    nkv_ref[:, 256:384] = _bf(kswn[:, 0:128])
    nkv_ref[:, 384:512] = _bf(kv[:, 384:512])
    nkv_ref[:, 512:640] = _bf(kswn[:, 128:256])
    nkv_ref[:, 640:768] = _bf(kv[:, 640:768])
    gt_ref[...] = jax.nn.sigmoid(h[:, 2816:2944] + b_ref[...])


def _attn_inproj(x2d, gain, w_in, gate_bias, dq_gain, dk_gain, nq_gain, nk_gain):
    T, D = x2d.shape
    tm = min(256, T)
    w = _bf(jnp.pad(w_in, ((0, 0), (0, IN_COLS_PAD - IN_COLS))))
    bias = jnp.pad(gate_bias, (0, LANES - GATE_COLS)).reshape(1, LANES)
    gq = (jnp.tile(dq_gain, 16) * DIFF_QK_DIM ** -0.5).reshape(1, 512)
    gk = jnp.tile(dk_gain, 16).reshape(1, 512)
    gnq = (jnp.tile(nq_gain, 8) * HEAD_DIM ** -0.5).reshape(1, 512)
    gnk = jnp.concatenate([jnp.tile(nk_gain[1], 2), jnp.tile(nk_gain[2], 2)]).reshape(1, 256)
    row = lambda n: pl.BlockSpec((tm, n), lambda i: (i, 0))
    full = lambda a: pl.BlockSpec(a.shape, lambda i: (0,) * a.ndim)
    consts = (gain.reshape(1, D), w, bias, _block_diag(512, 32), _block_diag(512, 64), gq, gk, gnq, gnk)
    return pl.pallas_call(
        _inproj_kernel,
        grid=(T // tm,),
        in_specs=[row(D)] + [full(a) for a in consts],
        out_specs=[row(512), row(512), row(512), row(512), row(768), row(LANES)],
        out_shape=[jax.ShapeDtypeStruct((T, 512), BF16)] * 4
        + [jax.ShapeDtypeStruct((T, 768), BF16), jax.ShapeDtypeStruct((T, LANES), F32)],
        compiler_params=_params(("arbitrary",)),
        name="attn_inproj",
    )(x2d, *consts)


def _softmax_step(s, v, m, l, acc):
    m_new = jnp.maximum(m, jnp.max(s, axis=-1, keepdims=True))
    alpha = jnp.exp(m - m_new)
    p = jnp.exp(s - m_new)
    l = alpha * l + jnp.sum(p, axis=-1, keepdims=True)
    acc = alpha * acc + _dot(_bf(p), v)
    return m_new, l, acc


def _softmax_init(rows, dv):
    return (jnp.full((rows, 1), NEG_INF, F32), jnp.zeros((rows, 1), F32), jnp.zeros((rows, dv), F32))


def _diff_attn_kernel(q_ref, kt_ref, v_ref, lam_ref, sg_ref, o_ref, *, tq, lambda_init):
    i = pl.program_id(1)
    q1 = q_ref[0, 0]
    q2 = q_ref[0, 1]

    def step(j, carry, causal):
        c1, c2 = carry
        v = v_ref[0, j]
        s1 = _dot(q1, kt_ref[0, 0, j])
        s2 = _dot(q2, kt_ref[0, 1, j])
        if causal is not None:
            s1 = jnp.where(causal, s1, NEG_INF)
            s2 = jnp.where(causal, s2, NEG_INF)
        return _softmax_step(s1, v, *c1), _softmax_step(s2, v, *c2)

    init = (_softmax_init(tq, HEAD_DIM), _softmax_init(tq, HEAD_DIM))
    carry = lax.fori_loop(0, i, lambda j, c: step(j, c, None), init)
    causal = lax.broadcasted_iota(jnp.int32, (tq, tq), 0) >= lax.broadcasted_iota(jnp.int32, (tq, tq), 1)
    (m1, l1, a1), (m2, l2, a2) = step(i, carry, causal)
    lp = lam_ref[...]
    lam = (jnp.exp(jnp.sum(lp[0:1] * lp[1:2], axis=-1, keepdims=True))
           - jnp.exp(jnp.sum(lp[2:3] * lp[3:4], axis=-1, keepdims=True)) + lambda_init)
    o = a1 / l1 - lam * (a2 / l2)
    o_ref[0] = _rms(o, sg_ref[...]) * (1.0 - lambda_init)


def _diff_attention(dq, dk, dv, lam_p, subln, lambda_init, B, S):
    tq = min(256, S)
    n = S // tq
    H = DIFF_HEADS
    q = dq.reshape(B, S, H, 2, DIFF_QK_DIM).transpose(0, 2, 3, 1, 4).reshape(B * H, 2, S, DIFF_QK_DIM)
    kt = dk.reshape(B, n, tq, H, 2, DIFF_QK_DIM).transpose(0, 3, 4, 1, 5, 2).reshape(B * H, 2, n, DIFF_QK_DIM, tq)
    v = dv.reshape(B, n, tq, H, HEAD_DIM).transpose(0, 3, 1, 2, 4).reshape(B * H, n, tq, HEAD_DIM)
    out = pl.pallas_call(
        functools.partial(_diff_attn_kernel, tq=tq, lambda_init=lambda_init),
        grid=(B * H, n),
        in_specs=[
            pl.BlockSpec((1, 2, tq, DIFF_QK_DIM), lambda b, i: (b, 0, i, 0)),
            pl.BlockSpec((1, 2, n, DIFF_QK_DIM, tq), lambda b, i: (b, 0, 0, 0, 0)),
            pl.BlockSpec((1, n, tq, HEAD_DIM), lambda b, i: (b, 0, 0, 0)),
            pl.BlockSpec((4, DIFF_QK_DIM), lambda b, i: (0, 0)),
            pl.BlockSpec((1, HEAD_DIM), lambda b, i: (0, 0)),
        ],
        out_specs=pl.BlockSpec((1, tq, HEAD_DIM), lambda b, i: (b, i, 0)),
        out_shape=jax.ShapeDtypeStruct((B * H, S, HEAD_DIM), F32),
        compiler_params=_params(("arbitrary", "arbitrary")),
        name="diff_attention",
    )(q, kt, v, lam_p, subln.reshape(1, HEAD_DIM))
    return out.reshape(B, H, S, HEAD_DIM).transpose(0, 2, 1, 3).reshape(B * S, H * HEAD_DIM)


def _nsa_compress_kernel(a_ref, w1a_ref, w1b_ref, pea_ref, peb_ref, w2_ref, kg_ref, o_ref, *, nch):
    kv = pl.program_id(1)
    a = a_ref[0, 0, 0]
    w1a = w1a_ref[0]
    w1b = w1b_ref[0]
    ya = _dot(a, w1a)
    yb = _dot(a, w1b)
    bias = _dot(pea_ref[0], w1a) + _dot(peb_ref[0], w1b)
    pre = ya + pltpu.roll(yb, nch - 1, 0) + bias
    out = _dot(_bf(jax.nn.gelu(pre)), w2_ref[0])
    o_ref[0, 0, 0] = jnp.where(kv == 0, _rms(out, kg_ref[...]), out)


def _nsa_compress(nkv, cmp_pe, cmp_w1, cmp_w2, k_gain0, B, S):
    nch = S // NSA_CMP_STRIDE
    G = NSA_KV_GROUPS
    half = NSA_CMP_STRIDE * HEAD_DIM
    a = nkv[:, 0:256].reshape(B, S, 2, G, HEAD_DIM).transpose(0, 2, 3, 1, 4).reshape(B, 2, G, nch, half)
    w1a = _bf(cmp_w1[:, :half])
    w1b = _bf(cmp_w1[:, half:])
    pea = _bf(cmp_pe[:, :NSA_CMP_STRIDE].reshape(2, 1, half))
    peb = _bf(cmp_pe[:, NSA_CMP_STRIDE:].reshape(2, 1, half))
    per_kv = lambda shp: pl.BlockSpec((1,) + shp, lambda b, kv, g: (kv, 0, 0))
    return pl.pallas_call(
        functools.partial(_nsa_compress_kernel, nch=nch),
        grid=(B, 2, G),
        in_specs=[
            pl.BlockSpec((1, 1, 1, nch, half), lambda b, kv, g: (b, kv, g, 0, 0)),
            per_kv((half, HEAD_DIM)), per_kv((half, HEAD_DIM)), per_kv((1, half)), per_kv((1, half)),
            per_kv((HEAD_DIM, HEAD_DIM)),
            pl.BlockSpec((1, HEAD_DIM), lambda b, kv, g: (0, 0)),
        ],
        out_specs=pl.BlockSpec((1, 1, 1, nch, HEAD_DIM), lambda b, kv, g: (b, kv, g, 0, 0)),
        out_shape=jax.ShapeDtypeStruct((B, 2, G, nch, HEAD_DIM), F32),
        compiler_params=_params(("arbitrary", "arbitrary", "arbitrary")),
        name="nsa_compress",
    )(a, w1a, w1b, pea, peb, _bf(cmp_w2), k_gain0.reshape(1, HEAD_DIM))


def _nsa_cmp_attn_kernel(q_ref, kt_ref, v_ref, o_ref, sel_ref, *, tq, nch, n_sel, top_n):
    i = pl.program_id(1)
    rows = NSA_HPG * tq
    n_cmp = nch - 1
    q = q_ref[0].reshape(rows, HEAD_DIM)
    s = _dot(q, kt_ref[0])
    pos = i * tq + (lax.broadcasted_iota(jnp.int32, (rows, nch), 0) & (tq - 1))
    c = lax.broadcasted_iota(jnp.int32, (rows, nch), 1)
    ok = jnp.where(c < n_cmp, c * NSA_CMP_STRIDE + (NSA_CMP_BLOCK - 1), 1 << 30) <= pos
    s = jnp.where(ok, s, NEG_INF)
    p = jnp.where(ok, jnp.exp(s - jnp.max(s, axis=-1, keepdims=True)), 0.0)
    l = jnp.sum(p, axis=-1, keepdims=True)
    p = p / jnp.where(l > 0.0, l, 1.0)
    o_ref[0] = _dot(_bf(p), v_ref[0]).reshape(NSA_HPG, tq, HEAD_DIM)

    pg = jnp.sum(p.reshape(NSA_HPG, tq, nch), axis=0)
    cc = lax.broadcasted_iota(jnp.int32, (nch, n_sel), 0) * NSA_CMP_STRIDE
    jj = lax.broadcasted_iota(jnp.int32, (nch, n_sel), 1) * NSA_SEL_BLOCK
    cover = jnp.where(cc < jj + NSA_SEL_BLOCK,
                      jnp.where(cc + NSA_CMP_BLOCK > jj, jnp.where(cc < n_cmp * NSA_CMP_STRIDE, 1.0, 0.0), 0.0), 0.0)
    cover = _bf(cover)
    hi, lo = _split2(pg)
    imp = _dot(hi, cover) + _dot(lo, cover)
    blk = lax.broadcasted_iota(jnp.int32, (tq, n_sel), 1)
    cur = (i * tq + lax.broadcasted_iota(jnp.int32, (tq, n_sel), 0)) >> 6
    imp = jnp.where(blk == cur, FORCE_SCORE, jnp.where(blk == 0, FORCE_SCORE, jnp.where(blk > cur, NEG_INF, imp)))
    blk_f = blk.astype(F32)
    sel = jnp.zeros((tq, n_sel), F32)
    for _ in range(top_n):
        mx = jnp.max(imp, axis=-1, keepdims=True)
        first = jnp.min(jnp.where(imp == mx, blk_f, float(n_sel)), axis=-1, keepdims=True)
        hit = blk_f == first
        sel = jnp.where(hit, 1.0, sel)
        imp = jnp.where(hit, -jnp.inf, imp)
    sel_ref[0] = _bf(sel)


def _nsa_cmp_attention(q, kc_t, vc, B, S):
    BG = q.shape[0]
    tq = min(256, S)
    nch = S // NSA_CMP_STRIDE
    n_sel = S // NSA_SEL_BLOCK
    top_n = min(NSA_TOP_N, n_sel)
    return pl.pallas_call(
        functools.partial(_nsa_cmp_attn_kernel, tq=tq, nch=nch, n_sel=n_sel, top_n=top_n),
        grid=(BG, S // tq),
        in_specs=[
            pl.BlockSpec((1, NSA_HPG, tq, HEAD_DIM), lambda b, i: (b, 0, i, 0)),
            pl.BlockSpec((1, HEAD_DIM, nch), lambda b, i: (b, 0, 0)),
            pl.BlockSpec((1, nch, HEAD_DIM), lambda b, i: (b, 0, 0)),
        ],
        out_specs=[
            pl.BlockSpec((1, NSA_HPG, tq, HEAD_DIM), lambda b, i: (b, 0, i, 0)),
            pl.BlockSpec((1, tq, n_sel), lambda b, i: (b, i, 0)),
        ],
        out_shape=[jax.ShapeDtypeStruct((BG, NSA_HPG, S, HEAD_DIM), F32),
                   jax.ShapeDtypeStruct((BG, S, n_sel), BF16)],
        compiler_params=_params(("arbitrary", "arbitrary")),
        name="nsa_cmp_attention",
    )(q, kc_t, vc)


def _nsa_sel_win_kernel(q_ref, kst_ref, vs_ref, kwt_ref, vw_ref, sel_ref, os_ref, ow_ref, *, tq, n_sel):
    i = pl.program_id(1)
    rows = NSA_HPG * tq
    q = q_ref[0].reshape(rows, HEAD_DIM)
    selm = sel_ref[0]
    qpos = i * tq + lax.broadcasted_iota(jnp.int32, (tq, tq), 0)
    kloc = lax.broadcasted_iota(jnp.int32, (tq, tq), 1)
    blk_row = lax.broadcasted_iota(jnp.int32, (n_sel, tq), 0)
    blk_col = lax.broadcasted_iota(jnp.int32, (n_sel, tq), 1)

    def masked(s, mask):
        return jnp.where(mask[None], s.reshape(NSA_HPG, tq, tq), NEG_INF).reshape(rows, tq)

    def sel_step(j, carry):
        expand = _bf(jnp.where(blk_row == ((j * tq + blk_col) >> 6), 1.0, 0.0))
        chosen = _dot(selm, expand)
        mask = jnp.where(j * tq + kloc <= qpos, chosen, 0.0) > 0.5
        s = masked(_dot(q, kst_ref[0, j]), mask)
        return _softmax_step(s, vs_ref[0, j], *carry)

    m, l, acc = lax.fori_loop(0, i + 1, sel_step, _softmax_init(rows, HEAD_DIM))
    os_ref[0] = (acc / l).reshape(NSA_HPG, tq, HEAD_DIM)

    def win_step(j, carry):
        dist = qpos - (j * tq + kloc)
        mask = dist.astype(jnp.uint32) < jnp.uint32(NSA_WINDOW)
        s = masked(_dot(q, kwt_ref[0, j]), mask)
        return _softmax_step(s, vw_ref[0, j], *carry)

    first = jnp.maximum(i - (NSA_WINDOW - 1 + tq - 1) // tq, 0)
    m, l, acc = lax.fori_loop(first, i + 1, win_step, _softmax_init(rows, HEAD_DIM))
    ow_ref[0] = (acc / l).reshape(NSA_HPG, tq, HEAD_DIM)


def _nsa_sel_win(q, kst, vs, kwt, vw, sel, B, S):
    BG = q.shape[0]
    tq = min(256, S)
    n = S // tq
    n_sel = S // NSA_SEL_BLOCK
    qspec = pl.BlockSpec((1, NSA_HPG, tq, HEAD_DIM), lambda b, i: (b, 0, i, 0))
    ktspec = pl.BlockSpec((1, n, HEAD_DIM, tq), lambda b, i: (b, 0, 0, 0))
    vspec = pl.BlockSpec((1, n, tq, HEAD_DIM), lambda b, i: (b, 0, 0, 0))
    return pl.pallas_call(
        functools.partial(_nsa_sel_win_kernel, tq=tq, n_sel=n_sel),
        grid=(BG, n),
        in_specs=[qspec, ktspec, vspec, ktspec, vspec, pl.BlockSpec((1, tq, n_sel), lambda b, i: (b, i, 0))],
        out_specs=[qspec, qspec],
        out_shape=[jax.ShapeDtypeStruct((BG, NSA_HPG, S, HEAD_DIM), F32)] * 2,
        compiler_params=_params(("arbitrary", "arbitrary")),
        name="nsa_sel_win_attention",
    )(q, kst, vs, kwt, vw, sel)


def _attn_outproj_kernel(x_ref, d_ref, oc_ref, os_ref, ow_ref, gt_ref, ge_ref, w_ref, o_ref):
    ghi, glo = _split2(gt_ref[...])

    def gate(r):
        return _dot(ghi, ge_ref[r]) + _dot(glo, ge_ref[r])

    nsa = gate(0) * oc_ref[...] + gate(1) * os_ref[...] + gate(2) * ow_ref[...]
    y = _dot(_bf(d_ref[...]), w_ref[0:512, :]) + _dot(_bf(nsa), w_ref[512:1024, :])
    o_ref[...] = x_ref[...] + y


def _attn_outproj(x2d, diff_o, o_cmp, o_sel, o_win, gates, w_out):
    T, D = x2d.shape
    tm = min(256, T)
    col = jnp.arange(512) // HEAD_DIM
    src = jnp.arange(LANES)
    ge = jnp.stack([_bf((src[:, None] == col[None, :] * 3 + r).astype(F32)) for r in range(3)])
    row = lambda n: pl.BlockSpec((tm, n), lambda i: (i, 0))
    return pl.pallas_call(
        _attn_outproj_kernel,
        grid=(T // tm,),
        in_specs=[row(D), row(512), row(512), row(512), row(512), row(LANES),
                  pl.BlockSpec((3, LANES, 512), lambda i: (0, 0, 0)),
                  pl.BlockSpec((D, D), lambda i: (0, 0))],
        out_specs=row(D),
        out_shape=jax.ShapeDtypeStruct((T, D), F32),
        compiler_params=_params(("arbitrary",)),
        name="attn_outproj",
    )(x2d, diff_o, o_cmp, o_sel, o_win, gates, ge, _bf(w_out))


def _hybrid_attention_layer(x, layer, norm_gain, w_in, gate_bias, dq_gain, dk_gain, lam_p, subln, nq_gain, nk_gain,
                            cmp_pe, cmp_w1, cmp_w2, w_out):
    B, S, D = x.shape
    T = B * S
    G = NSA_KV_GROUPS
    x2d = x.reshape(T, D)
    dq, dk, dv, nq, nkv, gates = _attn_inproj(x2d, norm_gain, w_in, gate_bias, dq_gain, dk_gain, nq_gain, nk_gain)
    lambda_init = 0.8 - 0.6 * math.exp(-0.3 * layer)
    diff_o = _diff_attention(dq, dk, dv, lam_p, subln, lambda_init, B, S)

    cmp = _nsa_compress(nkv, cmp_pe, cmp_w1, cmp_w2, nk_gain[0], B, S)
    nch = S // NSA_CMP_STRIDE
    kc_t = _bf(cmp[:, 0]).transpose(0, 1, 3, 2).reshape(B * G, HEAD_DIM, nch)
    vc = _bf(cmp[:, 1]).reshape(B * G, nch, HEAD_DIM)
    q = nq.reshape(B, S, G, NSA_HPG, HEAD_DIM).transpose(0, 2, 3, 1, 4).reshape(B * G, NSA_HPG, S, HEAD_DIM)
    o_cmp, sel = _nsa_cmp_attention(q, kc_t, vc, B, S)

    tq = min(256, S)
    n = S // tq

    def kt_tiles(cols):
        return cols.reshape(B, n, tq, G, HEAD_DIM).transpose(0, 3, 1, 4, 2).reshape(B * G, n, HEAD_DIM, tq)

    def v_tiles(cols):
        return cols.reshape(B, n, tq, G, HEAD_DIM).transpose(0, 3, 1, 2, 4).reshape(B * G, n, tq, HEAD_DIM)

    o_sel, o_win = _nsa_sel_win(q, kt_tiles(nkv[:, 256:384]), v_tiles(nkv[:, 384:512]),
                                kt_tiles(nkv[:, 512:640]), v_tiles(nkv[:, 640:768]), sel, B, S)

    def tokens(o):
        return o.reshape(B, G, NSA_HPG, S, HEAD_DIM).transpose(0, 3, 1, 2, 4).reshape(T, NSA_HEADS * HEAD_DIM)

    out = _attn_outproj(x2d, diff_o, tokens(o_cmp), tokens(o_sel), tokens(o_win), gates, w_out)
    return out.reshape(B, S, D)


def _softplus(z):
    return jnp.maximum(z, 0.0) + jnp.log(1.0 + jnp.exp(-jnp.abs(z)))


def _rwkv_proj_kernel(x_ref, xp_ref, g_ref, mix_ref, wr_ref, wk_ref, wv_ref, w1_ref, w2_ref, a1_ref, a2_ref,
                      g1_ref, g2_ref, vec_ref, bd_ref,
                      r_ref, wl_ref, k_ref, v_ref, kk_ref, b_ref, bonus_ref, gate_ref):
    i = pl.program_id(1)
    gain = g_ref[...]
    xn = _rms(x_ref[0], gain)
    prev = _rms(xp_ref[0], gain)[7:8] * jnp.where(i > 0, 1.0, 0.0)
    shifted = pltpu.roll(xn, 1, 0)
    first_row = lax.broadcasted_iota(jnp.int32, xn.shape, 0) == 0
    dx = jnp.where(first_row, prev, shifted) - xn
    mix = mix_ref[...]
    xr, xw, xk, xv, xa, xg = (_bf(xn + dx * mix[j:j + 1]) for j in range(6))
    vec = vec_ref[...]
    w0, a0, k_k, k_a, r_k = (vec[j:j + 1] for j in range(5))
    r = _dot(xr, wr_ref[...])
    k = _dot(xk, wk_ref[...])
    v = _dot(xv, wv_ref[...])
    w = -_softplus(-(w0 + _dot(_bf(jnp.tanh(_dot(xw, w1_ref[...]))), w2_ref[...]))) - 0.5
    a = jax.nn.sigmoid(a0 + _dot(_bf(_dot(xa, a1_ref[...])), a2_ref[...]))
    gate_ref[0] = _dot(_bf(jax.nn.sigmoid(_dot(xg, g1_ref[...]))), g2_ref[...])
    bd = bd_ref[...]
    kk = k * k_k
    kk = kk / jnp.maximum(jnp.sqrt(_dot(_bf(kk * kk), bd)), 1e-12)
    k = k * (1.0 + (a - 1.0) * k_a)
    r_ref[0] = r
    wl_ref[0] = -jnp.exp(w)
    k_ref[0] = k
    v_ref[0] = v
    kk_ref[0] = kk
    b_ref[0] = kk * a
    bonus_ref[0] = _dot(_bf(r * k * r_k), bd) * v


def _rwkv_proj(x, gain, mix, w_r, w_k, w_v, w0, w1, w2, a0, a1, a2, g1, g2, k_k, k_a, r_k):
    B, S, D = x.shape
    tm = min(256, S)
    pad_c = lambda m, n: _bf(jnp.pad(m, ((0, 0), (0, n - m.shape[1]))))
    pad_r = lambda m, n: _bf(jnp.pad(m, ((0, n - m.shape[0]), (0, 0))))
    lw = LANES
    lg = 2 * LANES
    consts = (gain.reshape(1, D), jnp.pad(mix, ((0, 2), (0, 0))), _bf(w_r), _bf(w_k), _bf(w_v),
              pad_c(w1, lw), pad_r(w2, lw), pad_c(a1, lw), pad_r(a2, lw), pad_c(g1, lg), pad_r(g2, lg),
              jnp.pad(jnp.stack([w0, a0, k_k, k_a, r_k]), ((0, 3), (0, 0))),
              (_block_diag(D, RWKV_HEAD).astype(F32) * RWKV_HEAD).astype(BF16))
    full = lambda a: pl.BlockSpec(a.shape, lambda b, i: (0,) * a.ndim)
    tile = pl.BlockSpec((1, tm, D), lambda b, i: (b, i, 0))
    return pl.pallas_call(
        _rwkv_proj_kernel,
        grid=(B, S // tm),
        in_specs=[tile, pl.BlockSpec((1, 8, D), lambda b, i: (b, jnp.maximum(i * (tm // 8) - 1, 0), 0))]
        + [full(a) for a in consts],
        out_specs=[tile] * 8,
        out_shape=[jax.ShapeDtypeStruct((B, S, D), F32)] * 8,
        compiler_params=_params(("arbitrary", "arbitrary")),
        name="rwkv_proj",
    )(x, x, *consts)


def _rwkv_chunk_kernel(r_ref, wl_ref, k_ref, v_ref, kk_ref, b_ref, m_ref, g0_ref, rq_ref, y0_ref, *, cpb):
    C = RWKV_CHUNK
    lane = lax.broadcasted_iota(jnp.int32, (C, LANES), 1)
    head0 = lane < RWKV_HEAD
    ti = lax.broadcasted_iota(jnp.int32, (2 * C, 2 * C), 0) & (C - 1)
    tj = lax.broadcasted_iota(jnp.int32, (2 * C, 2 * C), 1) & (C - 1)
    strict = ti > tj
    incl = ti >= tj
    eye = lax.broadcasted_iota(jnp.int32, (LANES, LANES), 0) == lax.broadcasted_iota(jnp.int32, (LANES, LANES), 1)
    ltri = _bf(jnp.where(lax.broadcasted_iota(jnp.int32, (C, C), 0) >= lax.broadcasted_iota(jnp.int32, (C, C), 1),
                         1.0, 0.0))

    def stack(x):
        return jnp.concatenate([jnp.where(head0, x, 0.0), jnp.where(head0, 0.0, x)], axis=0)

    for c in range(cpb):
        rows = slice(c * C, (c + 1) * C)
        wl = wl_ref[0, rows, :]
        h1 = _bf(wl)
        r1 = wl - h1.astype(F32)
        h2 = _bf(r1)
        h3 = _bf(r1 - h2.astype(F32))
        cum = _dot(ltri, h1) + _dot(ltri, h2) + _dot(ltri, h3)
        total = cum[C - 1:C, :]
        p_in = jnp.exp(cum)
        p_ex = jnp.exp(cum - wl)
        p_inv = jnp.exp(-cum)
        p_end = jnp.exp(total - cum)
        kk = kk_ref[0, rows, :]
        bb = b_ref[0, rows, :]
        kx = k_ref[0, rows, :]
        lhs_kk = stack(kk * p_ex)
        lhs_r = stack(r_ref[0, rows, :] * p_in)
        vs = _bf(stack(v_ref[0, rows, :]))
        a = _dot_nt(_bf(jnp.concatenate([lhs_kk, lhs_r], axis=0)),
                    _bf(jnp.concatenate([stack(bb * p_inv), stack(kx * p_inv)], axis=0)))
        n1 = _bf(jnp.where(strict, a[0:2 * C, 0:2 * C], 0.0))
        a_k = _bf(jnp.where(strict, a[0:2 * C, 2 * C:4 * C], 0.0))
        a_rb = _bf(jnp.where(incl, a[2 * C:4 * C, 0:2 * C], 0.0))
        a_rk = _bf(jnp.where(incl, a[2 * C:4 * C, 2 * C:4 * C], 0.0))
        x = jnp.concatenate([_dot(a_k, vs), lhs_kk], axis=1)
        x = x - _dot(n1, _bf(x))
        npow = n1
        for _ in range(5):
            npow = _bf(_dot(npow, npow))
            x = x + _dot(npow, _bf(x))
        u0 = _bf(-x[:, 0:LANES])
        wmat = _bf(-x[:, LANES:2 * LANES])
        bc = _bf(stack(bb * p_end))
        kc = _bf(stack(kx * p_end))
        y0_ref[0, 0, c] = _dot(a_rk, vs) + _dot(a_rb, u0)
        rq_ref[0, 0, c] = lhs_r + _dot(a_rb, wmat)
        g0_ref[0, 0, c] = _dot_tn(bc, u0) + _dot_tn(kc, vs)
        m_ref[0, 0, c] = jnp.where(eye, jnp.exp(total), 0.0) + _dot_tn(bc, wmat)


def _rwkv_chunks(r, wl, k, v, kk, b):
    B, S, D = r.shape
    C = RWKV_CHUNK
    nc = S // C
    cpb = min(4, nc)
    hp = D // LANES
    tile = pl.BlockSpec((1, cpb * C, LANES), lambda bi, h, c: (bi, c, h))
    out = pl.BlockSpec((1, 1, cpb, LANES, LANES), lambda bi, h, c: (bi, h, c, 0, 0))
    return pl.pallas_call(
        functools.partial(_rwkv_chunk_kernel, cpb=cpb),
        grid=(B, hp, nc // cpb),
        in_specs=[tile] * 6,
        out_specs=[out] * 4,
        out_shape=[jax.ShapeDtypeStruct((B, hp, nc, LANES, LANES), F32)] * 4,
        compiler_params=_params(("arbitrary", "arbitrary", "arbitrary")),
        name="rwkv_chunk_summaries",
    )(r, wl, k, v, kk, b)


def _rwkv_scan_kernel(m_ref, g0_ref, rq_ref, y0_ref, y_ref, st_ref, *, cpb, hp):
    C = RWKV_CHUNK

    @pl.when(pl.program_id(1) == 0)
    def _():
        st_ref[...] = jnp.zeros_like(st_ref)

    for c in range(cpb):
        for h in range(hp):
            st = st_ref[h]
            s_hi, s_lo = _split2(st)
            y = y0_ref[0, h, c] + _dot(_bf(rq_ref[0, h, c]), s_hi)
            y_ref[0, c * C:(c + 1) * C, h * LANES:(h + 1) * LANES] = y[0:C] + y[C:2 * C]
            m_hi, m_lo = _split2(m_ref[0, h, c])
            st_ref[h] = g0_ref[0, h, c] + _dot(m_hi, s_hi) + _dot(m_hi, s_lo) + _dot(m_lo, s_hi)


def _rwkv_scan(m, g0, rq, y0, S):
    B, hp, nc = m.shape[:3]
    C = RWKV_CHUNK
    cpb = min(4, nc)
    blk = pl.BlockSpec((1, hp, cpb, LANES, LANES), lambda bi, c: (bi, 0, c, 0, 0))
    return pl.pallas_call(
        functools.partial(_rwkv_scan_kernel, cpb=cpb, hp=hp),
        grid=(B, nc // cpb),
        in_specs=[blk] * 4,
        out_specs=pl.BlockSpec((1, cpb * C, hp * LANES), lambda bi, c: (bi, c, 0)),
        out_shape=jax.ShapeDtypeStruct((B, S, hp * LANES), F32),
        scratch_shapes=[pltpu.VMEM((hp, LANES, LANES), F32)],
        compiler_params=_params(("arbitrary", "arbitrary")),
        name="rwkv_state_scan",
    )(m, g0, rq, y0)


def _rwkv_out_kernel(x_ref, y_ref, bonus_ref, gate_ref, lnw_ref, lnb_ref, bd_ref, wo_ref, o_ref):
    y = y_ref[...]
    bd = bd_ref[...]
    hi, lo = _split2(y)
    mu = _dot(hi, bd) + _dot(lo, bd)
    dev = y - mu
    hi, lo = _split2(dev * dev)
    var = _dot(hi, bd) + _dot(lo, bd)
    yn = dev * lax.rsqrt(var + RWKV_LNX_EPS) * lnw_ref[...] + lnb_ref[...]
    z = (yn + bonus_ref[...]) * gate_ref[...]
    o_ref[...] = x_ref[...] + _dot(_bf(z), wo_ref[...])


def _rwkv_out(x2d, y, bonus, gate, ln_w, ln_b, w_o):
    T, D = x2d.shape
    tm = min(256, T)
    row = pl.BlockSpec((tm, D), lambda i: (i, 0))
    vecs = pl.BlockSpec((1, D), lambda i: (0, 0))
    mat = pl.BlockSpec((D, D), lambda i: (0, 0))
    return pl.pallas_call(
        _rwkv_out_kernel,
        grid=(T // tm,),
        in_specs=[row, row, row, row, vecs, vecs, mat, mat],
        out_specs=row,
        out_shape=jax.ShapeDtypeStruct((T, D), F32),
        compiler_params=_params(("arbitrary",)),
        name="rwkv_out",
    )(x2d, y, bonus, gate, ln_w.reshape(1, D), ln_b.reshape(1, D), _block_diag(D, RWKV_HEAD), _bf(w_o))


def _rwkv_layer(x, norm_gain, mix, w_r, w_k, w_v, w0, w1, w2, a0, a1, a2, g1, g2, k_k, k_a, r_k, ln_w, ln_b, w_o):
    B, S, D = x.shape
    r, wl, k, v, kk, b, bonus, gate = _rwkv_proj(x, norm_gain, mix, w_r, w_k, w_v, w0, w1, w2, a0, a1, a2,
                                                 g1, g2, k_k, k_a, r_k)
    m, g0, rq, y0 = _rwkv_chunks(r, wl, k, v, kk, b)
    y = _rwkv_scan(m, g0, rq, y0, S)
    T = B * S
    out = _rwkv_out(x.reshape(T, D), y.reshape(T, D), bonus.reshape(T, D), gate.reshape(T, D), ln_w, ln_b, w_o)
    return out.reshape(B, S, D)


def _router_kernel(x_ref, g_ref, whi_ref, wlo_ref, b_ref, xn_ref, route_ref):
    xn = _rms(x_ref[...], g_ref[...])
    xn_ref[...] = _bf(xn)
    hi, lo = _split2(xn)
    logits = _dot(hi, whi_ref[...]) + _dot(hi, wlo_ref[...]) + _dot(lo, whi_ref[...]) + b_ref[...]
    lane = lax.broadcasted_iota(jnp.int32, logits.shape, 1)
    lane_f = lane.astype(F32)

    def top(vals):
        mx = jnp.max(vals, axis=-1, keepdims=True)
        return mx, jnp.min(jnp.where(vals == mx, lane_f, float(LANES)), axis=-1, keepdims=True)

    glog = jnp.where(lane < N_GROUPS, logits, NEG_INF)
    gmax, gidx = top(glog)
    gsum = jnp.sum(jnp.where(lane < N_GROUPS, jnp.exp(glog - gmax), 0.0), axis=-1, keepdims=True)
    grp_p = 1.0 / gsum
    first = N_GROUPS + EXPERTS_PER_GROUP * gidx.astype(jnp.int32)
    in_grp = (lane - first).astype(jnp.uint32) < jnp.uint32(EXPERTS_PER_GROUP)
    elog = jnp.where(in_grp, logits, NEG_INF)
    v1, i1 = top(elog)
    v2, i2 = top(jnp.where(lane_f == i1, NEG_INF, elog))
    e2 = jnp.exp(v2 - v1)
    gate1 = grp_p / (1.0 + e2)
    gate2 = grp_p * e2 / (1.0 + e2)
    route_ref[...] = jnp.where(lane == 0, i1 - N_GROUPS, jnp.where(lane == 1, i2 - N_GROUPS,
                               jnp.where(lane == 2, gate1, jnp.where(lane == 3, gate2, 0.0))))


def _router(x2d, gain, wg, bg, we, be):
    T, D = x2d.shape
    tm = min(256, T)
    w = jnp.pad(jnp.concatenate([wg, we], axis=1), ((0, 0), (0, LANES - N_GROUPS - N_EXPERTS)))
    w_hi = _bf(w)
    w_lo = _bf(w - w_hi.astype(F32))
    bias = jnp.pad(jnp.concatenate([bg, be]), (0, LANES - N_GROUPS - N_EXPERTS)).reshape(1, LANES)
    row = lambda n: pl.BlockSpec((tm, n), lambda i: (i, 0))
    full = lambda a: pl.BlockSpec(a.shape, lambda i: (0, 0))
    consts = (gain.reshape(1, D), w_hi, w_lo, bias)
    return pl.pallas_call(
        _router_kernel,
        grid=(T // tm,),
        in_specs=[row(D)] + [full(a) for a in consts],
        out_specs=[row(D), row(LANES)],
        out_shape=[jax.ShapeDtypeStruct((T, D), BF16), jax.ShapeDtypeStruct((T, LANES), F32)],
        compiler_params=_params(("arbitrary",)),
        name="moe_router",
    )(x2d, *consts)


def _expert_kernel(be_ref, nu_ref, x_ref, wgu_ref, wd_ref, o_ref):
    i = pl.program_id(0)

    @pl.when(i < nu_ref[0])
    def _():
        gu = _dot(x_ref[...], wgu_ref[0])
        hid = jax.nn.silu(gu[:, 0:D_EXPERT]) * gu[:, D_EXPERT:2 * D_EXPERT]
        o_ref[...] = _dot(_bf(hid), wd_ref[0])

    @pl.when(i >= nu_ref[0])
    def _():
        o_ref[...] = jnp.zeros_like(o_ref)


def _experts(xb, block_e, n_used, w_gu, w_d):
    rows, D = xb.shape
    n_blocks = rows // MOE_ROWS
    return pl.pallas_call(
        _expert_kernel,
        grid_spec=pltpu.PrefetchScalarGridSpec(
            num_scalar_prefetch=2,
            grid=(n_blocks,),
            in_specs=[
                pl.BlockSpec((MOE_ROWS, D), lambda i, be, nu: (i, 0)),
                pl.BlockSpec((1, D, 2 * D_EXPERT), lambda i, be, nu: (be[i], 0, 0)),
                pl.BlockSpec((1, D_EXPERT, D), lambda i, be, nu: (be[i], 0, 0)),
            ],
            out_specs=pl.BlockSpec((MOE_ROWS, D), lambda i, be, nu: (i, 0)),
        ),
        out_shape=jax.ShapeDtypeStruct((rows, D), F32),
        compiler_params=_params(("arbitrary",)),
        name="moe_experts",
    )(block_e, n_used, xb, w_gu, w_d)


def _combine_kernel(x_ref, y0_ref, y1_ref, route_ref, o_ref):
    route = route_ref[...]
    o_ref[...] = x_ref[...] + route[:, 2:3] * y0_ref[...] + route[:, 3:4] * y1_ref[...]


def _combine(x2d, y0, y1, route):
    T, D = x2d.shape
    tm = min(512, T)
    row = lambda n: pl.BlockSpec((tm, n), lambda i: (i, 0))
    return pl.pallas_call(
        _combine_kernel,
        grid=(T // tm,),
        in_specs=[row(D), row(D), row(D), row(LANES)],
        out_specs=row(D),
        out_shape=jax.ShapeDtypeStruct((T, D), F32),
        compiler_params=_params(("arbitrary",)),
        name="moe_combine",
    )(x2d, y0, y1, route)


def _moe_layer(x, gain, wg, bg, we, be, e_gate, e_up, e_down):
    B, S, D = x.shape
    T = B * S
    x2d = x.reshape(T, D)
    xn, route = _router(x2d, gain, wg, bg, we, be)
    flat_e = route[:, 0:2].astype(jnp.int32).reshape(-1)
    n_pairs = 2 * T
    onehot = (flat_e[:, None] == jnp.arange(N_EXPERTS, dtype=jnp.int32)[None, :]).astype(jnp.int32)
    csum = jnp.cumsum(onehot, axis=0)
    rank = jnp.take_along_axis(csum, flat_e[:, None], axis=1)[:, 0] - 1
    counts = csum[-1]
    padded = (counts + MOE_ROWS - 1) // MOE_ROWS * MOE_ROWS
    pad_end = jnp.cumsum(padded)
    dest = (pad_end - padded)[flat_e] + rank
    n_blocks = -(-n_pairs // MOE_ROWS) + N_EXPERTS
    slot_tok = jnp.zeros((n_blocks * MOE_ROWS,), jnp.int32).at[dest].set(jnp.arange(n_pairs, dtype=jnp.int32) // 2)
    block_e = jnp.minimum(jnp.searchsorted(pad_end, jnp.arange(n_blocks, dtype=jnp.int32) * MOE_ROWS, side='right'),
                          N_EXPERTS - 1).astype(jnp.int32)
    n_used = (pad_end[-1:] // MOE_ROWS).astype(jnp.int32)
    xb = jnp.take(xn, slot_tok, axis=0)
    y_slots = _experts(xb, block_e, n_used, _bf(jnp.concatenate([e_gate, e_up], axis=-1)), _bf(e_down))
    ys = jnp.take(y_slots, dest, axis=0).reshape(T, 2, D)
    return _combine(x2d, ys[:, 0], ys[:, 1], route).reshape(B, S, D)


def kernel(x, mix_norm, attn_w_in, attn_gate_bias, diff_q_norm, diff_k_norm, diff_lambda, diff_subln, nsa_q_norm, nsa_k_norm, nsa_cmp_pe, nsa_cmp_w1, nsa_cmp_w2, attn_w_out, rwkv_mix, rwkv_w_r, rwkv_w_k, rwkv_w_v, rwkv_decay_w0, rwkv_decay_w1, rwkv_decay_w2, rwkv_iclr_a0, rwkv_iclr_a1, rwkv_iclr_a2, rwkv_gate_g1, rwkv_gate_g2, rwkv_k_k, rwkv_k_a, rwkv_r_k, rwkv_ln_w, rwkv_ln_b, rwkv_w_o, ffn_norm, router_group_w, router_group_b, router_expert_w, router_expert_b, expert_w_gate, expert_w_up, expert_w_down):
    depth = mix_norm.shape[0]
    for layer in range(depth):
        i = layer // 2
        if layer % 2 == 0:
            x = _hybrid_attention_layer(
                x, layer, mix_norm[layer], attn_w_in[i], attn_gate_bias[i], diff_q_norm[i], diff_k_norm[i],
                diff_lambda[i], diff_subln[i], nsa_q_norm[i], nsa_k_norm[i], nsa_cmp_pe[i], nsa_cmp_w1[i],
                nsa_cmp_w2[i], attn_w_out[i])
        else:
            x = _rwkv_layer(
                x, mix_norm[layer], rwkv_mix[i], rwkv_w_r[i], rwkv_w_k[i], rwkv_w_v[i], rwkv_decay_w0[i],
                rwkv_decay_w1[i], rwkv_decay_w2[i], rwkv_iclr_a0[i], rwkv_iclr_a1[i], rwkv_iclr_a2[i],
                rwkv_gate_g1[i], rwkv_gate_g2[i], rwkv_k_k[i], rwkv_k_a[i], rwkv_r_k[i], rwkv_ln_w[i],
                rwkv_ln_b[i], rwkv_w_o[i])
        x = _moe_layer(x, ffn_norm[layer], router_group_w[layer], router_group_b[layer], router_expert_w[layer],
                       router_expert_b[layer], expert_w_gate[layer], expert_w_up[layer], expert_w_down[layer])
    return x
```

```python
import functools
import math

import jax
import jax.numpy as jnp
from jax import lax
from jax.experimental import pallas as pl
from jax.experimental.pallas import tpu as pltpu

F32 = jnp.float32
BF16 = jnp.bfloat16

D_MODEL = 1024
HEAD_DIM = 64
NORM_EPS = 1e-6
NEG_INF = -1e30
FORCE_SCORE = 1e4

DIFF_HEADS = 8
DIFF_QK_DIM = 32
NSA_HEADS = 8
NSA_KV_GROUPS = 2
NSA_HPG = 4
NSA_CMP_BLOCK = 32
NSA_CMP_STRIDE = 16
NSA_SEL_BLOCK = 64
NSA_TOP_N = 16
NSA_WINDOW = 512
IN_COLS = 2840
IN_COLS_PAD = 2944
GATE_COLS = 24

RWKV_HEAD = 64
RWKV_LNX_EPS = 64e-5
RWKV_CHUNK = 64

N_GROUPS = 4
EXPERTS_PER_GROUP = 8
N_EXPERTS = 32
D_EXPERT = 256
MOE_ROWS = 256

LANES = 128
VMEM_LIMIT = 56 * 1024 * 1024


def _bf(x):
    return x.astype(BF16)


def _dot(a, b):
    return jnp.dot(a, b, preferred_element_type=F32)


def _dot_nt(a, b):
    return lax.dot_general(a, b, (((1,), (1,)), ((), ())), preferred_element_type=F32)


def _dot_tn(a, b):
    return lax.dot_general(a, b, (((0,), (0,)), ((), ())), preferred_element_type=F32)


def _split2(x):
    hi = _bf(x)
    lo = _bf(x - hi.astype(F32))
    return hi, lo


def _params(sem):
    return pltpu.CompilerParams(dimension_semantics=sem, vmem_limit_bytes=VMEM_LIMIT)


def _block_diag(n, group):
    r = jnp.arange(n) // group
    return ((r[:, None] == r[None, :]).astype(F32) / group).astype(BF16)


def _rms(x, gain):
    return x * lax.rsqrt(jnp.mean(x * x, axis=-1, keepdims=True) + NORM_EPS) * gain


def _inproj_kernel(x_ref, g_ref, w_ref, b_ref, bd32_ref, bd64_ref, gq_ref, gk_ref, gnq_ref, gnk_ref,
                   dq_ref, dk_ref, dv_ref, nq_ref, nkv_ref, gt_ref):
    xn = _rms(x_ref[...], g_ref[...])
    h = _dot(_bf(xn), w_ref[...])

    def gnorm(seg, bd, gain):
        hi, lo = _split2(seg * seg)
        ms = _dot(hi, bd) + _dot(lo, bd)
        return seg * lax.rsqrt(ms + NORM_EPS) * gain

    dq_ref[...] = _bf(gnorm(h[:, 0:512], bd32_ref[...], gq_ref[...]))
    dk_ref[...] = _bf(gnorm(h[:, 512:1024], bd32_ref[...], gk_ref[...]))
    dv_ref[...] = _bf(h[:, 1024:1536])
    nq_ref[...] = _bf(gnorm(h[:, 1536:2048], bd64_ref[...], gnq_ref[...]))
    kv = h[:, 2048:2816]
    ksw = jnp.concatenate([kv[:, 256:384], kv[:, 512:640]], axis=1)
    kswn = gnorm(ksw, bd64_ref[0:256, 0:256], gnk_ref[...])
    nkv_ref[:, 0:256] = _bf(kv[:, 0:256])
    nkv_ref[:, 256:384] = _bf(kswn[:, 0:128])
    nkv_ref[:, 384:512] = _bf(kv[:, 384:512])
    nkv_ref[:, 512:640] = _bf(kswn[:, 128:256])
    nkv_ref[:, 640:768] = _bf(kv[:, 640:768])
    gt_ref[...] = jax.nn.sigmoid(h[:, 2816:2944] + b_ref[...])


def _attn_inproj(x2d, gain, w_in, gate_bias, dq_gain, dk_gain, nq_gain, nk_gain):
    T, D = x2d.shape
    tm = min(256, T)
    w = _bf(jnp.pad(w_in, ((0, 0), (0, IN_COLS_PAD - IN_COLS))))
    bias = jnp.pad(gate_bias, (0, LANES - GATE_COLS)).reshape(1, LANES)
    log2e = math.log2(math.e)
    gq = (jnp.tile(dq_gain, 16) * (DIFF_QK_DIM ** -0.5 * log2e)).reshape(1, 512)
    gk = jnp.tile(dk_gain, 16).reshape(1, 512)
    gnq = (jnp.tile(nq_gain, 8) * (HEAD_DIM ** -0.5 * log2e)).reshape(1, 512)
    gnk = jnp.concatenate([jnp.tile(nk_gain[1], 2), jnp.tile(nk_gain[2], 2)]).reshape(1, 256)
    row = lambda n: pl.BlockSpec((tm, n), lambda i: (i, 0))
    full = lambda a: pl.BlockSpec(a.shape, lambda i: (0,) * a.ndim)
    consts = (gain.reshape(1, D), w, bias, _block_diag(512, 32), _block_diag(512, 64), gq, gk, gnq, gnk)
    return pl.pallas_call(
        _inproj_kernel,
        grid=(T // tm,),
        in_specs=[row(D)] + [full(a) for a in consts],
        out_specs=[row(512), row(512), row(512), row(512), row(768), row(LANES)],
        out_shape=[jax.ShapeDtypeStruct((T, 512), BF16)] * 4
        + [jax.ShapeDtypeStruct((T, 768), BF16), jax.ShapeDtypeStruct((T, LANES), F32)],
        compiler_params=_params(("arbitrary",)),
        name="attn_inproj",
    )(x2d, *consts)


def _flash_tiles(lo, hi, scores, values_t, mask_body, mask_last, s_ref, p_ref):
    tk, cols = s_ref.shape
    dv = values_t(lo).shape[0]
    p_ref[...] = jnp.zeros_like(p_ref)
    s_ref[...] = scores(lo)

    def process(j, carry, mask, s_next):
        m, l, acc = carry
        s = s_ref[...]
        if mask is not None:
            s = mask(j, s)
        pv = _dot(values_t(jnp.maximum(j - 1, lo)), p_ref[...])
        m_new = jnp.maximum(m, jnp.max(s, axis=0, keepdims=True))
        alpha = jnp.exp2(m - m_new)
        p = jnp.exp2(s - m_new)
        l = alpha * l + jnp.sum(p, axis=0, keepdims=True)
        acc = (acc + pv) * alpha
        p_ref[...] = _bf(p)
        if s_next is not None:
            s_ref[...] = s_next
        return m_new, l, acc

    init = (jnp.full((1, cols), NEG_INF, F32), jnp.zeros((1, cols), F32), jnp.zeros((dv, cols), F32))
    carry = lax.fori_loop(lo, hi, lambda j, c: process(j, c, mask_body, scores(j + 1)), init)
    m, l, acc = process(hi, carry, mask_last, None)
    return (acc + _dot(values_t(hi), p_ref[...])) / l


def _diff_attn_kernel(qt_ref, k_ref, vt_ref, lam_ref, sg_ref, o_ref, s_ref, p_ref, *, tq, lambda_init):
    i = pl.program_id(1)
    zero = jnp.zeros((DIFF_QK_DIM, tq), BF16)
    qb = jnp.concatenate([jnp.concatenate([qt_ref[0, 0], zero], axis=1),
                          jnp.concatenate([zero, qt_ref[0, 1]], axis=1)], axis=0)

    def causal(j, s):
        kpos = lax.broadcasted_iota(jnp.int32, (tq, 2 * tq), 0)
        qpos = lax.broadcasted_iota(jnp.int32, (tq, 2 * tq), 1) & (tq - 1)
        return jnp.where(kpos <= qpos, s, NEG_INF)

    o = _flash_tiles(0, i, lambda j: _dot(k_ref[0, j], qb), lambda j: vt_ref[0, j], None, causal, s_ref, p_ref)
    lp = lam_ref[...]
    lam = (jnp.exp(jnp.sum(lp[0:1] * lp[1:2], axis=-1, keepdims=True))
           - jnp.exp(jnp.sum(lp[2:3] * lp[3:4], axis=-1, keepdims=True)) + lambda_init)
    o = o[:, 0:tq] - lam * o[:, tq:2 * tq]
    o = o * lax.rsqrt(jnp.mean(o * o, axis=0, keepdims=True) + NORM_EPS) * sg_ref[...]
    o_ref[0] = o * (1.0 - lambda_init)


def _diff_attention(dq, dk, dv, lam_p, subln, lambda_init, B, S):
    tq = min(256, S)
    n = S // tq
    H = DIFF_HEADS
    qt = dq.reshape(B, S, H, 2, DIFF_QK_DIM).transpose(0, 2, 3, 4, 1).reshape(B * H, 2, DIFF_QK_DIM, S)
    k = dk.reshape(B, n, tq, H, HEAD_DIM).transpose(0, 3, 1, 2, 4).reshape(B * H, n, tq, HEAD_DIM)
    vt = dv.reshape(B, n, tq, H, HEAD_DIM).transpose(0, 3, 1, 4, 2).reshape(B * H, n, HEAD_DIM, tq)
    out = pl.pallas_call(
        functools.partial(_diff_attn_kernel, tq=tq, lambda_init=lambda_init),
        grid=(B * H, n),
        in_specs=[
            pl.BlockSpec((1, 2, DIFF_QK_DIM, tq), lambda b, i: (b, 0, 0, i)),
            pl.BlockSpec((1, n, tq, HEAD_DIM), lambda b, i: (b, 0, 0, 0)),
            pl.BlockSpec((1, n, HEAD_DIM, tq), lambda b, i: (b, 0, 0, 0)),
            pl.BlockSpec((4, DIFF_QK_DIM), lambda b, i: (0, 0)),
            pl.BlockSpec((HEAD_DIM, 1), lambda b, i: (0, 0)),
        ],
        out_specs=pl.BlockSpec((1, HEAD_DIM, tq), lambda b, i: (b, 0, i)),
        out_shape=jax.ShapeDtypeStruct((B * H, HEAD_DIM, S), F32),
        scratch_shapes=[pltpu.VMEM((tq, 2 * tq), F32), pltpu.VMEM((tq, 2 * tq), BF16)],
        compiler_params=_params(("arbitrary", "arbitrary")),
        name="diff_attention",
    )(qt, k, vt, lam_p, subln.reshape(HEAD_DIM, 1))
    return out.reshape(B, H, HEAD_DIM, S).transpose(0, 3, 1, 2).reshape(B * S, H * HEAD_DIM)


def _nsa_compress_kernel(a_ref, w1a_ref, w1b_ref, pea_ref, peb_ref, w2_ref, kg_ref, o_ref, *, nch):
    kv = pl.program_id(1)
    a = a_ref[0, 0, 0]
    w1a = w1a_ref[0]
    w1b = w1b_ref[0]
    ya = _dot(a, w1a)
    yb = _dot(a, w1b)
    bias = _dot(pea_ref[0], w1a) + _dot(peb_ref[0], w1b)
    pre = ya + pltpu.roll(yb, nch - 1, 0) + bias
    out = _dot(_bf(jax.nn.gelu(pre)), w2_ref[0])
    o_ref[0, 0, 0] = jnp.where(kv == 0, _rms(out, kg_ref[...]), out)


def _nsa_compress(nkv, cmp_pe, cmp_w1, cmp_w2, k_gain0, B, S):
    nch = S // NSA_CMP_STRIDE
    G = NSA_KV_GROUPS
    half = NSA_CMP_STRIDE * HEAD_DIM
    a = nkv[:, 0:256].reshape(B, S, 2, G, HEAD_DIM).transpose(0, 2, 3, 1, 4).reshape(B, 2, G, nch, half)
    w1a = _bf(cmp_w1[:, :half])
    w1b = _bf(cmp_w1[:, half:])
    pea = _bf(cmp_pe[:, :NSA_CMP_STRIDE].reshape(2, 1, half))
    peb = _bf(cmp_pe[:, NSA_CMP_STRIDE:].reshape(2, 1, half))
    per_kv = lambda shp: pl.BlockSpec((1,) + shp, lambda b, kv, g: (kv, 0, 0))
    return pl.pallas_call(
        functools.partial(_nsa_compress_kernel, nch=nch),
        grid=(B, 2, G),
        in_specs=[
            pl.BlockSpec((1, 1, 1, nch, half), lambda b, kv, g: (b, kv, g, 0, 0)),
            per_kv((half, HEAD_DIM)), per_kv((half, HEAD_DIM)), per_kv((1, half)), per_kv((1, half)),
            per_kv((HEAD_DIM, HEAD_DIM)),
            pl.BlockSpec((1, HEAD_DIM), lambda b, kv, g: (0, 0)),
        ],
        out_specs=pl.BlockSpec((1, 1, 1, nch, HEAD_DIM), lambda b, kv, g: (b, kv, g, 0, 0)),
        out_shape=jax.ShapeDtypeStruct((B, 2, G, nch, HEAD_DIM), F32),
        compiler_params=_params(("arbitrary", "arbitrary", "arbitrary")),
        name="nsa_compress",
    )(a, w1a, w1b, pea, peb, _bf(cmp_w2), k_gain0.reshape(1, HEAD_DIM))


def _group_queries(q_ref):
    return jnp.concatenate([q_ref[0, h] for h in range(NSA_HPG)], axis=1)


def _store_heads(o_ref, o, tq):
    for h in range(NSA_HPG):
        o_ref[0, h] = o[:, h * tq:(h + 1) * tq]


def _nsa_cmp_attn_kernel(q_ref, k_ref, vt_ref, o_ref, sel_ref, *, tq, nch, n_sel, top_n):
    i = pl.program_id(1)
    cols = NSA_HPG * tq
    n_cmp = nch - 1
    s = _dot(k_ref[0], _group_queries(q_ref))
    pos = i * tq + (lax.broadcasted_iota(jnp.int32, (nch, cols), 1) & (tq - 1))
    c = lax.broadcasted_iota(jnp.int32, (nch, cols), 0)
    ok = jnp.where(c < n_cmp, c * NSA_CMP_STRIDE + (NSA_CMP_BLOCK - 1), 1 << 30) <= pos
    s = jnp.where(ok, s, NEG_INF)
    p = jnp.where(ok, jnp.exp2(s - jnp.max(s, axis=0, keepdims=True)), 0.0)
    l = jnp.sum(p, axis=0, keepdims=True)
    p = p / jnp.where(l > 0.0, l, 1.0)
    _store_heads(o_ref, _dot(vt_ref[0], _bf(p)), tq)

    pg = p[:, 0:tq]
    for h in range(1, NSA_HPG):
        pg = pg + p[:, h * tq:(h + 1) * tq]
    jj = lax.broadcasted_iota(jnp.int32, (n_sel, nch), 0) * NSA_SEL_BLOCK
    cc = lax.broadcasted_iota(jnp.int32, (n_sel, nch), 1) * NSA_CMP_STRIDE
    cover = jnp.where(cc < jj + NSA_SEL_BLOCK,
                      jnp.where(cc + NSA_CMP_BLOCK > jj, jnp.where(cc < n_cmp * NSA_CMP_STRIDE, 1.0, 0.0), 0.0), 0.0)
    cover = _bf(cover)
    hi, lo = _split2(pg)
    imp = _dot(cover, hi) + _dot(cover, lo)
    blk = lax.broadcasted_iota(jnp.int32, (n_sel, tq), 0)
    cur = (i * tq + lax.broadcasted_iota(jnp.int32, (n_sel, tq), 1)) >> 6
    imp = jnp.where(blk == cur, FORCE_SCORE, jnp.where(blk == 0, FORCE_SCORE, jnp.where(blk > cur, NEG_INF, imp)))
    blk_f = blk.astype(F32)
    sel = jnp.zeros((n_sel, tq), F32)
    for _ in range(top_n):
        mx = jnp.max(imp, axis=0, keepdims=True)
        first = jnp.min(jnp.where(imp == mx, blk_f, float(n_sel)), axis=0, keepdims=True)
        hit = blk_f == first
        sel = jnp.where(hit, 1.0, sel)
        imp = jnp.where(hit, -jnp.inf, imp)
    sel_ref[0] = _bf(sel)


def _nsa_cmp_attention(qt, kc, vc_t, B, S):
    BG = qt.shape[0]
    tq = min(256, S)
    nch = S // NSA_CMP_STRIDE
    n_sel = S // NSA_SEL_BLOCK
    top_n = min(NSA_TOP_N, n_sel)
    qspec = pl.BlockSpec((1, NSA_HPG, HEAD_DIM, tq), lambda b, i: (b, 0, 0, i))
    return pl.pallas_call(
        functools.partial(_nsa_cmp_attn_kernel, tq=tq, nch=nch, n_sel=n_sel, top_n=top_n),
        grid=(BG, S // tq),
        in_specs=[
            qspec,
            pl.BlockSpec((1, nch, HEAD_DIM), lambda b, i: (b, 0, 0)),
            pl.BlockSpec((1, HEAD_DIM, nch), lambda b, i: (b, 0, 0)),
        ],
        out_specs=[qspec, pl.BlockSpec((1, n_sel, tq), lambda b, i: (b, 0, i))],
        out_shape=[jax.ShapeDtypeStruct((BG, NSA_HPG, HEAD_DIM, S), F32),
                   jax.ShapeDtypeStruct((BG, n_sel, S), BF16)],
        compiler_params=_params(("arbitrary", "arbitrary")),
        name="nsa_cmp_attention",
    )(qt, kc, vc_t)


def _nsa_sel_win_kernel(q_ref, ks_ref, vst_ref, kw_ref, vwt_ref, sel_ref, os_ref, ow_ref, s_ref, p_ref, *, tq, n_sel):
    i = pl.program_id(1)
    qt = _group_queries(q_ref)
    selm = sel_ref[0]
    kloc = lax.broadcasted_iota(jnp.int32, (tq, tq), 0)
    qpos = i * tq + lax.broadcasted_iota(jnp.int32, (tq, tq), 1)
    blk_row = lax.broadcasted_iota(jnp.int32, (tq, n_sel), 0)
    blk_col = lax.broadcasted_iota(jnp.int32, (tq, n_sel), 1)

    def masked(s, keep):
        return jnp.where(jnp.concatenate([keep] * NSA_HPG, axis=1) > 0.5, s, NEG_INF)

    def sel_mask(j, s):
        expand = _bf(jnp.where(blk_col == ((j * tq + blk_row) >> 6), 1.0, 0.0))
        chosen = _dot(expand, selm)
        return masked(s, jnp.where(j * tq + kloc <= qpos, chosen, 0.0))

    o = _flash_tiles(0, i, lambda j: _dot(ks_ref[0, j], qt), lambda j: vst_ref[0, j], sel_mask, sel_mask,
                     s_ref, p_ref)
    _store_heads(os_ref, o, tq)

    def win_mask(j, s):
        dist = qpos - (j * tq + kloc)
        return masked(s, jnp.where(dist >= 0, jnp.where(dist < NSA_WINDOW, 1.0, 0.0), 0.0))

    first = jnp.maximum(i - (NSA_WINDOW - 1 + tq - 1) // tq, 0)
    o = _flash_tiles(first, i, lambda j: _dot(kw_ref[0, j], qt), lambda j: vwt_ref[0, j], win_mask, win_mask,
                     s_ref, p_ref)
    _store_heads(ow_ref, o, tq)


def _nsa_sel_win(qt, ks, vst, kw, vwt, sel, B, S):
    BG = qt.shape[0]
    tq = min(256, S)
    n = S // tq
    n_sel = S // NSA_SEL_BLOCK
    qspec = pl.BlockSpec((1, NSA_HPG, HEAD_DIM, tq), lambda b, i: (b, 0, 0, i))
    kspec = pl.BlockSpec((1, n, tq, HEAD_DIM), lambda b, i: (b, 0, 0, 0))
    vtspec = pl.BlockSpec((1, n, HEAD_DIM, tq), lambda b, i: (b, 0, 0, 0))
    return pl.pallas_call(
        functools.partial(_nsa_sel_win_kernel, tq=tq, n_sel=n_sel),
        grid=(BG, n),
        in_specs=[qspec, kspec, vtspec, kspec, vtspec, pl.BlockSpec((1, n_sel, tq), lambda b, i: (b, 0, i))],
        out_specs=[qspec, qspec],
        out_shape=[jax.ShapeDtypeStruct((BG, NSA_HPG, HEAD_DIM, S), F32)] * 2,
        scratch_shapes=[pltpu.VMEM((tq, NSA_HPG * tq), F32), pltpu.VMEM((tq, NSA_HPG * tq), BF16)],
        compiler_params=_params(("arbitrary", "arbitrary")),
        name="nsa_sel_win_attention",
    )(qt, ks, vst, kw, vwt, sel)


def _attn_outproj_kernel(x_ref, d_ref, oc_ref, os_ref, ow_ref, gt_ref, ge_ref, w_ref, o_ref):
    ghi, glo = _split2(gt_ref[...])

    def gate(r):
        return _dot(ghi, ge_ref[r]) + _dot(glo, ge_ref[r])

    nsa = gate(0) * oc_ref[...] + gate(1) * os_ref[...] + gate(2) * ow_ref[...]
    y = _dot(_bf(d_ref[...]), w_ref[0:512, :]) + _dot(_bf(nsa), w_ref[512:1024, :])
    o_ref[...] = x_ref[...] + y


def _attn_outproj(x2d, diff_o, o_cmp, o_sel, o_win, gates, w_out):
    T, D = x2d.shape
    tm = min(256, T)
    col = jnp.arange(512) // HEAD_DIM
    src = jnp.arange(LANES)
    ge = jnp.stack([_bf((src[:, None] == col[None, :] * 3 + r).astype(F32)) for r in range(3)])
    row = lambda n: pl.BlockSpec((tm, n), lambda i: (i, 0))
    return pl.pallas_call(
        _attn_outproj_kernel,
        grid=(T // tm,),
        in_specs=[row(D), row(512), row(512), row(512), row(512), row(LANES),
                  pl.BlockSpec((3, LANES, 512), lambda i: (0, 0, 0)),
                  pl.BlockSpec((D, D), lambda i: (0, 0))],
        out_specs=row(D),
        out_shape=jax.ShapeDtypeStruct((T, D), F32),
        compiler_params=_params(("arbitrary",)),
        name="attn_outproj",
    )(x2d, diff_o, o_cmp, o_sel, o_win, gates, ge, _bf(w_out))


def _hybrid_attention_layer(x, layer, norm_gain, w_in, gate_bias, dq_gain, dk_gain, lam_p, subln, nq_gain, nk_gain,
                            cmp_pe, cmp_w1, cmp_w2, w_out):
    B, S, D = x.shape
    T = B * S
    G = NSA_KV_GROUPS
    x2d = x.reshape(T, D)
    dq, dk, dv, nq, nkv, gates = _attn_inproj(x2d, norm_gain, w_in, gate_bias, dq_gain, dk_gain, nq_gain, nk_gain)
    lambda_init = 0.8 - 0.6 * math.exp(-0.3 * layer)
    diff_o = _diff_attention(dq, dk, dv, lam_p, subln, lambda_init, B, S)

    cmp = _nsa_compress(nkv, cmp_pe, cmp_w1, cmp_w2, nk_gain[0], B, S)
    nch = S // NSA_CMP_STRIDE
    kc = _bf(cmp[:, 0]).reshape(B * G, nch, HEAD_DIM)
    vc_t = _bf(cmp[:, 1]).transpose(0, 1, 3, 2).reshape(B * G, HEAD_DIM, nch)
    qt = nq.reshape(B, S, G, NSA_HPG, HEAD_DIM).transpose(0, 2, 3, 4, 1).reshape(B * G, NSA_HPG, HEAD_DIM, S)
    o_cmp, sel = _nsa_cmp_attention(qt, kc, vc_t, B, S)

    tq = min(256, S)
    n = S // tq

    def k_tiles(cols):
        return cols.reshape(B, n, tq, G, HEAD_DIM).transpose(0, 3, 1, 2, 4).reshape(B * G, n, tq, HEAD_DIM)

    def vt_tiles(cols):
        return cols.reshape(B, n, tq, G, HEAD_DIM).transpose(0, 3, 1, 4, 2).reshape(B * G, n, HEAD_DIM, tq)

    o_sel, o_win = _nsa_sel_win(qt, k_tiles(nkv[:, 256:384]), vt_tiles(nkv[:, 384:512]),
                                k_tiles(nkv[:, 512:640]), vt_tiles(nkv[:, 640:768]), sel, B, S)

    def tokens(o):
        return o.reshape(B, G, NSA_HPG, HEAD_DIM, S).transpose(0, 4, 1, 2, 3).reshape(T, NSA_HEADS * HEAD_DIM)

    out = _attn_outproj(x2d, diff_o, tokens(o_cmp), tokens(o_sel), tokens(o_win), gates, w_out)
    return out.reshape(B, S, D)


def _softplus(z):
    return jnp.maximum(z, 0.0) + jnp.log(1.0 + jnp.exp(-jnp.abs(z)))


def _rwkv_proj_kernel(x_ref, xp_ref, g_ref, mix_ref, wr_ref, wk_ref, wv_ref, w1_ref, w2_ref, a1_ref, a2_ref,
                      g1_ref, g2_ref, vec_ref, bd_ref,
                      r_ref, wl_ref, k_ref, v_ref, kk_ref, b_ref, bonus_ref, gate_ref):
    i = pl.program_id(1)
    gain = g_ref[...]
    xn = _rms(x_ref[0], gain)
    prev = _rms(xp_ref[0], gain)[7:8] * jnp.where(i > 0, 1.0, 0.0)
    shifted = pltpu.roll(xn, 1, 0)
    first_row = lax.broadcasted_iota(jnp.int32, xn.shape, 0) == 0
    dx = jnp.where(first_row, prev, shifted) - xn
    mix = mix_ref[...]
    xr, xw, xk, xv, xa, xg = (_bf(xn + dx * mix[j:j + 1]) for j in range(6))
    vec = vec_ref[...]
    w0, a0, k_k, k_a, r_k = (vec[j:j + 1] for j in range(5))
    r = _dot(xr, wr_ref[...])
    k = _dot(xk, wk_ref[...])
    v = _dot(xv, wv_ref[...])
    w = -_softplus(-(w0 + _dot(_bf(jnp.tanh(_dot(xw, w1_ref[...]))), w2_ref[...]))) - 0.5
    a = jax.nn.sigmoid(a0 + _dot(_bf(_dot(xa, a1_ref[...])), a2_ref[...]))
    gate_ref[0] = _dot(_bf(jax.nn.sigmoid(_dot(xg, g1_ref[...]))), g2_ref[...])
    bd = bd_ref[...]
    kk = k * k_k
    kk = kk / jnp.maximum(jnp.sqrt(_dot(_bf(kk * kk), bd)), 1e-12)
    k = k * (1.0 + (a - 1.0) * k_a)
    r_ref[0] = r
    wl_ref[0] = -jnp.exp(w)
    k_ref[0] = k
    v_ref[0] = v
    kk_ref[0] = kk
    b_ref[0] = kk * a
    bonus_ref[0] = _dot(_bf(r * k * r_k), bd) * v


def _rwkv_proj(x, gain, mix, w_r, w_k, w_v, w0, w1, w2, a0, a1, a2, g1, g2, k_k, k_a, r_k):
    B, S, D = x.shape
    tm = min(256, S)
    pad_c = lambda m, n: _bf(jnp.pad(m, ((0, 0), (0, n - m.shape[1]))))
    pad_r = lambda m, n: _bf(jnp.pad(m, ((0, n - m.shape[0]), (0, 0))))
    lw = LANES
    lg = 2 * LANES
    consts = (gain.reshape(1, D), jnp.pad(mix, ((0, 2), (0, 0))), _bf(w_r), _bf(w_k), _bf(w_v),
              pad_c(w1, lw), pad_r(w2, lw), pad_c(a1, lw), pad_r(a2, lw), pad_c(g1, lg), pad_r(g2, lg),
              jnp.pad(jnp.stack([w0, a0, k_k, k_a, r_k]), ((0, 3), (0, 0))),
              (_block_diag(D, RWKV_HEAD).astype(F32) * RWKV_HEAD).astype(BF16))
    full = lambda a: pl.BlockSpec(a.shape, lambda b, i: (0,) * a.ndim)
    tile = pl.BlockSpec((1, tm, D), lambda b, i: (b, i, 0))
    return pl.pallas_call(
        _rwkv_proj_kernel,
        grid=(B, S // tm),
        in_specs=[tile, pl.BlockSpec((1, 8, D), lambda b, i: (b, jnp.maximum(i * (tm // 8) - 1, 0), 0))]
        + [full(a) for a in consts],
        out_specs=[tile] * 8,
        out_shape=[jax.ShapeDtypeStruct((B, S, D), F32)] * 8,
        compiler_params=_params(("arbitrary", "arbitrary")),
        name="rwkv_proj",
    )(x, x, *consts)


def _rwkv_chunk_kernel(r_ref, wl_ref, k_ref, v_ref, kk_ref, b_ref, m_ref, g0_ref, rq_ref, y0_ref, *, cpb):
    C = RWKV_CHUNK
    lane = lax.broadcasted_iota(jnp.int32, (C, LANES), 1)
    head0 = lane < RWKV_HEAD
    ti = lax.broadcasted_iota(jnp.int32, (2 * C, 2 * C), 0) & (C - 1)
    tj = lax.broadcasted_iota(jnp.int32, (2 * C, 2 * C), 1) & (C - 1)
    strict = ti > tj
    incl = ti >= tj
    eye = lax.broadcasted_iota(jnp.int32, (LANES, LANES), 0) == lax.broadcasted_iota(jnp.int32, (LANES, LANES), 1)
    ltri = _bf(jnp.where(lax.broadcasted_iota(jnp.int32, (C, C), 0) >= lax.broadcasted_iota(jnp.int32, (C, C), 1),
                         1.0, 0.0))

    def stack(x):
        return jnp.concatenate([jnp.where(head0, x, 0.0), jnp.where(head0, 0.0, x)], axis=0)

    chunks = range(cpb)
    rows = [slice(c * C, (c + 1) * C) for c in chunks]
    wl = [wl_ref[0, rows[c], :] for c in chunks]
    cum = []
    for c in chunks:
        h1 = _bf(wl[c])
        r1 = wl[c] - h1.astype(F32)
        h2 = _bf(r1)
        h3 = _bf(r1 - h2.astype(F32))
        cum.append(_dot(ltri, h1) + _dot(ltri, h2) + _dot(ltri, h3))
    total = [cum[c][C - 1:C, :] for c in chunks]
    lhs_kk, lhs_r, vs, a = [], [], [], []
    for c in chunks:
        p_inv = jnp.exp(-cum[c])
        lhs_kk.append(stack(kk_ref[0, rows[c], :] * jnp.exp(cum[c] - wl[c])))
        lhs_r.append(stack(r_ref[0, rows[c], :] * jnp.exp(cum[c])))
        vs.append(_bf(stack(v_ref[0, rows[c], :])))
        a.append(_dot_nt(_bf(jnp.concatenate([lhs_kk[c], lhs_r[c]], axis=0)),
                         _bf(jnp.concatenate([stack(b_ref[0, rows[c], :] * p_inv),
                                              stack(k_ref[0, rows[c], :] * p_inv)], axis=0))))
    npow = [_bf(jnp.where(strict, a[c][0:2 * C, 0:2 * C], 0.0)) for c in chunks]
    a_k = [_bf(jnp.where(strict, a[c][0:2 * C, 2 * C:4 * C], 0.0)) for c in chunks]
    a_rb = [_bf(jnp.where(incl, a[c][2 * C:4 * C, 0:2 * C], 0.0)) for c in chunks]
    a_rk = [_bf(jnp.where(incl, a[c][2 * C:4 * C, 2 * C:4 * C], 0.0)) for c in chunks]
    x = [jnp.concatenate([_dot(a_k[c], vs[c]), lhs_kk[c]], axis=1) for c in chunks]
    x = [x[c] - _dot(npow[c], _bf(x[c])) for c in chunks]
    for _ in range(5):
        npow = [_bf(_dot(npow[c], npow[c])) for c in chunks]
        x = [x[c] + _dot(npow[c], _bf(x[c])) for c in chunks]
    u0 = [_bf(-x[c][:, 0:LANES]) for c in chunks]
    wmat = [_bf(-x[c][:, LANES:2 * LANES]) for c in chunks]
    for c in chunks:
        y0_ref[0, 0, c] = _dot(a_rk[c], vs[c]) + _dot(a_rb[c], u0[c])
        rq_ref[0, 0, c] = lhs_r[c] + _dot(a_rb[c], wmat[c])
    for c in chunks:
        p_end = jnp.exp(total[c] - cum[c])
        bc = _bf(stack(b_ref[0, rows[c], :] * p_end))
        kc = _bf(stack(k_ref[0, rows[c], :] * p_end))
        g0_ref[0, 0, c] = _dot_tn(bc, u0[c]) + _dot_tn(kc, vs[c])
        m_ref[0, 0, c] = jnp.where(eye, jnp.exp(total[c]), 0.0) + _dot_tn(bc, wmat[c])


def _rwkv_chunks(r, wl, k, v, kk, b):
    B, S, D = r.shape
    C = RWKV_CHUNK
    nc = S // C
    cpb = min(8, nc)
    hp = D // LANES
    tile = pl.BlockSpec((1, cpb * C, LANES), lambda bi, h, c: (bi, c, h))
    out = pl.BlockSpec((1, 1, cpb, LANES, LANES), lambda bi, h, c: (bi, h, c, 0, 0))
    return pl.pallas_call(
        functools.partial(_rwkv_chunk_kernel, cpb=cpb),
        grid=(B, hp, nc // cpb),
        in_specs=[tile] * 6,
        out_specs=[out] * 4,
        out_shape=[jax.ShapeDtypeStruct((B, hp, nc, LANES, LANES), F32)] * 4,
        compiler_params=_params(("arbitrary", "arbitrary", "arbitrary")),
        name="rwkv_chunk_summaries",
    )(r, wl, k, v, kk, b)


def _rwkv_scan_kernel(m_ref, g0_ref, rq_ref, y0_ref, y_ref, st_ref, *, cpb, hp):
    C = RWKV_CHUNK

    @pl.when(pl.program_id(1) == 0)
    def _():
        st_ref[...] = jnp.zeros_like(st_ref)

    for c in range(cpb):
        for h in range(hp):
            st = st_ref[h]
            s_hi, s_lo = _split2(st)
            y = y0_ref[0, h, c] + _dot(_bf(rq_ref[0, h, c]), s_hi)
            y_ref[0, c * C:(c + 1) * C, h * LANES:(h + 1) * LANES] = y[0:C] + y[C:2 * C]
            m_hi, m_lo = _split2(m_ref[0, h, c])
            st_ref[h] = g0_ref[0, h, c] + _dot(m_hi, s_hi) + _dot(m_hi, s_lo) + _dot(m_lo, s_hi)


def _rwkv_scan(m, g0, rq, y0, S):
    B, hp, nc = m.shape[:3]
    C = RWKV_CHUNK
    cpb = min(4, nc)
    blk = pl.BlockSpec((1, hp, cpb, LANES, LANES), lambda bi, c: (bi, 0, c, 0, 0))
    return pl.pallas_call(
        functools.partial(_rwkv_scan_kernel, cpb=cpb, hp=hp),
        grid=(B, nc // cpb),
        in_specs=[blk] * 4,
        out_specs=pl.BlockSpec((1, cpb * C, hp * LANES), lambda bi, c: (bi, c, 0)),
        out_shape=jax.ShapeDtypeStruct((B, S, hp * LANES), F32),
        scratch_shapes=[pltpu.VMEM((hp, LANES, LANES), F32)],
        compiler_params=_params(("arbitrary", "arbitrary")),
        name="rwkv_state_scan",
    )(m, g0, rq, y0)


def _rwkv_out_kernel(x_ref, y_ref, bonus_ref, gate_ref, lnw_ref, lnb_ref, bd_ref, wo_ref, o_ref):
    y = y_ref[...]
    bd = bd_ref[...]
    hi, lo = _split2(y)
    mu = _dot(hi, bd) + _dot(lo, bd)
    dev = y - mu
    hi, lo = _split2(dev * dev)
    var = _dot(hi, bd) + _dot(lo, bd)
    yn = dev * lax.rsqrt(var + RWKV_LNX_EPS) * lnw_ref[...] + lnb_ref[...]
    z = (yn + bonus_ref[...]) * gate_ref[...]
    o_ref[...] = x_ref[...] + _dot(_bf(z), wo_ref[...])


def _rwkv_out(x2d, y, bonus, gate, ln_w, ln_b, w_o):
    T, D = x2d.shape
    tm = min(256, T)
    row = pl.BlockSpec((tm, D), lambda i: (i, 0))
    vecs = pl.BlockSpec((1, D), lambda i: (0, 0))
    mat = pl.BlockSpec((D, D), lambda i: (0, 0))
    return pl.pallas_call(
        _rwkv_out_kernel,
        grid=(T // tm,),
        in_specs=[row, row, row, row, vecs, vecs, mat, mat],
        out_specs=row,
        out_shape=jax.ShapeDtypeStruct((T, D), F32),
        compiler_params=_params(("arbitrary",)),
        name="rwkv_out",
    )(x2d, y, bonus, gate, ln_w.reshape(1, D), ln_b.reshape(1, D), _block_diag(D, RWKV_HEAD), _bf(w_o))


def _rwkv_layer(x, norm_gain, mix, w_r, w_k, w_v, w0, w1, w2, a0, a1, a2, g1, g2, k_k, k_a, r_k, ln_w, ln_b, w_o):
    B, S, D = x.shape
    r, wl, k, v, kk, b, bonus, gate = _rwkv_proj(x, norm_gain, mix, w_r, w_k, w_v, w0, w1, w2, a0, a1, a2,
                                                 g1, g2, k_k, k_a, r_k)
    m, g0, rq, y0 = _rwkv_chunks(r, wl, k, v, kk, b)
    y = _rwkv_scan(m, g0, rq, y0, S)
    T = B * S
    out = _rwkv_out(x.reshape(T, D), y.reshape(T, D), bonus.reshape(T, D), gate.reshape(T, D), ln_w, ln_b, w_o)
    return out.reshape(B, S, D)


def _router_kernel(x_ref, g_ref, whi_ref, wlo_ref, b_ref, xn_ref, route_ref):
    xn = _rms(x_ref[...], g_ref[...])
    xn_ref[...] = _bf(xn)
    hi, lo = _split2(xn)
    logits = _dot(hi, whi_ref[...]) + _dot(hi, wlo_ref[...]) + _dot(lo, whi_ref[...]) + b_ref[...]
    lane = lax.broadcasted_iota(jnp.int32, logits.shape, 1)
    lane_f = lane.astype(F32)

    def top(vals):
        mx = jnp.max(vals, axis=-1, keepdims=True)
        return mx, jnp.min(jnp.where(vals == mx, lane_f, float(LANES)), axis=-1, keepdims=True)

    glog = jnp.where(lane < N_GROUPS, logits, NEG_INF)
    gmax, gidx = top(glog)
    gsum = jnp.sum(jnp.where(lane < N_GROUPS, jnp.exp(glog - gmax), 0.0), axis=-1, keepdims=True)
    grp_p = 1.0 / gsum
    first = float(N_GROUPS) + EXPERTS_PER_GROUP * gidx
    elog = jnp.where(lane_f >= first, jnp.where(lane_f < first + EXPERTS_PER_GROUP, logits, NEG_INF), NEG_INF)
    v1, i1 = top(elog)
    v2, i2 = top(jnp.where(lane_f == i1, NEG_INF, elog))
    e2 = jnp.exp(v2 - v1)
    gate1 = grp_p / (1.0 + e2)
    gate2 = grp_p * e2 / (1.0 + e2)
    route_ref[...] = jnp.where(lane == 0, i1 - N_GROUPS, jnp.where(lane == 1, i2 - N_GROUPS,
                               jnp.where(lane == 2, gate1, jnp.where(lane == 3, gate2, 0.0))))


def _router(x2d, gain, wg, bg, we, be):
    T, D = x2d.shape
    tm = min(256, T)
    w = jnp.pad(jnp.concatenate([wg, we], axis=1), ((0, 0), (0, LANES - N_GROUPS - N_EXPERTS)))
    w_hi = _bf(w)
    w_lo = _bf(w - w_hi.astype(F32))
    bias = jnp.pad(jnp.concatenate([bg, be]), (0, LANES - N_GROUPS - N_EXPERTS)).reshape(1, LANES)
    row = lambda n: pl.BlockSpec((tm, n), lambda i: (i, 0))
    full = lambda a: pl.BlockSpec(a.shape, lambda i: (0, 0))
    consts = (gain.reshape(1, D), w_hi, w_lo, bias)
    return pl.pallas_call(
        _router_kernel,
        grid=(T // tm,),
        in_specs=[row(D)] + [full(a) for a in consts],
        out_specs=[row(D), row(LANES)],
        out_shape=[jax.ShapeDtypeStruct((T, D), BF16), jax.ShapeDtypeStruct((T, LANES), F32)],
        compiler_params=_params(("arbitrary",)),
        name="moe_router",
    )(x2d, *consts)


def _expert_kernel(be_ref, nu_ref, x_ref, wgu_ref, wd_ref, o_ref):
    i = pl.program_id(0)

    @pl.when(i < nu_ref[0])
    def _():
        gu = _dot(x_ref[...], wgu_ref[0])
        hid = jax.nn.silu(gu[:, 0:D_EXPERT]) * gu[:, D_EXPERT:2 * D_EXPERT]
        o_ref[...] = _dot(_bf(hid), wd_ref[0])

    @pl.when(i >= nu_ref[0])
    def _():
        o_ref[...] = jnp.zeros_like(o_ref)


def _experts(xb, block_e, n_used, w_gu, w_d):
    rows, D = xb.shape
    n_blocks = rows // MOE_ROWS
    return pl.pallas_call(
        _expert_kernel,
        grid_spec=pltpu.PrefetchScalarGridSpec(
            num_scalar_prefetch=2,
            grid=(n_blocks,),
            in_specs=[
                pl.BlockSpec((MOE_ROWS, D), lambda i, be, nu: (i, 0)),
                pl.BlockSpec((1, D, 2 * D_EXPERT), lambda i, be, nu: (be[i], 0, 0)),
                pl.BlockSpec((1, D_EXPERT, D), lambda i, be, nu: (be[i], 0, 0)),
            ],
            out_specs=pl.BlockSpec((MOE_ROWS, D), lambda i, be, nu: (i, 0)),
        ),
        out_shape=jax.ShapeDtypeStruct((rows, D), F32),
        compiler_params=_params(("arbitrary",)),
        name="moe_experts",
    )(block_e, n_used, xb, w_gu, w_d)


def _combine_kernel(x_ref, y0_ref, y1_ref, route_ref, o_ref):
    route = route_ref[...]
    o_ref[...] = x_ref[...] + route[:, 2:3] * y0_ref[...] + route[:, 3:4] * y1_ref[...]


def _combine(x2d, y0, y1, route):
    T, D = x2d.shape
    tm = min(512, T)
    row = lambda n: pl.BlockSpec((tm, n), lambda i: (i, 0))
    return pl.pallas_call(
        _combine_kernel,
        grid=(T // tm,),
        in_specs=[row(D), row(D), row(D), row(LANES)],
        out_specs=row(D),
        out_shape=jax.ShapeDtypeStruct((T, D), F32),
        compiler_params=_params(("arbitrary",)),
        name="moe_combine",
    )(x2d, y0, y1, route)


def _moe_layer(x, gain, wg, bg, we, be, e_gate, e_up, e_down):
    B, S, D = x.shape
    T = B * S
    x2d = x.reshape(T, D)
    xn, route = _router(x2d, gain, wg, bg, we, be)
    flat_e = route[:, 0:2].astype(jnp.int32).reshape(-1)
    n_pairs = 2 * T
    onehot = (flat_e[:, None] == jnp.arange(N_EXPERTS, dtype=jnp.int32)[None, :]).astype(jnp.int32)
    csum = jnp.cumsum(onehot, axis=0)
    rank = jnp.take_along_axis(csum, flat_e[:, None], axis=1)[:, 0] - 1
    counts = csum[-1]
    padded = (counts + MOE_ROWS - 1) // MOE_ROWS * MOE_ROWS
    pad_end = jnp.cumsum(padded)
    dest = (pad_end - padded)[flat_e] + rank
    n_blocks = -(-n_pairs // MOE_ROWS) + N_EXPERTS
    slot_tok = jnp.zeros((n_blocks * MOE_ROWS,), jnp.int32).at[dest].set(jnp.arange(n_pairs, dtype=jnp.int32) // 2)
    block_start = jnp.arange(n_blocks, dtype=jnp.int32) * MOE_ROWS
    block_e = jnp.minimum(jnp.sum((pad_end[None, :] <= block_start[:, None]).astype(jnp.int32), axis=1),
                          N_EXPERTS - 1)
    n_used = (pad_end[-1:] // MOE_ROWS).astype(jnp.int32)
    xb = jnp.take(xn, slot_tok, axis=0)
    y_slots = _experts(xb, block_e, n_used, _bf(jnp.concatenate([e_gate, e_up], axis=-1)), _bf(e_down))
    ys = jnp.take(y_slots, dest, axis=0).reshape(T, 2, D)
    return _combine(x2d, ys[:, 0], ys[:, 1], route).reshape(B, S, D)


def kernel(x, mix_norm, attn_w_in, attn_gate_bias, diff_q_norm, diff_k_norm, diff_lambda, diff_subln, nsa_q_norm, nsa_k_norm, nsa_cmp_pe, nsa_cmp_w1, nsa_cmp_w2, attn_w_out, rwkv_mix, rwkv_w_r, rwkv_w_k, rwkv_w_v, rwkv_decay_w0, rwkv_decay_w1, rwkv_decay_w2, rwkv_iclr_a0, rwkv_iclr_a1, rwkv_iclr_a2, rwkv_gate_g1, rwkv_gate_g2, rwkv_k_k, rwkv_k_a, rwkv_r_k, rwkv_ln_w, rwkv_ln_b, rwkv_w_o, ffn_norm, router_group_w, router_group_b, router_expert_w, router_expert_b, expert_w_gate, expert_w_up, expert_w_down):
    depth = mix_norm.shape[0]
    for layer in range(depth):
        i = layer // 2
        if layer % 2 == 0:
            x = _hybrid_attention_layer(
                x, layer, mix_norm[layer], attn_w_in[i], attn_gate_bias[i], diff_q_norm[i], diff_k_norm[i],
                diff_lambda[i], diff_subln[i], nsa_q_norm[i], nsa_k_norm[i], nsa_cmp_pe[i], nsa_cmp_w1[i],
                nsa_cmp_w2[i], attn_w_out[i])
        else:
            x = _rwkv_layer(
                x, mix_norm[layer], rwkv_mix[i], rwkv_w_r[i], rwkv_w_k[i], rwkv_w_v[i], rwkv_decay_w0[i],
                rwkv_decay_w1[i], rwkv_decay_w2[i], rwkv_iclr_a0[i], rwkv_iclr_a1[i], rwkv_iclr_a2[i],
                rwkv_gate_g1[i], rwkv_gate_g2[i], rwkv_k_k[i], rwkv_k_a[i], rwkv_r_k[i], rwkv_ln_w[i],
                rwkv_ln_b[i], rwkv_w_o[i])
        x = _moe_layer(x, ffn_norm[layer], router_group_w[layer], router_group_b[layer], router_expert_w[layer],
                       router_expert_b[layer], expert_w_gate[layer], expert_w_up[layer], expert_w_down[layer])
    return x
```

```python
import functools
import math

import jax
import jax.numpy as jnp
from jax import lax
from jax.experimental import pallas as pl
from jax.experimental.pallas import tpu as pltpu

F32 = jnp.float32
BF16 = jnp.bfloat16

D_MODEL = 1024
HEAD_DIM = 64
NORM_EPS = 1e-6
NEG_INF = -1e30
FORCE_SCORE = 1e4

DIFF_HEADS = 8
DIFF_QK_DIM = 32
NSA_HEADS = 8
NSA_KV_GROUPS = 2
NSA_HPG = 4
NSA_CMP_BLOCK = 32
NSA_CMP_STRIDE = 16
NSA_SEL_BLOCK = 64
NSA_TOP_N = 16
NSA_WINDOW = 512
IN_COLS = 2840
IN_COLS_PAD = 2944
GATE_COLS = 24

RWKV_HEAD = 64
RWKV_LNX_EPS = 64e-5
RWKV_CHUNK = 64

N_GROUPS = 4
EXPERTS_PER_GROUP = 8
N_EXPERTS = 32
D_EXPERT = 256
MOE_ROWS = 256

LANES = 128
VMEM_LIMIT = 56 * 1024 * 1024


def _bf(x):
    return x.astype(BF16)


def _dot(a, b):
    return jnp.dot(a, b, preferred_element_type=F32)


def _dot_nt(a, b):
    return lax.dot_general(a, b, (((1,), (1,)), ((), ())), preferred_element_type=F32)


def _dot_tn(a, b):
    return lax.dot_general(a, b, (((0,), (0,)), ((), ())), preferred_element_type=F32)


def _split2(x):
    hi = _bf(x)
    lo = _bf(x - hi.astype(F32))
    return hi, lo


def _params(sem):
    return pltpu.CompilerParams(dimension_semantics=sem, vmem_limit_bytes=VMEM_LIMIT)


def _block_diag(n, group):
    r = jnp.arange(n) // group
    return ((r[:, None] == r[None, :]).astype(F32) / group).astype(BF16)


def _rms(x, gain):
    return x * lax.rsqrt(jnp.mean(x * x, axis=-1, keepdims=True) + NORM_EPS) * gain


def _inproj_kernel(x_ref, g_ref, wa_ref, wbt_ref, b_ref, bd32_ref, bd64_ref, gk_ref, gnk_ref, gq_ref, gnq_ref,
                   qd_ref, kd_ref, vd_ref, qn_ref, cmp_ref, ks_ref, vs_ref, kw_ref, vw_ref, gt_ref):
    xn = _bf(_rms(x_ref[0], g_ref[...]))
    ha = _dot(xn, wa_ref[...])
    hb = _dot_nt(wbt_ref[...], xn)
    bd32 = bd32_ref[...]
    bd64 = bd64_ref[...]

    def gnorm(seg, bd, gain):
        hi, lo = _split2(seg * seg)
        return seg * lax.rsqrt(_dot(hi, bd) + _dot(lo, bd) + NORM_EPS) * gain

    def gnorm_t(seg, bd, gain):
        hi, lo = _split2(seg * seg)
        return seg * lax.rsqrt(_dot(bd, hi) + _dot(bd, lo) + NORM_EPS) * gain

    kd_ref[0] = _bf(gnorm(ha[:, 0:512], bd32, gk_ref[...]))
    cmp_ref[0, 0] = ha[:, 512:640]
    cmp_ref[0, 1] = ha[:, 640:768]
    ksw = gnorm(ha[:, 768:1024], bd64[0:256, 0:256], gnk_ref[...])
    ks_ref[0] = _bf(ksw[:, 0:128])
    kw_ref[0] = _bf(ksw[:, 128:256])
    gt_ref[0] = jax.nn.sigmoid(ha[:, 1024:1152] + b_ref[...])
    qd_ref[0, 0] = _bf(gnorm_t(hb[0:512], bd32, gq_ref[...]))
    vd_ref[0, 0] = _bf(hb[512:1024])
    qn_ref[0, 0] = _bf(gnorm_t(hb[1024:1536], bd64, gnq_ref[...]))
    vs_ref[0, 0] = _bf(hb[1536:1664])
    vw_ref[0, 0] = _bf(hb[1664:1792])


def _attn_inproj(x, gain, w_in, gate_bias, dq_gain, dk_gain, nq_gain, nk_gain, tm):
    B, S, D = x.shape
    n = S // tm
    c = lambda lo, hi: w_in[:, lo:hi]
    gate_w = jnp.pad(c(2816, IN_COLS), ((0, 0), (0, LANES - GATE_COLS)))
    wa = _bf(jnp.concatenate([c(512, 1024), c(2048, 2304), c(2304, 2432), c(2560, 2688), gate_w], axis=1))
    wbt = _bf(jnp.concatenate([c(0, 512), c(1024, 1536), c(1536, 2048), c(2432, 2560), c(2688, 2816)], axis=1).T)
    bias = jnp.pad(gate_bias, (0, LANES - GATE_COLS)).reshape(1, LANES)
    log2e = math.log2(math.e)
    gk = jnp.tile(dk_gain, 16).reshape(1, 512)
    gnk = jnp.concatenate([jnp.tile(nk_gain[1], 2), jnp.tile(nk_gain[2], 2)]).reshape(1, 256)
    gq = (jnp.tile(dq_gain, 16) * (DIFF_QK_DIM ** -0.5 * log2e)).reshape(512, 1)
    gnq = (jnp.tile(nq_gain, 8) * (HEAD_DIM ** -0.5 * log2e)).reshape(512, 1)
    consts = (gain.reshape(1, D), wa, wbt, bias, _block_diag(512, 32), _block_diag(512, 64), gk, gnk, gq, gnq)
    full = lambda a: pl.BlockSpec(a.shape, lambda b, i: (0,) * a.ndim)
    tok = lambda w: pl.BlockSpec((1, tm, w), lambda b, i: (b, i, 0))
    tr = lambda r: pl.BlockSpec((1, 1, r, tm), lambda b, i: (b, i, 0, 0))
    tok_shape = lambda w, dt: jax.ShapeDtypeStruct((B, S, w), dt)
    tr_shape = lambda r: jax.ShapeDtypeStruct((B, n, r, tm), BF16)
    return pl.pallas_call(
        _inproj_kernel,
        grid=(B, n),
        in_specs=[tok(D)] + [full(a) for a in consts],
        out_specs=[tr(512), tok(512), tr(512), tr(512), pl.BlockSpec((1, 2, tm, LANES), lambda b, i: (b, 0, i, 0)),
                   tok(LANES), tr(LANES), tok(LANES), tr(LANES), tok(LANES)],
        out_shape=[tr_shape(512), tok_shape(512, BF16), tr_shape(512), tr_shape(512),
                   jax.ShapeDtypeStruct((B, 2, S, LANES), F32),
                   tok_shape(LANES, BF16), tr_shape(LANES), tok_shape(LANES, BF16), tr_shape(LANES),
                   tok_shape(LANES, F32)],
        compiler_params=_params(("arbitrary", "arbitrary")),
        name="attn_inproj",
    )(x, *consts)


def _flash_tiles(lo, hi, scores, values, mask_body, mask_last, s_ref, p_ref):
    tk, cols = s_ref.shape
    dv = HEAD_DIM
    p_ref[...] = jnp.zeros_like(p_ref)
    s_ref[...] = scores(lo)

    def process(j, carry, mask, s_next):
        m, l, acc = carry
        s = s_ref[...]
        if mask is not None:
            s = mask(j, s)
        pv = values(jnp.maximum(j - 1, lo), p_ref[...])
        m_new = jnp.maximum(m, jnp.max(s, axis=0, keepdims=True))
        alpha = jnp.exp2(m - m_new)
        p = jnp.exp2(s - m_new)
        l = alpha * l + jnp.sum(p, axis=0, keepdims=True)
        acc = (acc + pv) * alpha
        p_ref[...] = _bf(p)
        if s_next is not None:
            s_ref[...] = s_next
        return m_new, l, acc

    init = (jnp.full((1, cols), NEG_INF, F32), jnp.zeros((1, cols), F32), jnp.zeros((dv, cols), F32))
    carry = lax.fori_loop(lo, hi, lambda j, c: process(j, c, mask_body, scores(j + 1)), init)
    m, l, acc = process(hi, carry, mask_last, None)
    return (acc + values(hi, p_ref[...])) / l


def _diff_attn_kernel(qt_ref, k_ref, vt_ref, lam_ref, sg_ref, o_ref, s_ref, p_ref, *, tq, lambda_init):
    i = pl.program_id(2)
    q = qt_ref[0, 0]
    zero = jnp.zeros((DIFF_QK_DIM, tq), BF16)
    qb = jnp.concatenate(
        [jnp.concatenate([q[r * DIFF_QK_DIM:(r + 1) * DIFF_QK_DIM] if c == r else zero for c in range(4)], axis=1)
         for r in range(4)], axis=0)

    def causal(j, s):
        kpos = lax.broadcasted_iota(jnp.int32, (tq, 4 * tq), 0)
        qpos = lax.broadcasted_iota(jnp.int32, (tq, 4 * tq), 1) & (tq - 1)
        return jnp.where(kpos <= qpos, s, NEG_INF)

    def values(j, p):
        vt = vt_ref[0, j]
        return jnp.concatenate([_dot(vt[0:HEAD_DIM], p[:, 0:2 * tq]),
                                _dot(vt[HEAD_DIM:2 * HEAD_DIM], p[:, 2 * tq:4 * tq])], axis=1)

    o = _flash_tiles(0, i, lambda j: _dot(k_ref[0, j], qb), values, None, causal, s_ref, p_ref)
    lp = lam_ref[...]
    lam = (jnp.exp(jnp.sum(lp[0:1] * lp[1:2], axis=-1, keepdims=True))
           - jnp.exp(jnp.sum(lp[2:3] * lp[3:4], axis=-1, keepdims=True)) + lambda_init)
    heads = []
    for h in range(2):
        oh = o[:, 2 * h * tq:(2 * h + 1) * tq] - lam * o[:, (2 * h + 1) * tq:(2 * h + 2) * tq]
        oh = oh * lax.rsqrt(jnp.mean(oh * oh, axis=0, keepdims=True) + NORM_EPS) * sg_ref[...]
        heads.append(oh * (1.0 - lambda_init))
    o_ref[0] = jnp.concatenate(heads, axis=0).T


def _diff_attention(qd_t, kd, vd_t, lam_p, subln, lambda_init, tq):
    B, n = qd_t.shape[:2]
    S = n * tq
    pairs = DIFF_HEADS // 2
    return pl.pallas_call(
        functools.partial(_diff_attn_kernel, tq=tq, lambda_init=lambda_init),
        grid=(B, pairs, n),
        in_specs=[
            pl.BlockSpec((1, 1, LANES, tq), lambda b, h, i: (b, i, h, 0)),
            pl.BlockSpec((1, n, tq, LANES), lambda b, h, i: (b, 0, 0, h)),
            pl.BlockSpec((1, n, LANES, tq), lambda b, h, i: (b, 0, h, 0)),
            pl.BlockSpec((4, DIFF_QK_DIM), lambda b, h, i: (0, 0)),
            pl.BlockSpec((HEAD_DIM, 1), lambda b, h, i: (0, 0)),
        ],
        out_specs=pl.BlockSpec((1, tq, LANES), lambda b, h, i: (b, i, h)),
        out_shape=jax.ShapeDtypeStruct((B, S, DIFF_HEADS * HEAD_DIM), F32),
        scratch_shapes=[pltpu.VMEM((tq, 4 * tq), F32), pltpu.VMEM((tq, 4 * tq), BF16)],
        compiler_params=_params(("arbitrary", "arbitrary", "arbitrary")),
        name="diff_attention",
    )(qd_t, kd.reshape(B, n, tq, DIFF_HEADS * HEAD_DIM), vd_t, lam_p, subln.reshape(HEAD_DIM, 1))


def _nsa_compress_kernel(t_ref, pe_ref, w1_ref, w2_ref, bd_ref, kg_ref, kc_ref, vct_ref, *, nch):
    half = NSA_CMP_BLOCK // 2
    ya = jnp.zeros((nch, 2 * LANES), F32)
    yb = jnp.zeros((nch, 2 * LANES), F32)
    for l in range(half):
        rows = pl.ds(l, nch, stride=NSA_CMP_STRIDE)
        piece = jnp.concatenate([t_ref[0, 0, rows, :], t_ref[0, 1, rows, :]], axis=1)
        ya = ya + _dot(_bf(piece + pe_ref[l:l + 1]), w1_ref[l])
        yb = yb + _dot(_bf(piece + pe_ref[half + l:half + l + 1]), w1_ref[half + l])
    pre = ya + pltpu.roll(yb, nch - 1, 0)
    out = _dot(_bf(jax.nn.gelu(pre)), w2_ref[...])
    k = out[:, 0:LANES]
    hi, lo = _split2(k * k)
    ms = _dot(hi, bd_ref[...]) + _dot(lo, bd_ref[...])
    kc_ref[0] = _bf(k * lax.rsqrt(ms + NORM_EPS) * kg_ref[...])
    vct_ref[0] = _bf(out[:, LANES:2 * LANES].T)


def _nsa_compress(cmp_kv, cmp_pe, cmp_w1, cmp_w2, k_gain0):
    B, _, S, _ = cmp_kv.shape
    nch = S // NSA_CMP_STRIDE

    def over_groups(w):
        parts = [w[0], w[0], w[1], w[1]]
        zero = jnp.zeros_like(w[0])
        return jnp.concatenate(
            [jnp.concatenate([parts[r] if c == r else zero for c in range(4)], axis=-1) for r in range(4)], axis=-2)

    w1 = _bf(over_groups(cmp_w1.reshape(2, NSA_CMP_BLOCK, HEAD_DIM, HEAD_DIM)))
    w2 = _bf(over_groups(cmp_w2))
    pe = jnp.concatenate([cmp_pe[0], cmp_pe[0], cmp_pe[1], cmp_pe[1]], axis=-1)
    full = lambda a: pl.BlockSpec(a.shape, lambda b: (0,) * a.ndim)
    consts = (pe, w1, w2, _block_diag(LANES, HEAD_DIM), jnp.tile(k_gain0, 2).reshape(1, LANES))
    return pl.pallas_call(
        functools.partial(_nsa_compress_kernel, nch=nch),
        grid=(B,),
        in_specs=[pl.BlockSpec((1, 2, S, LANES), lambda b: (b, 0, 0, 0))] + [full(a) for a in consts],
        out_specs=[pl.BlockSpec((1, nch, LANES), lambda b: (b, 0, 0)), pl.BlockSpec((1, LANES, nch), lambda b: (b, 0, 0))],
        out_shape=[jax.ShapeDtypeStruct((B, nch, LANES), BF16), jax.ShapeDtypeStruct((B, LANES, nch), BF16)],
        compiler_params=_params(("arbitrary",)),
        name="nsa_compress",
    )(cmp_kv, *consts)


def _group_queries(q_ref, g):
    q = q_ref[0, 0]
    qg = jnp.concatenate([q[h * HEAD_DIM:(h + 1) * HEAD_DIM] for h in range(NSA_HPG)], axis=1)
    zero = jnp.zeros_like(qg)
    return jnp.where(g == 0, jnp.concatenate([qg, zero], axis=0), jnp.concatenate([zero, qg], axis=0))


def _store_heads(o_ref, o, tq):
    o_ref[0] = jnp.concatenate([o[:, h * tq:(h + 1) * tq] for h in range(NSA_HPG)], axis=0).T


def _nsa_cmp_attn_kernel(q_ref, k_ref, vt_ref, o_ref, sel_ref, *, tq, nch, n_sel, top_n):
    i = pl.program_id(2)
    cols = NSA_HPG * tq
    n_cmp = nch - 1
    s = _dot(k_ref[0], _group_queries(q_ref, pl.program_id(1)))
    pos = i * tq + (lax.broadcasted_iota(jnp.int32, (nch, cols), 1) & (tq - 1))
    c = lax.broadcasted_iota(jnp.int32, (nch, cols), 0)
    ok = jnp.where(c < n_cmp, c * NSA_CMP_STRIDE + (NSA_CMP_BLOCK - 1), 1 << 30) <= pos
    s = jnp.where(ok, s, NEG_INF)
    p = jnp.where(ok, jnp.exp2(s - jnp.max(s, axis=0, keepdims=True)), 0.0)
    l = jnp.sum(p, axis=0, keepdims=True)
    p = p / jnp.where(l > 0.0, l, 1.0)
    _store_heads(o_ref, _dot(vt_ref[0], _bf(p)), tq)

    pg = p[:, 0:tq]
    for h in range(1, NSA_HPG):
        pg = pg + p[:, h * tq:(h + 1) * tq]
    jj = lax.broadcasted_iota(jnp.int32, (n_sel, nch), 0) * NSA_SEL_BLOCK
    cc = lax.broadcasted_iota(jnp.int32, (n_sel, nch), 1) * NSA_CMP_STRIDE
    cover = jnp.where(cc < jj + NSA_SEL_BLOCK,
                      jnp.where(cc + NSA_CMP_BLOCK > jj, jnp.where(cc < n_cmp * NSA_CMP_STRIDE, 1.0, 0.0), 0.0), 0.0)
    cover = _bf(cover)
    hi, lo = _split2(pg)
    imp = _dot(cover, hi) + _dot(cover, lo)
    blk = lax.broadcasted_iota(jnp.int32, (n_sel, tq), 0)
    cur = (i * tq + lax.broadcasted_iota(jnp.int32, (n_sel, tq), 1)) >> 6
    imp = jnp.where(blk == cur, FORCE_SCORE, jnp.where(blk == 0, FORCE_SCORE, jnp.where(blk > cur, NEG_INF, imp)))
    blk_f = blk.astype(F32)
    sel = jnp.zeros((n_sel, tq), F32)
    for _ in range(top_n):
        mx = jnp.max(imp, axis=0, keepdims=True)
        first = jnp.min(jnp.where(imp == mx, blk_f, float(n_sel)), axis=0, keepdims=True)
        hit = blk_f == first
        sel = jnp.where(hit, 1.0, sel)
        imp = jnp.where(hit, -jnp.inf, imp)
    sel_ref[0, 0] = _bf(sel)


def _nsa_cmp_attention(qn_t, kc, vc_t, tq):
    B, n = qn_t.shape[:2]
    S = n * tq
    G = NSA_KV_GROUPS
    nch = S // NSA_CMP_STRIDE
    n_sel = S // NSA_SEL_BLOCK
    top_n = min(NSA_TOP_N, n_sel)
    gcols = NSA_HPG * HEAD_DIM
    return pl.pallas_call(
        functools.partial(_nsa_cmp_attn_kernel, tq=tq, nch=nch, n_sel=n_sel, top_n=top_n),
        grid=(B, G, n),
        in_specs=[
            pl.BlockSpec((1, 1, gcols, tq), lambda b, g, i: (b, i, g, 0)),
            pl.BlockSpec((1, nch, LANES), lambda b, g, i: (b, 0, 0)),
            pl.BlockSpec((1, HEAD_DIM, nch), lambda b, g, i: (b, g, 0)),
        ],
        out_specs=[pl.BlockSpec((1, tq, gcols), lambda b, g, i: (b, i, g)),
                   pl.BlockSpec((1, 1, n_sel, tq), lambda b, g, i: (b, g, 0, i))],
        out_shape=[jax.ShapeDtypeStruct((B, S, G * gcols), F32),
                   jax.ShapeDtypeStruct((B, G, n_sel, S), BF16)],
        compiler_params=_params(("arbitrary", "arbitrary", "arbitrary")),
        name="nsa_cmp_attention",
    )(qn_t, kc, vc_t)


def _nsa_sel_win_kernel(q_ref, ks_ref, vst_ref, kw_ref, vwt_ref, sel_ref, os_ref, ow_ref, s_ref, p_ref, *, tq, n_sel):
    i = pl.program_id(2)
    qt = _group_queries(q_ref, pl.program_id(1))
    selm = sel_ref[0, 0]
    kloc = lax.broadcasted_iota(jnp.int32, (tq, tq), 0)
    qpos = i * tq + lax.broadcasted_iota(jnp.int32, (tq, tq), 1)
    blk_row = lax.broadcasted_iota(jnp.int32, (tq, n_sel), 0)
    blk_col = lax.broadcasted_iota(jnp.int32, (tq, n_sel), 1)

    def masked(s, keep):
        return jnp.where(jnp.concatenate([keep] * NSA_HPG, axis=1) > 0.5, s, NEG_INF)

    def sel_mask(j, s):
        expand = _bf(jnp.where(blk_col == ((j * tq + blk_row) >> 6), 1.0, 0.0))
        chosen = _dot(expand, selm)
        return masked(s, jnp.where(j * tq + kloc <= qpos, chosen, 0.0))

    o = _flash_tiles(0, i, lambda j: _dot(ks_ref[0, j], qt), lambda j, p: _dot(vst_ref[0, j], p),
                     sel_mask, sel_mask, s_ref, p_ref)
    _store_heads(os_ref, o, tq)

    def win_mask(j, s):
        dist = qpos - (j * tq + kloc)
        return masked(s, jnp.where(dist >= 0, jnp.where(dist < NSA_WINDOW, 1.0, 0.0), 0.0))

    first = jnp.maximum(i - (NSA_WINDOW - 1 + tq - 1) // tq, 0)
    o = _flash_tiles(first, i, lambda j: _dot(kw_ref[0, j], qt), lambda j, p: _dot(vwt_ref[0, j], p),
                     win_mask, win_mask, s_ref, p_ref)
    _store_heads(ow_ref, o, tq)


def _nsa_sel_win(qn_t, ks, vs_t, kw, vw_t, sel, tq):
    B, n = qn_t.shape[:2]
    S = n * tq
    G = NSA_KV_GROUPS
    n_sel = S // NSA_SEL_BLOCK
    gcols = NSA_HPG * HEAD_DIM
    kspec = pl.BlockSpec((1, n, tq, LANES), lambda b, g, i: (b, 0, 0, 0))
    vtspec = pl.BlockSpec((1, n, HEAD_DIM, tq), lambda b, g, i: (b, 0, g, 0))
    ospec = pl.BlockSpec((1, tq, gcols), lambda b, g, i: (b, i, g))
    return pl.pallas_call(
        functools.partial(_nsa_sel_win_kernel, tq=tq, n_sel=n_sel),
        grid=(B, G, n),
        in_specs=[pl.BlockSpec((1, 1, gcols, tq), lambda b, g, i: (b, i, g, 0)), kspec, vtspec, kspec, vtspec,
                  pl.BlockSpec((1, 1, n_sel, tq), lambda b, g, i: (b, g, 0, i))],
        out_specs=[ospec, ospec],
        out_shape=[jax.ShapeDtypeStruct((B, S, G * gcols), F32)] * 2,
        scratch_shapes=[pltpu.VMEM((tq, NSA_HPG * tq), F32), pltpu.VMEM((tq, NSA_HPG * tq), BF16)],
        compiler_params=_params(("arbitrary", "arbitrary", "arbitrary")),
        name="nsa_sel_win_attention",
    )(qn_t, ks.reshape(B, n, tq, LANES), vs_t, kw.reshape(B, n, tq, LANES), vw_t, sel)


def _attn_outproj_kernel(x_ref, d_ref, oc_ref, os_ref, ow_ref, gt_ref, ge_ref, w_ref, o_ref):
    ghi, glo = _split2(gt_ref[...])

    def gate(r):
        return _dot(ghi, ge_ref[r]) + _dot(glo, ge_ref[r])

    nsa = gate(0) * oc_ref[...] + gate(1) * os_ref[...] + gate(2) * ow_ref[...]
    y = _dot(_bf(d_ref[...]), w_ref[0:512, :]) + _dot(_bf(nsa), w_ref[512:1024, :])
    o_ref[...] = x_ref[...] + y


def _attn_outproj(x2d, diff_o, o_cmp, o_sel, o_win, gates, w_out):
    T, D = x2d.shape
    tm = min(256, T)
    col = jnp.arange(512) // HEAD_DIM
    src = jnp.arange(LANES)
    ge = jnp.stack([_bf((src[:, None] == col[None, :] * 3 + r).astype(F32)) for r in range(3)])
    row = lambda n: pl.BlockSpec((tm, n), lambda i: (i, 0))
    return pl.pallas_call(
        _attn_outproj_kernel,
        grid=(T // tm,),
        in_specs=[row(D), row(512), row(512), row(512), row(512), row(LANES),
                  pl.BlockSpec((3, LANES, 512), lambda i: (0, 0, 0)),
                  pl.BlockSpec((D, D), lambda i: (0, 0))],
        out_specs=row(D),
        out_shape=jax.ShapeDtypeStruct((T, D), F32),
        compiler_params=_params(("arbitrary",)),
        name="attn_outproj",
    )(x2d, diff_o, o_cmp, o_sel, o_win, gates, ge, _bf(w_out))


def _hybrid_attention_layer(x, layer, norm_gain, w_in, gate_bias, dq_gain, dk_gain, lam_p, subln, nq_gain, nk_gain,
                            cmp_pe, cmp_w1, cmp_w2, w_out):
    B, S, D = x.shape
    T = B * S
    tq = min(256, S)
    qd_t, kd, vd_t, qn_t, cmp_kv, ks, vs_t, kw, vw_t, gates = _attn_inproj(
        x, norm_gain, w_in, gate_bias, dq_gain, dk_gain, nq_gain, nk_gain, tq)
    lambda_init = 0.8 - 0.6 * math.exp(-0.3 * layer)
    diff_o = _diff_attention(qd_t, kd, vd_t, lam_p, subln, lambda_init, tq)
    kc, vc_t = _nsa_compress(cmp_kv, cmp_pe, cmp_w1, cmp_w2, nk_gain[0])
    o_cmp, sel = _nsa_cmp_attention(qn_t, kc, vc_t, tq)
    o_sel, o_win = _nsa_sel_win(qn_t, ks, vs_t, kw, vw_t, sel, tq)
    flat = lambda a: a.reshape(T, a.shape[-1])
    out = _attn_outproj(flat(x), flat(diff_o), flat(o_cmp), flat(o_sel), flat(o_win), flat(gates), w_out)
    return out.reshape(B, S, D)


def _softplus(z):
    return jnp.maximum(z, 0.0) + jnp.log(1.0 + jnp.exp(-jnp.abs(z)))


def _rwkv_proj_kernel(x_ref, xp_ref, g_ref, mix_ref, wr_ref, wk_ref, wv_ref, w1_ref, w2_ref, a1_ref, a2_ref,
                      g1_ref, g2_ref, vec_ref, bd_ref,
                      r_ref, wl_ref, k_ref, v_ref, kk_ref, b_ref, bonus_ref, gate_ref):
    i = pl.program_id(1)
    gain = g_ref[...]
    xn = _rms(x_ref[0], gain)
    prev = _rms(xp_ref[0], gain)[7:8] * jnp.where(i > 0, 1.0, 0.0)
    shifted = pltpu.roll(xn, 1, 0)
    first_row = lax.broadcasted_iota(jnp.int32, xn.shape, 0) == 0
    dx = jnp.where(first_row, prev, shifted) - xn
    mix = mix_ref[...]
    xr, xw, xk, xv, xa, xg = (_bf(xn + dx * mix[j:j + 1]) for j in range(6))
    vec = vec_ref[...]
    w0, a0, k_k, k_a, r_k = (vec[j:j + 1] for j in range(5))
    r = _dot(xr, wr_ref[...])
    k = _dot(xk, wk_ref[...])
    v = _dot(xv, wv_ref[...])
    w = -_softplus(-(w0 + _dot(_bf(jnp.tanh(_dot(xw, w1_ref[...]))), w2_ref[...]))) - 0.5
    a = jax.nn.sigmoid(a0 + _dot(_bf(_dot(xa, a1_ref[...])), a2_ref[...]))
    gate_ref[0] = _dot(_bf(jax.nn.sigmoid(_dot(xg, g1_ref[...]))), g2_ref[...])
    bd = bd_ref[...]
    kk = k * k_k
    kk = kk / jnp.maximum(jnp.sqrt(_dot(_bf(kk * kk), bd)), 1e-12)
    k = k * (1.0 + (a - 1.0) * k_a)
    r_ref[0] = r
    wl_ref[0] = -jnp.exp(w)
    k_ref[0] = k
    v_ref[0] = v
    kk_ref[0] = kk
    b_ref[0] = kk * a
    bonus_ref[0] = _dot(_bf(r * k * r_k), bd) * v


def _rwkv_proj(x, gain, mix, w_r, w_k, w_v, w0, w1, w2, a0, a1, a2, g1, g2, k_k, k_a, r_k):
    B, S, D = x.shape
    tm = min(256, S)
    pad_c = lambda m, n: _bf(jnp.pad(m, ((0, 0), (0, n - m.shape[1]))))
    pad_r = lambda m, n: _bf(jnp.pad(m, ((0, n - m.shape[0]), (0, 0))))
    lw = LANES
    lg = 2 * LANES
    consts = (gain.reshape(1, D), jnp.pad(mix, ((0, 2), (0, 0))), _bf(w_r), _bf(w_k), _bf(w_v),
              pad_c(w1, lw), pad_r(w2, lw), pad_c(a1, lw), pad_r(a2, lw), pad_c(g1, lg), pad_r(g2, lg),
              jnp.pad(jnp.stack([w0, a0, k_k, k_a, r_k]), ((0, 3), (0, 0))),
              (_block_diag(D, RWKV_HEAD).astype(F32) * RWKV_HEAD).astype(BF16))
    full = lambda a: pl.BlockSpec(a.shape, lambda b, i: (0,) * a.ndim)
    tile = pl.BlockSpec((1, tm, D), lambda b, i: (b, i, 0))
    return pl.pallas_call(
        _rwkv_proj_kernel,
        grid=(B, S // tm),
        in_specs=[tile, pl.BlockSpec((1, 8, D), lambda b, i: (b, jnp.maximum(i * (tm // 8) - 1, 0), 0))]
        + [full(a) for a in consts],
        out_specs=[tile] * 8,
        out_shape=[jax.ShapeDtypeStruct((B, S, D), F32)] * 8,
        compiler_params=_params(("arbitrary", "arbitrary")),
        name="rwkv_proj",
    )(x, x, *consts)


def _rwkv_chunk_kernel(r_ref, wl_ref, k_ref, v_ref, kk_ref, b_ref, m_ref, g0_ref, rq_ref, y0_ref, *, cpb):
    C = RWKV_CHUNK
    lane = lax.broadcasted_iota(jnp.int32, (C, LANES), 1)
    head0 = lane < RWKV_HEAD
    ti = lax.broadcasted_iota(jnp.int32, (2 * C, 2 * C), 0) & (C - 1)
    tj = lax.broadcasted_iota(jnp.int32, (2 * C, 2 * C), 1) & (C - 1)
    strict = ti > tj
    incl = ti >= tj
    eye = lax.broadcasted_iota(jnp.int32, (LANES, LANES), 0) == lax.broadcasted_iota(jnp.int32, (LANES, LANES), 1)
    ltri = _bf(jnp.where(lax.broadcasted_iota(jnp.int32, (C, C), 0) >= lax.broadcasted_iota(jnp.int32, (C, C), 1),
                         1.0, 0.0))

    def stack(x):
        return jnp.concatenate([jnp.where(head0, x, 0.0), jnp.where(head0, 0.0, x)], axis=0)

    chunks = range(cpb)
    rows = [slice(c * C, (c + 1) * C) for c in chunks]
    wl = [wl_ref[0, rows[c], :] for c in chunks]
    cum = []
    for c in chunks:
        h1 = _bf(wl[c])
        r1 = wl[c] - h1.astype(F32)
        h2 = _bf(r1)
        h3 = _bf(r1 - h2.astype(F32))
        cum.append(_dot(ltri, h1) + _dot(ltri, h2) + _dot(ltri, h3))
    total = [cum[c][C - 1:C, :] for c in chunks]
    lhs_kk, lhs_r, vs, a = [], [], [], []
    for c in chunks:
        p_inv = jnp.exp(-cum[c])
        lhs_kk.append(stack(kk_ref[0, rows[c], :] * jnp.exp(cum[c] - wl[c])))
        lhs_r.append(stack(r_ref[0, rows[c], :] * jnp.exp(cum[c])))
        vs.append(_bf(stack(v_ref[0, rows[c], :])))
        a.append(_dot_nt(_bf(jnp.concatenate([lhs_kk[c], lhs_r[c]], axis=0)),
                         _bf(jnp.concatenate([stack(b_ref[0, rows[c], :] * p_inv),
                                              stack(k_ref[0, rows[c], :] * p_inv)], axis=0))))
    npow = [_bf(jnp.where(strict, a[c][0:2 * C, 0:2 * C], 0.0)) for c in chunks]
    a_k = [_bf(jnp.where(strict, a[c][0:2 * C, 2 * C:4 * C], 0.0)) for c in chunks]
    a_rb = [_bf(jnp.where(incl, a[c][2 * C:4 * C, 0:2 * C], 0.0)) for c in chunks]
    a_rk = [_bf(jnp.where(incl, a[c][2 * C:4 * C, 2 * C:4 * C], 0.0)) for c in chunks]
    x = [jnp.concatenate([_dot(a_k[c], vs[c]), lhs_kk[c]], axis=1) for c in chunks]
    x = [x[c] - _dot(npow[c], _bf(x[c])) for c in chunks]
    for _ in range(5):
        npow = [_bf(_dot(npow[c], npow[c])) for c in chunks]
        x = [x[c] + _dot(npow[c], _bf(x[c])) for c in chunks]
    u0 = [_bf(-x[c][:, 0:LANES]) for c in chunks]
    wmat = [_bf(-x[c][:, LANES:2 * LANES]) for c in chunks]
    for c in chunks:
        y0_ref[0, 0, c] = _dot(a_rk[c], vs[c]) + _dot(a_rb[c], u0[c])
        rq_ref[0, 0, c] = lhs_r[c] + _dot(a_rb[c], wmat[c])
    for c in chunks:
        p_end = jnp.exp(total[c] - cum[c])
        bc = _bf(stack(b_ref[0, rows[c], :] * p_end))
        kc = _bf(stack(k_ref[0, rows[c], :] * p_end))
        g0_ref[0, 0, c] = _dot_tn(bc, u0[c]) + _dot_tn(kc, vs[c])
        m_ref[0, 0, c] = jnp.where(eye, jnp.exp(total[c]), 0.0) + _dot_tn(bc, wmat[c])


def _rwkv_chunks(r, wl, k, v, kk, b):
    B, S, D = r.shape
    C = RWKV_CHUNK
    nc = S // C
    cpb = min(8, nc)
    hp = D // LANES
    tile = pl.BlockSpec((1, cpb * C, LANES), lambda bi, h, c: (bi, c, h))
    out = pl.BlockSpec((1, 1, cpb, LANES, LANES), lambda bi, h, c: (bi, h, c, 0, 0))
    return pl.pallas_call(
        functools.partial(_rwkv_chunk_kernel, cpb=cpb),
        grid=(B, hp, nc // cpb),
        in_specs=[tile] * 6,
        out_specs=[out] * 4,
        out_shape=[jax.ShapeDtypeStruct((B, hp, nc, LANES, LANES), F32)] * 4,
        compiler_params=_params(("arbitrary", "arbitrary", "arbitrary")),
        name="rwkv_chunk_summaries",
    )(r, wl, k, v, kk, b)


def _rwkv_scan_kernel(m_ref, g0_ref, rq_ref, y0_ref, y_ref, st_ref, *, cpb, hp):
    C = RWKV_CHUNK

    @pl.when(pl.program_id(1) == 0)
    def _():
        st_ref[...] = jnp.zeros_like(st_ref)

    for c in range(cpb):
        for h in range(hp):
            st = st_ref[h]
            s_hi, s_lo = _split2(st)
            y = y0_ref[0, h, c] + _dot(_bf(rq_ref[0, h, c]), s_hi)
            y_ref[0, c * C:(c + 1) * C, h * LANES:(h + 1) * LANES] = y[0:C] + y[C:2 * C]
            m_hi, m_lo = _split2(m_ref[0, h, c])
            st_ref[h] = g0_ref[0, h, c] + _dot(m_hi, s_hi) + _dot(m_hi, s_lo) + _dot(m_lo, s_hi)


def _rwkv_scan(m, g0, rq, y0, S):
    B, hp, nc = m.shape[:3]
    C = RWKV_CHUNK
    cpb = min(4, nc)
    blk = pl.BlockSpec((1, hp, cpb, LANES, LANES), lambda bi, c: (bi, 0, c, 0, 0))
    return pl.pallas_call(
        functools.partial(_rwkv_scan_kernel, cpb=cpb, hp=hp),
        grid=(B, nc // cpb),
        in_specs=[blk] * 4,
        out_specs=pl.BlockSpec((1, cpb * C, hp * LANES), lambda bi, c: (bi, c, 0)),
        out_shape=jax.ShapeDtypeStruct((B, S, hp * LANES), F32),
        scratch_shapes=[pltpu.VMEM((hp, LANES, LANES), F32)],
        compiler_params=_params(("arbitrary", "arbitrary")),
        name="rwkv_state_scan",
    )(m, g0, rq, y0)


def _rwkv_out_kernel(x_ref, y_ref, bonus_ref, gate_ref, lnw_ref, lnb_ref, bd_ref, wo_ref, o_ref):
    y = y_ref[...]
    bd = bd_ref[...]
    hi, lo = _split2(y)
    mu = _dot(hi, bd) + _dot(lo, bd)
    dev = y - mu
    hi, lo = _split2(dev * dev)
    var = _dot(hi, bd) + _dot(lo, bd)
    yn = dev * lax.rsqrt(var + RWKV_LNX_EPS) * lnw_ref[...] + lnb_ref[...]
    z = (yn + bonus_ref[...]) * gate_ref[...]
    o_ref[...] = x_ref[...] + _dot(_bf(z), wo_ref[...])


def _rwkv_out(x2d, y, bonus, gate, ln_w, ln_b, w_o):
    T, D = x2d.shape
    tm = min(256, T)
    row = pl.BlockSpec((tm, D), lambda i: (i, 0))
    vecs = pl.BlockSpec((1, D), lambda i: (0, 0))
    mat = pl.BlockSpec((D, D), lambda i: (0, 0))
    return pl.pallas_call(
        _rwkv_out_kernel,
        grid=(T // tm,),
        in_specs=[row, row, row, row, vecs, vecs, mat, mat],
        out_specs=row,
        out_shape=jax.ShapeDtypeStruct((T, D), F32),
        compiler_params=_params(("arbitrary",)),
        name="rwkv_out",
    )(x2d, y, bonus, gate, ln_w.reshape(1, D), ln_b.reshape(1, D), _block_diag(D, RWKV_HEAD), _bf(w_o))


def _rwkv_layer(x, norm_gain, mix, w_r, w_k, w_v, w0, w1, w2, a0, a1, a2, g1, g2, k_k, k_a, r_k, ln_w, ln_b, w_o):
    B, S, D = x.shape
    r, wl, k, v, kk, b, bonus, gate = _rwkv_proj(x, norm_gain, mix, w_r, w_k, w_v, w0, w1, w2, a0, a1, a2,
                                                 g1, g2, k_k, k_a, r_k)
    m, g0, rq, y0 = _rwkv_chunks(r, wl, k, v, kk, b)
    y = _rwkv_scan(m, g0, rq, y0, S)
    T = B * S
    out = _rwkv_out(x.reshape(T, D), y.reshape(T, D), bonus.reshape(T, D), gate.reshape(T, D), ln_w, ln_b, w_o)
    return out.reshape(B, S, D)


def _router_kernel(x_ref, g_ref, whi_ref, wlo_ref, b_ref, xn_ref, route_ref):
    xn = _rms(x_ref[...], g_ref[...])
    xn_ref[...] = xn
    hi, lo = _split2(xn)
    logits = _dot(hi, whi_ref[...]) + _dot(hi, wlo_ref[...]) + _dot(lo, whi_ref[...]) + b_ref[...]
    lane = lax.broadcasted_iota(jnp.int32, logits.shape, 1)
    lane_f = lane.astype(F32)

    def top(vals):
        mx = jnp.max(vals, axis=-1, keepdims=True)
        return mx, jnp.min(jnp.where(vals == mx, lane_f, float(LANES)), axis=-1, keepdims=True)

    glog = jnp.where(lane < N_GROUPS, logits, NEG_INF)
    gmax, gidx = top(glog)
    gsum = jnp.sum(jnp.where(lane < N_GROUPS, jnp.exp(glog - gmax), 0.0), axis=-1, keepdims=True)
    grp_p = 1.0 / gsum
    first = float(N_GROUPS) + EXPERTS_PER_GROUP * gidx
    elog = jnp.where(lane_f >= first, jnp.where(lane_f < first + EXPERTS_PER_GROUP, logits, NEG_INF), NEG_INF)
    v1, i1 = top(elog)
    v2, i2 = top(jnp.where(lane_f == i1, NEG_INF, elog))
    e2 = jnp.exp(v2 - v1)
    gate1 = grp_p / (1.0 + e2)
    gate2 = grp_p * e2 / (1.0 + e2)
    route_ref[...] = jnp.where(lane == 0, i1 - N_GROUPS, jnp.where(lane == 1, i2 - N_GROUPS,
                               jnp.where(lane == 2, gate1, jnp.where(lane == 3, gate2, 0.0))))


def _router(x2d, gain, wg, bg, we, be):
    T, D = x2d.shape
    tm = min(256, T)
    w = jnp.pad(jnp.concatenate([wg, we], axis=1), ((0, 0), (0, LANES - N_GROUPS - N_EXPERTS)))
    w_hi = _bf(w)
    w_lo = _bf(w - w_hi.astype(F32))
    bias = jnp.pad(jnp.concatenate([bg, be]), (0, LANES - N_GROUPS - N_EXPERTS)).reshape(1, LANES)
    row = lambda n: pl.BlockSpec((tm, n), lambda i: (i, 0))
    full = lambda a: pl.BlockSpec(a.shape, lambda i: (0, 0))
    consts = (gain.reshape(1, D), w_hi, w_lo, bias)
    return pl.pallas_call(
        _router_kernel,
        grid=(T // tm,),
        in_specs=[row(D)] + [full(a) for a in consts],
        out_specs=[row(D), row(LANES)],
        out_shape=[jax.ShapeDtypeStruct((T, D), F32), jax.ShapeDtypeStruct((T, LANES), F32)],
        compiler_params=_params(("arbitrary",)),
        name="moe_router",
    )(x2d, *consts)


def _row_gather_start(src_hbm, idx_ref, idx_base, idx_stride, dst, sem, rows):
    def body(r, carry):
        src_row = idx_ref[idx_base + r * idx_stride]
        pltpu.make_async_copy(src_hbm.at[pl.ds(src_row, 1)], dst.at[pl.ds(r, 1)], sem).start()
        return carry
    lax.fori_loop(0, rows, body, 0, unroll=8)


def _row_gather_wait(src_hbm, dst, sem, rows):
    pltpu.make_async_copy(src_hbm.at[pl.ds(0, rows)], dst, sem).wait()


def _expert_kernel(be_ref, nu_ref, tok_ref, xn_hbm, wg_ref, wu_ref, wd_ref, o_ref, xbuf, sem):
    i = pl.program_id(0)
    nu = nu_ref[0]
    cur = i & 1
    last = pl.num_programs(0) - 1

    @pl.when(i == 0)
    def _():
        _row_gather_start(xn_hbm, tok_ref, 0, 1, xbuf.at[0], sem.at[0], MOE_ROWS)

    @pl.when(i < nu)
    def _():
        nxt = jnp.minimum(i + 1, last)
        _row_gather_start(xn_hbm, tok_ref, nxt * MOE_ROWS, 1, xbuf.at[1 - cur], sem.at[1 - cur], MOE_ROWS)
        _row_gather_wait(xn_hbm, xbuf.at[cur], sem.at[cur], MOE_ROWS)
        x = _bf(xbuf[cur])
        hid = jax.nn.silu(_dot(x, _bf(wg_ref[0]))) * _dot(x, _bf(wu_ref[0]))
        o_ref[...] = _dot(_bf(hid), _bf(wd_ref[0]))

        @pl.when(i == nu - 1)
        def _():
            _row_gather_wait(xn_hbm, xbuf.at[1 - cur], sem.at[1 - cur], MOE_ROWS)

    @pl.when(i >= nu)
    def _():
        o_ref[...] = jnp.zeros_like(o_ref)


def _experts(xn, slot_tok, block_e, n_used, e_gate, e_up, e_down):
    T, D = xn.shape
    n_blocks = slot_tok.shape[0] // MOE_ROWS
    wspec = lambda shp: pl.BlockSpec((1,) + shp, lambda i, be, nu, tok: (be[i], 0, 0))
    return pl.pallas_call(
        _expert_kernel,
        grid_spec=pltpu.PrefetchScalarGridSpec(
            num_scalar_prefetch=3,
            grid=(n_blocks,),
            in_specs=[pl.BlockSpec(memory_space=pl.ANY), wspec((D, D_EXPERT)), wspec((D, D_EXPERT)),
                      wspec((D_EXPERT, D))],
            out_specs=pl.BlockSpec((MOE_ROWS, D), lambda i, be, nu, tok: (i, 0)),
            scratch_shapes=[pltpu.VMEM((2, MOE_ROWS, D), F32), pltpu.SemaphoreType.DMA((2,))],
        ),
        out_shape=jax.ShapeDtypeStruct((n_blocks * MOE_ROWS, D), F32),
        compiler_params=_params(("arbitrary",)),
        name="moe_experts",
    )(block_e, n_used, slot_tok, xn, e_gate, e_up, e_down)


def _combine_kernel(dest_ref, x_ref, route_ref, ys_hbm, o_ref, buf, sem, *, tm):
    i = pl.program_id(0)
    cur = i & 1

    def start(tile, par):
        for k in range(2):
            _row_gather_start(ys_hbm, dest_ref, tile * (2 * tm) + k, 2, buf.at[par, k], sem.at[par], tm)

    @pl.when(i == 0)
    def _():
        start(0, 0)

    @pl.when(i + 1 < pl.num_programs(0))
    def _():
        start(i + 1, 1 - cur)

    for k in range(2):
        _row_gather_wait(ys_hbm, buf.at[cur, k], sem.at[cur], tm)
    route = route_ref[...]
    o_ref[...] = x_ref[...] + route[:, 2:3] * buf[cur, 0] + route[:, 3:4] * buf[cur, 1]


def _combine(x2d, y_slots, dest, route):
    T, D = x2d.shape
    tm = min(256, T)
    row = lambda n: pl.BlockSpec((tm, n), lambda i, dest: (i, 0))
    return pl.pallas_call(
        functools.partial(_combine_kernel, tm=tm),
        grid_spec=pltpu.PrefetchScalarGridSpec(
            num_scalar_prefetch=1,
            grid=(T // tm,),
            in_specs=[row(D), row(LANES), pl.BlockSpec(memory_space=pl.ANY)],
            out_specs=row(D),
            scratch_shapes=[pltpu.VMEM((2, 2, tm, D), F32), pltpu.SemaphoreType.DMA((2,))],
        ),
        out_shape=jax.ShapeDtypeStruct((T, D), F32),
        compiler_params=_params(("arbitrary",)),
        name="moe_combine",
    )(dest, x2d, route, y_slots)


def _moe_layer(x, gain, wg, bg, we, be, e_gate, e_up, e_down):
    B, S, D = x.shape
    T = B * S
    x2d = x.reshape(T, D)
    xn, route = _router(x2d, gain, wg, bg, we, be)
    flat_e = route[:, 0:2].astype(jnp.int32).reshape(-1)
    n_pairs = 2 * T
    onehot = (flat_e[:, None] == jnp.arange(N_EXPERTS, dtype=jnp.int32)[None, :]).astype(jnp.int32)
    csum = jnp.cumsum(onehot, axis=0)
    rank = jnp.take_along_axis(csum, flat_e[:, None], axis=1)[:, 0] - 1
    counts = csum[-1]
    padded = (counts + MOE_ROWS - 1) // MOE_ROWS * MOE_ROWS
    pad_end = jnp.cumsum(padded)
    dest = (pad_end - padded)[flat_e] + rank
    n_blocks = -(-n_pairs // MOE_ROWS) + N_EXPERTS
    slot_tok = jnp.zeros((n_blocks * MOE_ROWS,), jnp.int32).at[dest].set(jnp.arange(n_pairs, dtype=jnp.int32) // 2)
    block_start = jnp.arange(n_blocks, dtype=jnp.int32) * MOE_ROWS
    block_e = jnp.minimum(jnp.sum((pad_end[None, :] <= block_start[:, None]).astype(jnp.int32), axis=1),
                          N_EXPERTS - 1)
    n_used = (pad_end[-1:] // MOE_ROWS).astype(jnp.int32)
    y_slots = _experts(xn, slot_tok, block_e, n_used, e_gate, e_up, e_down)
    return _combine(x2d, y_slots, dest.astype(jnp.int32), route).reshape(B, S, D)


def kernel(x, mix_norm, attn_w_in, attn_gate_bias, diff_q_norm, diff_k_norm, diff_lambda, diff_subln, nsa_q_norm, nsa_k_norm, nsa_cmp_pe, nsa_cmp_w1, nsa_cmp_w2, attn_w_out, rwkv_mix, rwkv_w_r, rwkv_w_k, rwkv_w_v, rwkv_decay_w0, rwkv_decay_w1, rwkv_decay_w2, rwkv_iclr_a0, rwkv_iclr_a1, rwkv_iclr_a2, rwkv_gate_g1, rwkv_gate_g2, rwkv_k_k, rwkv_k_a, rwkv_r_k, rwkv_ln_w, rwkv_ln_b, rwkv_w_o, ffn_norm, router_group_w, router_group_b, router_expert_w, router_expert_b, expert_w_gate, expert_w_up, expert_w_down):
    depth = mix_norm.shape[0]
    for layer in range(depth):
        i = layer // 2
        if layer % 2 == 0:
            x = _hybrid_attention_layer(
                x, layer, mix_norm[layer], attn_w_in[i], attn_gate_bias[i], diff_q_norm[i], diff_k_norm[i],
                diff_lambda[i], diff_subln[i], nsa_q_norm[i], nsa_k_norm[i], nsa_cmp_pe[i], nsa_cmp_w1[i],
                nsa_cmp_w2[i], attn_w_out[i])
        else:
            x = _rwkv_layer(
                x, mix_norm[layer], rwkv_mix[i], rwkv_w_r[i], rwkv_w_k[i], rwkv_w_v[i], rwkv_decay_w0[i],
                rwkv_decay_w1[i], rwkv_decay_w2[i], rwkv_iclr_a0[i], rwkv_iclr_a1[i], rwkv_iclr_a2[i],
                rwkv_gate_g1[i], rwkv_gate_g2[i], rwkv_k_k[i], rwkv_k_a[i], rwkv_r_k[i], rwkv_ln_w[i],
                rwkv_ln_b[i], rwkv_w_o[i])
        x = _moe_layer(x, ffn_norm[layer], router_group_w[layer], router_group_b[layer], router_expert_w[layer],
                       router_expert_b[layer], expert_w_gate[layer], expert_w_up[layer], expert_w_down[layer])
    return x
```

```python
import functools
import math

import jax
import jax.numpy as jnp
from jax import lax
from jax.experimental import pallas as pl
from jax.experimental.pallas import tpu as pltpu

F32 = jnp.float32
BF16 = jnp.bfloat16

D_MODEL = 1024
HEAD_DIM = 64
NORM_EPS = 1e-6
NEG_INF = -1e30
FORCE_SCORE = 1e4

DIFF_HEADS = 8
DIFF_QK_DIM = 32
NSA_HEADS = 8
NSA_KV_GROUPS = 2
NSA_HPG = 4
NSA_CMP_BLOCK = 32
NSA_CMP_STRIDE = 16
NSA_SEL_BLOCK = 64
NSA_TOP_N = 16
NSA_WINDOW = 512
IN_COLS = 2840
IN_COLS_PAD = 2944
GATE_COLS = 24

RWKV_HEAD = 64
RWKV_LNX_EPS = 64e-5
RWKV_CHUNK = 64

N_GROUPS = 4
EXPERTS_PER_GROUP = 8
N_EXPERTS = 32
D_EXPERT = 256
MOE_ROWS = 256

LANES = 128
VMEM_LIMIT = 56 * 1024 * 1024


def _bf(x):
    return x.astype(BF16)


def _dot(a, b):
    return jnp.dot(a, b, preferred_element_type=F32)


def _dot_nt(a, b):
    return lax.dot_general(a, b, (((1,), (1,)), ((), ())), preferred_element_type=F32)


def _dot_tn(a, b):
    return lax.dot_general(a, b, (((0,), (0,)), ((), ())), preferred_element_type=F32)


def _split2(x):
    hi = _bf(x)
    lo = _bf(x - hi.astype(F32))
    return hi, lo


def _params(sem):
    return pltpu.CompilerParams(dimension_semantics=sem, vmem_limit_bytes=VMEM_LIMIT)


def _block_diag(n, group):
    r = jnp.arange(n) // group
    return ((r[:, None] == r[None, :]).astype(F32) / group).astype(BF16)


def _rms(x, gain):
    return x * lax.rsqrt(jnp.mean(x * x, axis=-1, keepdims=True) + NORM_EPS) * gain


def _inproj_kernel(x_ref, g_ref, wa_ref, wbt_ref, b_ref, bd32_ref, bd64_ref, gk_ref, gnk_ref, gq_ref, gnq_ref,
                   qd_ref, kd_ref, vd_ref, qn_ref, cmp_ref, ks_ref, vs_ref, kw_ref, vw_ref, gt_ref):
    xn = _bf(_rms(x_ref[0], g_ref[...]))
    ha = _dot(xn, wa_ref[...])
    hb = _dot_nt(wbt_ref[...], xn)
    bd32 = bd32_ref[...]
    bd64 = bd64_ref[...]

    def gnorm(seg, bd, gain):
        hi, lo = _split2(seg * seg)
        return seg * lax.rsqrt(_dot(hi, bd) + _dot(lo, bd) + NORM_EPS) * gain

    def gnorm_t(seg, bd, gain):
        hi, lo = _split2(seg * seg)
        return seg * lax.rsqrt(_dot(bd, hi) + _dot(bd, lo) + NORM_EPS) * gain

    kd_ref[0] = _bf(gnorm(ha[:, 0:512], bd32, gk_ref[...]))
    cmp_ref[0, 0] = ha[:, 512:640]
    cmp_ref[0, 1] = ha[:, 640:768]
    ksw = gnorm(ha[:, 768:1024], bd64[0:256, 0:256], gnk_ref[...])
    ks_ref[0] = _bf(ksw[:, 0:128])
    kw_ref[0] = _bf(ksw[:, 128:256])
    gt_ref[0] = jax.nn.sigmoid(ha[:, 1024:1152] + b_ref[...])
    qd_ref[0, 0] = _bf(gnorm_t(hb[0:512], bd32, gq_ref[...]))
    vd_ref[0, 0] = _bf(hb[512:1024])
    qn_ref[0, 0] = _bf(gnorm_t(hb[1024:1536], bd64, gnq_ref[...]))
    vs_ref[0, 0] = _bf(hb[1536:1664])
    vw_ref[0, 0] = _bf(hb[1664:1792])


def _attn_inproj(x, gain, w_in, gate_bias, dq_gain, dk_gain, nq_gain, nk_gain, tm):
    B, S, D = x.shape
    n = S // tm
    c = lambda lo, hi: w_in[:, lo:hi]
    gate_w = jnp.pad(c(2816, IN_COLS), ((0, 0), (0, LANES - GATE_COLS)))
    wa = _bf(jnp.concatenate([c(512, 1024), c(2048, 2304), c(2304, 2432), c(2560, 2688), gate_w], axis=1))
    wbt = _bf(jnp.concatenate([c(0, 512), c(1024, 1536), c(1536, 2048), c(2432, 2560), c(2688, 2816)], axis=1).T)
    bias = jnp.pad(gate_bias, (0, LANES - GATE_COLS)).reshape(1, LANES)
    log2e = math.log2(math.e)
    gk = jnp.tile(dk_gain, 16).reshape(1, 512)
    gnk = jnp.concatenate([jnp.tile(nk_gain[1], 2), jnp.tile(nk_gain[2], 2)]).reshape(1, 256)
    gq = (jnp.tile(dq_gain, 16) * (DIFF_QK_DIM ** -0.5 * log2e)).reshape(512, 1)
    gnq = (jnp.tile(nq_gain, 8) * (HEAD_DIM ** -0.5 * log2e)).reshape(512, 1)
    consts = (gain.reshape(1, D), wa, wbt, bias, _block_diag(512, 32), _block_diag(512, 64), gk, gnk, gq, gnq)
    full = lambda a: pl.BlockSpec(a.shape, lambda b, i: (0,) * a.ndim)
    tok = lambda w: pl.BlockSpec((1, tm, w), lambda b, i: (b, i, 0))
    tr = lambda r: pl.BlockSpec((1, 1, r, tm), lambda b, i: (b, i, 0, 0))
    tok_shape = lambda w, dt: jax.ShapeDtypeStruct((B, S, w), dt)
    tr_shape = lambda r: jax.ShapeDtypeStruct((B, n, r, tm), BF16)
    return pl.pallas_call(
        _inproj_kernel,
        grid=(B, n),
        in_specs=[tok(D)] + [full(a) for a in consts],
        out_specs=[tr(512), tok(512), tr(512), tr(512), pl.BlockSpec((1, 2, tm, LANES), lambda b, i: (b, 0, i, 0)),
                   tok(LANES), tr(LANES), tok(LANES), tr(LANES), tok(LANES)],
        out_shape=[tr_shape(512), tok_shape(512, BF16), tr_shape(512), tr_shape(512),
                   jax.ShapeDtypeStruct((B, 2, S, LANES), F32),
                   tok_shape(LANES, BF16), tr_shape(LANES), tok_shape(LANES, BF16), tr_shape(LANES),
                   tok_shape(LANES, F32)],
        compiler_params=_params(("arbitrary", "arbitrary")),
        name="attn_inproj",
    )(x, *consts)


def _flash_tiles(lo, hi, scores, values, mask_body, mask_last, s_ref, p_ref):
    tk, cols = s_ref.shape
    dv = HEAD_DIM
    p_ref[...] = jnp.zeros_like(p_ref)
    s_ref[...] = scores(lo)

    def process(j, carry, mask, s_next):
        m, l, acc = carry
        s = s_ref[...]
        if mask is not None:
            s = mask(j, s)
        pv = values(jnp.maximum(j - 1, lo), p_ref[...])
        m_new = jnp.maximum(m, jnp.max(s, axis=0, keepdims=True))
        alpha = jnp.exp2(m - m_new)
        p = jnp.exp2(s - m_new)
        l = alpha * l + jnp.sum(p, axis=0, keepdims=True)
        acc = (acc + pv) * alpha
        p_ref[...] = _bf(p)
        if s_next is not None:
            s_ref[...] = s_next
        return m_new, l, acc

    init = (jnp.full((1, cols), NEG_INF, F32), jnp.zeros((1, cols), F32), jnp.zeros((dv, cols), F32))
    carry = lax.fori_loop(lo, hi, lambda j, c: process(j, c, mask_body, scores(j + 1)), init)
    m, l, acc = process(hi, carry, mask_last, None)
    return (acc + values(hi, p_ref[...])) / l


def _diff_attn_kernel(qt_ref, k_ref, vt_ref, lam_ref, sg_ref, o_ref, s_ref, p_ref, *, tq, lambda_init):
    i = pl.program_id(2)
    q = qt_ref[0, 0]
    zero = jnp.zeros((DIFF_QK_DIM, tq), BF16)
    qb = jnp.concatenate(
        [jnp.concatenate([q[r * DIFF_QK_DIM:(r + 1) * DIFF_QK_DIM] if c == r else zero for c in range(4)], axis=1)
         for r in range(4)], axis=0)

    def causal(j, s):
        kpos = lax.broadcasted_iota(jnp.int32, (tq, 4 * tq), 0)
        qpos = lax.broadcasted_iota(jnp.int32, (tq, 4 * tq), 1) & (tq - 1)
        return jnp.where(kpos <= qpos, s, NEG_INF)

    def values(j, p):
        vt = vt_ref[0, j]
        return jnp.concatenate([_dot(vt[0:HEAD_DIM], p[:, 0:2 * tq]),
                                _dot(vt[HEAD_DIM:2 * HEAD_DIM], p[:, 2 * tq:4 * tq])], axis=1)

    o = _flash_tiles(0, i, lambda j: _dot(k_ref[0, j], qb), values, None, causal, s_ref, p_ref)
    lp = lam_ref[...]
    lam = (jnp.exp(jnp.sum(lp[0:1] * lp[1:2], axis=-1, keepdims=True))
           - jnp.exp(jnp.sum(lp[2:3] * lp[3:4], axis=-1, keepdims=True)) + lambda_init)
    heads = []
    for h in range(2):
        oh = o[:, 2 * h * tq:(2 * h + 1) * tq] - lam * o[:, (2 * h + 1) * tq:(2 * h + 2) * tq]
        oh = oh * lax.rsqrt(jnp.mean(oh * oh, axis=0, keepdims=True) + NORM_EPS) * sg_ref[...]
        heads.append(oh * (1.0 - lambda_init))
    o_ref[0] = jnp.concatenate(heads, axis=0).T


def _diff_attention(qd_t, kd, vd_t, lam_p, subln, lambda_init, tq):
    B, n = qd_t.shape[:2]
    S = n * tq
    pairs = DIFF_HEADS // 2
    return pl.pallas_call(
        functools.partial(_diff_attn_kernel, tq=tq, lambda_init=lambda_init),
        grid=(B, pairs, n),
        in_specs=[
            pl.BlockSpec((1, 1, LANES, tq), lambda b, h, i: (b, i, h, 0)),
            pl.BlockSpec((1, n, tq, LANES), lambda b, h, i: (b, 0, 0, h)),
            pl.BlockSpec((1, n, LANES, tq), lambda b, h, i: (b, 0, h, 0)),
            pl.BlockSpec((4, DIFF_QK_DIM), lambda b, h, i: (0, 0)),
            pl.BlockSpec((HEAD_DIM, 1), lambda b, h, i: (0, 0)),
        ],
        out_specs=pl.BlockSpec((1, tq, LANES), lambda b, h, i: (b, i, h)),
        out_shape=jax.ShapeDtypeStruct((B, S, DIFF_HEADS * HEAD_DIM), F32),
        scratch_shapes=[pltpu.VMEM((tq, 4 * tq), F32), pltpu.VMEM((tq, 4 * tq), BF16)],
        compiler_params=_params(("arbitrary", "arbitrary", "arbitrary")),
        name="diff_attention",
    )(qd_t, kd.reshape(B, n, tq, DIFF_HEADS * HEAD_DIM), vd_t, lam_p, subln.reshape(HEAD_DIM, 1))


def _nsa_compress_kernel(t_ref, pe_ref, w1_ref, w2_ref, bd_ref, kg_ref, kc_ref, vct_ref, *, nch):
    half = NSA_CMP_BLOCK // 2
    ya = jnp.zeros((nch, 2 * LANES), F32)
    yb = jnp.zeros((nch, 2 * LANES), F32)
    for l in range(half):
        rows = pl.ds(l, nch, stride=NSA_CMP_STRIDE)
        piece = jnp.concatenate([t_ref[0, 0, rows, :], t_ref[0, 1, rows, :]], axis=1)
        ya = ya + _dot(_bf(piece + pe_ref[l:l + 1]), w1_ref[l])
        yb = yb + _dot(_bf(piece + pe_ref[half + l:half + l + 1]), w1_ref[half + l])
    pre = ya + pltpu.roll(yb, nch - 1, 0)
    out = _dot(_bf(jax.nn.gelu(pre)), w2_ref[...])
    k = out[:, 0:LANES]
    hi, lo = _split2(k * k)
    ms = _dot(hi, bd_ref[...]) + _dot(lo, bd_ref[...])
    kc_ref[0] = _bf(k * lax.rsqrt(ms + NORM_EPS) * kg_ref[...])
    vct_ref[0] = _bf(out[:, LANES:2 * LANES].T)


def _nsa_compress(cmp_kv, cmp_pe, cmp_w1, cmp_w2, k_gain0):
    B, _, S, _ = cmp_kv.shape
    nch = S // NSA_CMP_STRIDE

    def over_groups(w):
        parts = [w[0], w[0], w[1], w[1]]
        zero = jnp.zeros_like(w[0])
        return jnp.concatenate(
            [jnp.concatenate([parts[r] if c == r else zero for c in range(4)], axis=-1) for r in range(4)], axis=-2)

    w1 = _bf(over_groups(cmp_w1.reshape(2, NSA_CMP_BLOCK, HEAD_DIM, HEAD_DIM)))
    w2 = _bf(over_groups(cmp_w2))
    pe = jnp.concatenate([cmp_pe[0], cmp_pe[0], cmp_pe[1], cmp_pe[1]], axis=-1)
    full = lambda a: pl.BlockSpec(a.shape, lambda b: (0,) * a.ndim)
    consts = (pe, w1, w2, _block_diag(LANES, HEAD_DIM), jnp.tile(k_gain0, 2).reshape(1, LANES))
    return pl.pallas_call(
        functools.partial(_nsa_compress_kernel, nch=nch),
        grid=(B,),
        in_specs=[pl.BlockSpec((1, 2, S, LANES), lambda b: (b, 0, 0, 0))] + [full(a) for a in consts],
        out_specs=[pl.BlockSpec((1, nch, LANES), lambda b: (b, 0, 0)), pl.BlockSpec((1, LANES, nch), lambda b: (b, 0, 0))],
        out_shape=[jax.ShapeDtypeStruct((B, nch, LANES), BF16), jax.ShapeDtypeStruct((B, LANES, nch), BF16)],
        compiler_params=_params(("arbitrary",)),
        name="nsa_compress",
    )(cmp_kv, *consts)


def _group_queries(q_ref, g):
    q = q_ref[0, 0]
    qg = jnp.concatenate([q[h * HEAD_DIM:(h + 1) * HEAD_DIM] for h in range(NSA_HPG)], axis=1)
    zero = jnp.zeros_like(qg)
    return jnp.where(g == 0, jnp.concatenate([qg, zero], axis=0), jnp.concatenate([zero, qg], axis=0))


def _store_heads(o_ref, o, tq):
    o_ref[0] = jnp.concatenate([o[:, h * tq:(h + 1) * tq] for h in range(NSA_HPG)], axis=0).T


def _nsa_cmp_attn_kernel(q_ref, k_ref, vt_ref, o_ref, sel_ref, *, tq, nch, n_sel, top_n):
    i = pl.program_id(2)
    cols = NSA_HPG * tq
    n_cmp = nch - 1
    s = _dot(k_ref[0], _group_queries(q_ref, pl.program_id(1)))
    pos = i * tq + (lax.broadcasted_iota(jnp.int32, (nch, cols), 1) & (tq - 1))
    c = lax.broadcasted_iota(jnp.int32, (nch, cols), 0)
    ok = jnp.where(c < n_cmp, c * NSA_CMP_STRIDE + (NSA_CMP_BLOCK - 1), 1 << 30) <= pos
    s = jnp.where(ok, s, NEG_INF)
    p = jnp.where(ok, jnp.exp2(s - jnp.max(s, axis=0, keepdims=True)), 0.0)
    l = jnp.sum(p, axis=0, keepdims=True)
    p = p / jnp.where(l > 0.0, l, 1.0)
    _store_heads(o_ref, _dot(vt_ref[0], _bf(p)), tq)

    pg = p[:, 0:tq]
    for h in range(1, NSA_HPG):
        pg = pg + p[:, h * tq:(h + 1) * tq]
    jj = lax.broadcasted_iota(jnp.int32, (n_sel, nch), 0) * NSA_SEL_BLOCK
    cc = lax.broadcasted_iota(jnp.int32, (n_sel, nch), 1) * NSA_CMP_STRIDE
    cover = jnp.where(cc < jj + NSA_SEL_BLOCK,
                      jnp.where(cc + NSA_CMP_BLOCK > jj, jnp.where(cc < n_cmp * NSA_CMP_STRIDE, 1.0, 0.0), 0.0), 0.0)
    cover = _bf(cover)
    hi, lo = _split2(pg)
    imp = _dot(cover, hi) + _dot(cover, lo)
    blk = lax.broadcasted_iota(jnp.int32, (n_sel, tq), 0)
    cur = (i * tq + lax.broadcasted_iota(jnp.int32, (n_sel, tq), 1)) >> 6
    imp = jnp.where(blk == cur, FORCE_SCORE, jnp.where(blk == 0, FORCE_SCORE, jnp.where(blk > cur, NEG_INF, imp)))
    blk_f = blk.astype(F32)
    sel = jnp.zeros((n_sel, tq), F32)
    for _ in range(top_n):
        mx = jnp.max(imp, axis=0, keepdims=True)
        first = jnp.min(jnp.where(imp == mx, blk_f, float(n_sel)), axis=0, keepdims=True)
        hit = blk_f == first
        sel = jnp.where(hit, 1.0, sel)
        imp = jnp.where(hit, -jnp.inf, imp)
    sel_ref[0, 0] = _bf(sel)


def _nsa_cmp_attention(qn_t, kc, vc_t, tq):
    B, n = qn_t.shape[:2]
    S = n * tq
    G = NSA_KV_GROUPS
    nch = S // NSA_CMP_STRIDE
    n_sel = S // NSA_SEL_BLOCK
    top_n = min(NSA_TOP_N, n_sel)
    gcols = NSA_HPG * HEAD_DIM
    return pl.pallas_call(
        functools.partial(_nsa_cmp_attn_kernel, tq=tq, nch=nch, n_sel=n_sel, top_n=top_n),
        grid=(B, G, n),
        in_specs=[
            pl.BlockSpec((1, 1, gcols, tq), lambda b, g, i: (b, i, g, 0)),
            pl.BlockSpec((1, nch, LANES), lambda b, g, i: (b, 0, 0)),
            pl.BlockSpec((1, HEAD_DIM, nch), lambda b, g, i: (b, g, 0)),
        ],
        out_specs=[pl.BlockSpec((1, tq, gcols), lambda b, g, i: (b, i, g)),
                   pl.BlockSpec((1, 1, n_sel, tq), lambda b, g, i: (b, g, 0, i))],
        out_shape=[jax.ShapeDtypeStruct((B, S, G * gcols), F32),
                   jax.ShapeDtypeStruct((B, G, n_sel, S), BF16)],
        compiler_params=_params(("arbitrary", "arbitrary", "arbitrary")),
        name="nsa_cmp_attention",
    )(qn_t, kc, vc_t)


def _nsa_sel_win_kernel(q_ref, ks_ref, vst_ref, kw_ref, vwt_ref, sel_ref, os_ref, ow_ref, s_ref, p_ref, *, tq, n_sel):
    i = pl.program_id(2)
    qt = _group_queries(q_ref, pl.program_id(1))
    selm = sel_ref[0, 0]
    kloc = lax.broadcasted_iota(jnp.int32, (tq, tq), 0)
    qpos = i * tq + lax.broadcasted_iota(jnp.int32, (tq, tq), 1)
    blk_row = lax.broadcasted_iota(jnp.int32, (tq, n_sel), 0)
    blk_col = lax.broadcasted_iota(jnp.int32, (tq, n_sel), 1)

    def masked(s, keep):
        return jnp.where(jnp.concatenate([keep] * NSA_HPG, axis=1) > 0.5, s, NEG_INF)

    def sel_mask(j, s):
        expand = _bf(jnp.where(blk_col == ((j * tq + blk_row) >> 6), 1.0, 0.0))
        chosen = _dot(expand, selm)
        return masked(s, jnp.where(j * tq + kloc <= qpos, chosen, 0.0))

    o = _flash_tiles(0, i, lambda j: _dot(ks_ref[0, j], qt), lambda j, p: _dot(vst_ref[0, j], p),
                     sel_mask, sel_mask, s_ref, p_ref)
    _store_heads(os_ref, o, tq)

    def win_mask(j, s):
        dist = qpos - (j * tq + kloc)
        return masked(s, jnp.where(dist >= 0, jnp.where(dist < NSA_WINDOW, 1.0, 0.0), 0.0))

    first = jnp.maximum(i - (NSA_WINDOW - 1 + tq - 1) // tq, 0)
    o = _flash_tiles(first, i, lambda j: _dot(kw_ref[0, j], qt), lambda j, p: _dot(vwt_ref[0, j], p),
                     win_mask, win_mask, s_ref, p_ref)
    _store_heads(ow_ref, o, tq)


def _nsa_sel_win(qn_t, ks, vs_t, kw, vw_t, sel, tq):
    B, n = qn_t.shape[:2]
    S = n * tq
    G = NSA_KV_GROUPS
    n_sel = S // NSA_SEL_BLOCK
    gcols = NSA_HPG * HEAD_DIM
    kspec = pl.BlockSpec((1, n, tq, LANES), lambda b, g, i: (b, 0, 0, 0))
    vtspec = pl.BlockSpec((1, n, HEAD_DIM, tq), lambda b, g, i: (b, 0, g, 0))
    ospec = pl.BlockSpec((1, tq, gcols), lambda b, g, i: (b, i, g))
    return pl.pallas_call(
        functools.partial(_nsa_sel_win_kernel, tq=tq, n_sel=n_sel),
        grid=(B, G, n),
        in_specs=[pl.BlockSpec((1, 1, gcols, tq), lambda b, g, i: (b, i, g, 0)), kspec, vtspec, kspec, vtspec,
                  pl.BlockSpec((1, 1, n_sel, tq), lambda b, g, i: (b, g, 0, i))],
        out_specs=[ospec, ospec],
        out_shape=[jax.ShapeDtypeStruct((B, S, G * gcols), F32)] * 2,
        scratch_shapes=[pltpu.VMEM((tq, NSA_HPG * tq), F32), pltpu.VMEM((tq, NSA_HPG * tq), BF16)],
        compiler_params=_params(("arbitrary", "arbitrary", "arbitrary")),
        name="nsa_sel_win_attention",
    )(qn_t, ks.reshape(B, n, tq, LANES), vs_t, kw.reshape(B, n, tq, LANES), vw_t, sel)


def _attn_outproj_kernel(x_ref, d_ref, oc_ref, os_ref, ow_ref, gt_ref, ge_ref, w_ref, o_ref):
    ghi, glo = _split2(gt_ref[...])

    def gate(r):
        return _dot(ghi, ge_ref[r]) + _dot(glo, ge_ref[r])

    nsa = gate(0) * oc_ref[...] + gate(1) * os_ref[...] + gate(2) * ow_ref[...]
    y = _dot(_bf(d_ref[...]), w_ref[0:512, :]) + _dot(_bf(nsa), w_ref[512:1024, :])
    o_ref[...] = x_ref[...] + y


def _attn_outproj(x2d, diff_o, o_cmp, o_sel, o_win, gates, w_out):
    T, D = x2d.shape
    tm = min(256, T)
    col = jnp.arange(512) // HEAD_DIM
    src = jnp.arange(LANES)
    ge = jnp.stack([_bf((src[:, None] == col[None, :] * 3 + r).astype(F32)) for r in range(3)])
    row = lambda n: pl.BlockSpec((tm, n), lambda i: (i, 0))
    return pl.pallas_call(
        _attn_outproj_kernel,
        grid=(T // tm,),
        in_specs=[row(D), row(512), row(512), row(512), row(512), row(LANES),
                  pl.BlockSpec((3, LANES, 512), lambda i: (0, 0, 0)),
                  pl.BlockSpec((D, D), lambda i: (0, 0))],
        out_specs=row(D),
        out_shape=jax.ShapeDtypeStruct((T, D), F32),
        compiler_params=_params(("arbitrary",)),
        name="attn_outproj",
    )(x2d, diff_o, o_cmp, o_sel, o_win, gates, ge, _bf(w_out))


def _hybrid_attention_layer(x, layer, norm_gain, w_in, gate_bias, dq_gain, dk_gain, lam_p, subln, nq_gain, nk_gain,
                            cmp_pe, cmp_w1, cmp_w2, w_out):
    B, S, D = x.shape
    T = B * S
    tq = min(256, S)
    qd_t, kd, vd_t, qn_t, cmp_kv, ks, vs_t, kw, vw_t, gates = _attn_inproj(
        x, norm_gain, w_in, gate_bias, dq_gain, dk_gain, nq_gain, nk_gain, tq)
    lambda_init = 0.8 - 0.6 * math.exp(-0.3 * layer)
    diff_o = _diff_attention(qd_t, kd, vd_t, lam_p, subln, lambda_init, tq)
    kc, vc_t = _nsa_compress(cmp_kv, cmp_pe, cmp_w1, cmp_w2, nk_gain[0])
    o_cmp, sel = _nsa_cmp_attention(qn_t, kc, vc_t, tq)
    o_sel, o_win = _nsa_sel_win(qn_t, ks, vs_t, kw, vw_t, sel, tq)
    flat = lambda a: a.reshape(T, a.shape[-1])
    out = _attn_outproj(flat(x), flat(diff_o), flat(o_cmp), flat(o_sel), flat(o_win), flat(gates), w_out)
    return out.reshape(B, S, D)


def _softplus(z):
    return jnp.maximum(z, 0.0) + jnp.log(1.0 + jnp.exp(-jnp.abs(z)))


def _head_pool(d, head):
    member = (jnp.arange(d)[:, None] // head == jnp.arange(LANES)[None, :]).astype(BF16)
    return member, member.T


def _head_sum(parts, pool, expand):
    sums = _dot(parts[0], pool)
    for part in parts[1:]:
        sums = sums + _dot(part, pool)
    hi, lo = _split2(sums)
    return _dot(hi, expand) + _dot(lo, expand)


def _rwkv_proj_kernel(x_ref, xp_ref, g_ref, mix_ref, wr_ref, wk_ref, wv_ref, w1_ref, w2_ref, a1_ref, a2_ref,
                      g1_ref, g2_ref, vec_ref, pool_ref, expand_ref,
                      r_ref, wl_ref, k_ref, v_ref, kk_ref, b_ref, bonus_ref, gate_ref):
    i = pl.program_id(1)
    gain = g_ref[...]
    xn = _rms(x_ref[0], gain)
    prev = _rms(xp_ref[0], gain)[7:8] * jnp.where(i > 0, 1.0, 0.0)
    shifted = pltpu.roll(xn, 1, 0)
    first_row = lax.broadcasted_iota(jnp.int32, xn.shape, 0) == 0
    dx = jnp.where(first_row, prev, shifted) - xn
    mix = mix_ref[...]
    xr, xw, xk, xv, xa, xg = (_bf(xn + dx * mix[j:j + 1]) for j in range(6))
    vec = vec_ref[...]
    w0, a0, k_k, k_a, r_k = (vec[j:j + 1] for j in range(5))
    r = _dot(xr, wr_ref[...])
    k = _dot(xk, wk_ref[...])
    v = _dot(xv, wv_ref[...])
    w = -_softplus(-(w0 + _dot(_bf(jnp.tanh(_dot(xw, w1_ref[...]))), w2_ref[...]))) - 0.5
    a = jax.nn.sigmoid(a0 + _dot(_bf(_dot(xa, a1_ref[...])), a2_ref[...]))
    gate_ref[0] = _dot(_bf(jax.nn.sigmoid(_dot(xg, g1_ref[...]))), g2_ref[...])
    pool = pool_ref[...]
    expand = expand_ref[...]
    kk = k * k_k
    kk = kk / jnp.maximum(jnp.sqrt(_head_sum([_bf(kk * kk)], pool, expand)), 1e-12)
    k = k * (1.0 + (a - 1.0) * k_a)
    r_ref[0] = r
    wl_ref[0] = -jnp.exp(w)
    k_ref[0] = k
    v_ref[0] = v
    kk_ref[0] = kk
    b_ref[0] = kk * a
    bonus_ref[0] = _head_sum([_bf(r * k * r_k)], pool, expand) * v


def _rwkv_proj(x, gain, mix, w_r, w_k, w_v, w0, w1, w2, a0, a1, a2, g1, g2, k_k, k_a, r_k):
    B, S, D = x.shape
    tm = min(256, S)
    pad_c = lambda m, n: _bf(jnp.pad(m, ((0, 0), (0, n - m.shape[1]))))
    pad_r = lambda m, n: _bf(jnp.pad(m, ((0, n - m.shape[0]), (0, 0))))
    lw = LANES
    lg = 2 * LANES
    consts = (gain.reshape(1, D), jnp.pad(mix, ((0, 2), (0, 0))), _bf(w_r), _bf(w_k), _bf(w_v),
              pad_c(w1, lw), pad_r(w2, lw), pad_c(a1, lw), pad_r(a2, lw), pad_c(g1, lg), pad_r(g2, lg),
              jnp.pad(jnp.stack([w0, a0, k_k, k_a, r_k]), ((0, 3), (0, 0)))) + _head_pool(D, RWKV_HEAD)
    full = lambda a: pl.BlockSpec(a.shape, lambda b, i: (0,) * a.ndim)
    tile = pl.BlockSpec((1, tm, D), lambda b, i: (b, i, 0))
    return pl.pallas_call(
        _rwkv_proj_kernel,
        grid=(B, S // tm),
        in_specs=[tile, pl.BlockSpec((1, 8, D), lambda b, i: (b, jnp.maximum(i * (tm // 8) - 1, 0), 0))]
        + [full(a) for a in consts],
        out_specs=[tile] * 8,
        out_shape=[jax.ShapeDtypeStruct((B, S, D), F32)] * 8,
        compiler_params=_params(("arbitrary", "arbitrary")),
        name="rwkv_proj",
    )(x, x, *consts)


def _rwkv_chunk_kernel(r_ref, wl_ref, k_ref, v_ref, kk_ref, b_ref, m_ref, g0_ref, rq_ref, y0_ref, *, cpb):
    C = RWKV_CHUNK
    lane = lax.broadcasted_iota(jnp.int32, (C, LANES), 1)
    head0 = lane < RWKV_HEAD
    ti = lax.broadcasted_iota(jnp.int32, (2 * C, 2 * C), 0) & (C - 1)
    tj = lax.broadcasted_iota(jnp.int32, (2 * C, 2 * C), 1) & (C - 1)
    strict = ti > tj
    incl = ti >= tj
    eye = lax.broadcasted_iota(jnp.int32, (LANES, LANES), 0) == lax.broadcasted_iota(jnp.int32, (LANES, LANES), 1)
    ltri = _bf(jnp.where(lax.broadcasted_iota(jnp.int32, (C, C), 0) >= lax.broadcasted_iota(jnp.int32, (C, C), 1),
                         1.0, 0.0))

    def stack(x):
        return jnp.concatenate([jnp.where(head0, x, 0.0), jnp.where(head0, 0.0, x)], axis=0)

    chunks = range(cpb)
    rows = [slice(c * C, (c + 1) * C) for c in chunks]
    wl = [wl_ref[0, rows[c], :] for c in chunks]
    cum = []
    for c in chunks:
        h1 = _bf(wl[c])
        r1 = wl[c] - h1.astype(F32)
        h2 = _bf(r1)
        h3 = _bf(r1 - h2.astype(F32))
        cum.append(_dot(ltri, h1) + _dot(ltri, h2) + _dot(ltri, h3))
    total = [cum[c][C - 1:C, :] for c in chunks]
    lhs_kk, lhs_r, vs, a = [], [], [], []
    for c in chunks:
        p_inv = jnp.exp(-cum[c])
        lhs_kk.append(stack(kk_ref[0, rows[c], :] * jnp.exp(cum[c] - wl[c])))
        lhs_r.append(stack(r_ref[0, rows[c], :] * jnp.exp(cum[c])))
        vs.append(_bf(stack(v_ref[0, rows[c], :])))
        a.append(_dot_nt(_bf(jnp.concatenate([lhs_kk[c], lhs_r[c]], axis=0)),
                         _bf(jnp.concatenate([stack(b_ref[0, rows[c], :] * p_inv),
                                              stack(k_ref[0, rows[c], :] * p_inv)], axis=0))))
    npow = [_bf(jnp.where(strict, a[c][0:2 * C, 0:2 * C], 0.0)) for c in chunks]
    a_k = [_bf(jnp.where(strict, a[c][0:2 * C, 2 * C:4 * C], 0.0)) for c in chunks]
    a_rb = [_bf(jnp.where(incl, a[c][2 * C:4 * C, 0:2 * C], 0.0)) for c in chunks]
    a_rk = [_bf(jnp.where(incl, a[c][2 * C:4 * C, 2 * C:4 * C], 0.0)) for c in chunks]
    x = [jnp.concatenate([_dot(a_k[c], vs[c]), lhs_kk[c]], axis=1) for c in chunks]
    x = [x[c] - _dot(npow[c], _bf(x[c])) for c in chunks]
    for _ in range(5):
        npow = [_bf(_dot(npow[c], npow[c])) for c in chunks]
        x = [x[c] + _dot(npow[c], _bf(x[c])) for c in chunks]
    u0 = [_bf(-x[c][:, 0:LANES]) for c in chunks]
    wmat = [_bf(-x[c][:, LANES:2 * LANES]) for c in chunks]
    for c in chunks:
        y0_ref[0, 0, c] = _bf(_dot(a_rk[c], vs[c]) + _dot(a_rb[c], u0[c]))
        rq_ref[0, 0, c] = _bf(lhs_r[c] + _dot(a_rb[c], wmat[c]))
    for c in chunks:
        p_end = jnp.exp(total[c] - cum[c])
        bc = _bf(stack(b_ref[0, rows[c], :] * p_end))
        kc = _bf(stack(k_ref[0, rows[c], :] * p_end))
        g0_ref[0, 0, c] = _bf(_dot_tn(bc, u0[c]) + _dot_tn(kc, vs[c]))
        m_ref[0, 0, c] = jnp.where(eye, jnp.exp(total[c]), 0.0) + _dot_tn(bc, wmat[c])


def _rwkv_chunks(r, wl, k, v, kk, b):
    B, S, D = r.shape
    C = RWKV_CHUNK
    nc = S // C
    cpb = min(8, nc)
    hp = D // LANES
    tile = pl.BlockSpec((1, cpb * C, LANES), lambda bi, h, c: (bi, c, h))
    out = pl.BlockSpec((1, 1, cpb, LANES, LANES), lambda bi, h, c: (bi, h, c, 0, 0))
    return pl.pallas_call(
        functools.partial(_rwkv_chunk_kernel, cpb=cpb),
        grid=(B, hp, nc // cpb),
        in_specs=[tile] * 6,
        out_specs=[out] * 4,
        out_shape=[jax.ShapeDtypeStruct((B, hp, nc, LANES, LANES), F32)]
        + [jax.ShapeDtypeStruct((B, hp, nc, LANES, LANES), BF16)] * 3,
        compiler_params=_params(("arbitrary", "arbitrary", "arbitrary")),
        name="rwkv_chunk_summaries",
    )(r, wl, k, v, kk, b)


def _rwkv_scan_kernel(m_ref, g0_ref, rq_ref, y0_ref, y_ref, st_ref, *, cpb, hp):
    C = RWKV_CHUNK

    @pl.when(pl.program_id(1) == 0)
    def _():
        st_ref[...] = jnp.zeros_like(st_ref)

    for c in range(cpb):
        for h in range(hp):
            st = st_ref[h]
            s_hi, s_lo = _split2(st)
            y = y0_ref[0, h, c].astype(F32) + _dot(rq_ref[0, h, c], s_hi)
            y_ref[0, c * C:(c + 1) * C, h * LANES:(h + 1) * LANES] = y[0:C] + y[C:2 * C]
            m_hi, m_lo = _split2(m_ref[0, h, c])
            st_ref[h] = g0_ref[0, h, c] + _dot(m_hi, s_hi) + _dot(m_hi, s_lo) + _dot(m_lo, s_hi)


def _rwkv_scan(m, g0, rq, y0, S):
    B, hp, nc = m.shape[:3]
    C = RWKV_CHUNK
    cpb = min(4, nc)
    blk = pl.BlockSpec((1, hp, cpb, LANES, LANES), lambda bi, c: (bi, 0, c, 0, 0))
    return pl.pallas_call(
        functools.partial(_rwkv_scan_kernel, cpb=cpb, hp=hp),
        grid=(B, nc // cpb),
        in_specs=[blk] * 4,
        out_specs=pl.BlockSpec((1, cpb * C, hp * LANES), lambda bi, c: (bi, c, 0)),
        out_shape=jax.ShapeDtypeStruct((B, S, hp * LANES), F32),
        scratch_shapes=[pltpu.VMEM((hp, LANES, LANES), F32)],
        compiler_params=_params(("arbitrary", "arbitrary")),
        name="rwkv_state_scan",
    )(m, g0, rq, y0)


def _rwkv_out_kernel(x_ref, y_ref, bonus_ref, gate_ref, lnw_ref, lnb_ref, pool_ref, expand_ref, wo_ref, o_ref):
    y = y_ref[...]
    pool = pool_ref[...]
    expand = expand_ref[...]
    mu = _head_sum(_split2(y), pool, expand) * (1.0 / RWKV_HEAD)
    dev = y - mu
    var = _head_sum(_split2(dev * dev), pool, expand) * (1.0 / RWKV_HEAD)
    yn = dev * lax.rsqrt(var + RWKV_LNX_EPS) * lnw_ref[...] + lnb_ref[...]
    z = (yn + bonus_ref[...]) * gate_ref[...]
    o_ref[...] = x_ref[...] + _dot(_bf(z), wo_ref[...])


def _rwkv_out(x2d, y, bonus, gate, ln_w, ln_b, w_o):
    T, D = x2d.shape
    tm = min(256, T)
    row = pl.BlockSpec((tm, D), lambda i: (i, 0))
    vecs = pl.BlockSpec((1, D), lambda i: (0, 0))
    full = lambda shp: pl.BlockSpec(shp, lambda i: (0, 0))
    pool, expand = _head_pool(D, RWKV_HEAD)
    return pl.pallas_call(
        _rwkv_out_kernel,
        grid=(T // tm,),
        in_specs=[row, row, row, row, vecs, vecs, full((D, LANES)), full((LANES, D)), full((D, D))],
        out_specs=row,
        out_shape=jax.ShapeDtypeStruct((T, D), F32),
        compiler_params=_params(("arbitrary",)),
        name="rwkv_out",
    )(x2d, y, bonus, gate, ln_w.reshape(1, D), ln_b.reshape(1, D), pool, expand, _bf(w_o))


def _rwkv_layer(x, norm_gain, mix, w_r, w_k, w_v, w0, w1, w2, a0, a1, a2, g1, g2, k_k, k_a, r_k, ln_w, ln_b, w_o):
    B, S, D = x.shape
    r, wl, k, v, kk, b, bonus, gate = _rwkv_proj(x, norm_gain, mix, w_r, w_k, w_v, w0, w1, w2, a0, a1, a2,
                                                 g1, g2, k_k, k_a, r_k)
    m, g0, rq, y0 = _rwkv_chunks(r, wl, k, v, kk, b)
    y = _rwkv_scan(m, g0, rq, y0, S)
    T = B * S
    out = _rwkv_out(x.reshape(T, D), y.reshape(T, D), bonus.reshape(T, D), gate.reshape(T, D), ln_w, ln_b, w_o)
    return out.reshape(B, S, D)


def _router_kernel(x_ref, g_ref, whi_ref, wlo_ref, b_ref, xn_ref, route_ref):
    xn = _rms(x_ref[...], g_ref[...])
    xn_ref[...] = xn
    hi, lo = _split2(xn)
    logits = _dot(hi, whi_ref[...]) + _dot(hi, wlo_ref[...]) + _dot(lo, whi_ref[...]) + b_ref[...]
    lane = lax.broadcasted_iota(jnp.int32, logits.shape, 1)
    lane_f = lane.astype(F32)

    def top(vals):
        mx = jnp.max(vals, axis=-1, keepdims=True)
        return mx, jnp.min(jnp.where(vals == mx, lane_f, float(LANES)), axis=-1, keepdims=True)

    glog = jnp.where(lane < N_GROUPS, logits, NEG_INF)
    gmax, gidx = top(glog)
    gsum = jnp.sum(jnp.where(lane < N_GROUPS, jnp.exp(glog - gmax), 0.0), axis=-1, keepdims=True)
    grp_p = 1.0 / gsum
    first = float(N_GROUPS) + EXPERTS_PER_GROUP * gidx
    elog = jnp.where(lane_f >= first, jnp.where(lane_f < first + EXPERTS_PER_GROUP, logits, NEG_INF), NEG_INF)
    v1, i1 = top(elog)
    v2, i2 = top(jnp.where(lane_f == i1, NEG_INF, elog))
    e2 = jnp.exp(v2 - v1)
    gate1 = grp_p / (1.0 + e2)
    gate2 = grp_p * e2 / (1.0 + e2)
    route_ref[...] = jnp.where(lane == 0, i1 - N_GROUPS, jnp.where(lane == 1, i2 - N_GROUPS,
                               jnp.where(lane == 2, gate1, jnp.where(lane == 3, gate2, 0.0))))


def _router(x2d, gain, wg, bg, we, be):
    T, D = x2d.shape
    tm = min(256, T)
    w = jnp.pad(jnp.concatenate([wg, we], axis=1), ((0, 0), (0, LANES - N_GROUPS - N_EXPERTS)))
    w_hi = _bf(w)
    w_lo = _bf(w - w_hi.astype(F32))
    bias = jnp.pad(jnp.concatenate([bg, be]), (0, LANES - N_GROUPS - N_EXPERTS)).reshape(1, LANES)
    row = lambda n: pl.BlockSpec((tm, n), lambda i: (i, 0))
    full = lambda a: pl.BlockSpec(a.shape, lambda i: (0, 0))
    consts = (gain.reshape(1, D), w_hi, w_lo, bias)
    return pl.pallas_call(
        _router_kernel,
        grid=(T // tm,),
        in_specs=[row(D)] + [full(a) for a in consts],
        out_specs=[row(D), row(LANES)],
        out_shape=[jax.ShapeDtypeStruct((T, D), F32), jax.ShapeDtypeStruct((T, LANES), F32)],
        compiler_params=_params(("arbitrary",)),
        name="moe_router",
    )(x2d, *consts)


def _slot_tokens_kernel(dest_ref, slot_ref):
    def clear(s, carry):
        slot_ref[s] = 0
        return carry
    lax.fori_loop(0, slot_ref.shape[0], clear, 0, unroll=8)

    def place(p, carry):
        slot_ref[dest_ref[p]] = lax.shift_right_logical(p, 1)
        return carry
    lax.fori_loop(0, dest_ref.shape[0], place, 0, unroll=8)


def _slot_tokens(dest, n_slots):
    return pl.pallas_call(
        _slot_tokens_kernel,
        in_specs=[pl.BlockSpec(memory_space=pltpu.SMEM)],
        out_specs=pl.BlockSpec(memory_space=pltpu.SMEM),
        out_shape=jax.ShapeDtypeStruct((n_slots,), jnp.int32),
        name="moe_slot_tokens",
    )(dest)


def _row_gather_start(src_hbm, idx_ref, idx_base, idx_stride, dst, sem, rows):
    for r in range(rows):
        src_row = idx_ref[idx_base + r * idx_stride]
        pltpu.make_async_copy(src_hbm.at[pl.ds(src_row, 1)], dst.at[pl.ds(r, 1)], sem).start(priority=r % 2)


def _row_gather_wait(src_hbm, dst, sem, rows):
    pltpu.make_async_copy(src_hbm.at[pl.ds(0, rows)], dst, sem).wait()


def _expert_kernel(be_ref, nu_ref, tok_ref, xn_hbm, wg_ref, wu_ref, wd_ref, o_ref, xbuf, sem):
    i = pl.program_id(0)
    nu = nu_ref[0]
    cur = i & 1
    last = pl.num_programs(0) - 1

    @pl.when(i == 0)
    def _():
        _row_gather_start(xn_hbm, tok_ref, 0, 1, xbuf.at[0], sem.at[0], MOE_ROWS)

    @pl.when(i < nu)
    def _():
        nxt = jnp.minimum(i + 1, last)
        _row_gather_start(xn_hbm, tok_ref, nxt * MOE_ROWS, 1, xbuf.at[1 - cur], sem.at[1 - cur], MOE_ROWS)
        _row_gather_wait(xn_hbm, xbuf.at[cur], sem.at[cur], MOE_ROWS)
        x = _bf(xbuf[cur])
        hid = jax.nn.silu(_dot(x, _bf(wg_ref[0]))) * _dot(x, _bf(wu_ref[0]))
        o_ref[...] = _dot(_bf(hid), _bf(wd_ref[0]))

        @pl.when(i == nu - 1)
        def _():
            _row_gather_wait(xn_hbm, xbuf.at[1 - cur], sem.at[1 - cur], MOE_ROWS)

    @pl.when(i >= nu)
    def _():
        o_ref[...] = jnp.zeros_like(o_ref)


def _experts(xn, slot_tok, block_e, n_used, e_gate, e_up, e_down):
    T, D = xn.shape
    n_blocks = slot_tok.shape[0] // MOE_ROWS
    wspec = lambda shp: pl.BlockSpec((1,) + shp, lambda i, be, nu, tok: (be[i], 0, 0))
    return pl.pallas_call(
        _expert_kernel,
        grid_spec=pltpu.PrefetchScalarGridSpec(
            num_scalar_prefetch=3,
            grid=(n_blocks,),
            in_specs=[pl.BlockSpec(memory_space=pl.ANY), wspec((D, D_EXPERT)), wspec((D, D_EXPERT)),
                      wspec((D_EXPERT, D))],
            out_specs=pl.BlockSpec((MOE_ROWS, D), lambda i, be, nu, tok: (i, 0)),
            scratch_shapes=[pltpu.VMEM((2, MOE_ROWS, D), F32), pltpu.SemaphoreType.DMA((2,))],
        ),
        out_shape=jax.ShapeDtypeStruct((n_blocks * MOE_ROWS, D), F32),
        compiler_params=_params(("arbitrary",)),
        name="moe_experts",
    )(block_e, n_used, slot_tok, xn, e_gate, e_up, e_down)


def _combine_kernel(dest_ref, x_ref, route_ref, ys_hbm, o_ref, buf, sem, *, tm):
    i = pl.program_id(0)
    cur = i & 1

    def start(tile, par):
        for k in range(2):
            _row_gather_start(ys_hbm, dest_ref, tile * (2 * tm) + k, 2, buf.at[par, k], sem.at[par], tm)

    @pl.when(i == 0)
    def _():
        start(0, 0)

    @pl.when(i + 1 < pl.num_programs(0))
    def _():
        start(i + 1, 1 - cur)

    for k in range(2):
        _row_gather_wait(ys_hbm, buf.at[cur, k], sem.at[cur], tm)
    route = route_ref[...]
    o_ref[...] = x_ref[...] + route[:, 2:3] * buf[cur, 0] + route[:, 3:4] * buf[cur, 1]


def _combine(x2d, y_slots, dest, route):
    T, D = x2d.shape
    tm = min(256, T)
    row = lambda n: pl.BlockSpec((tm, n), lambda i, dest: (i, 0))
    return pl.pallas_call(
        functools.partial(_combine_kernel, tm=tm),
        grid_spec=pltpu.PrefetchScalarGridSpec(
            num_scalar_prefetch=1,
            grid=(T // tm,),
            in_specs=[row(D), row(LANES), pl.BlockSpec(memory_space=pl.ANY)],
            out_specs=row(D),
            scratch_shapes=[pltpu.VMEM((2, 2, tm, D), F32), pltpu.SemaphoreType.DMA((2,))],
        ),
        out_shape=jax.ShapeDtypeStruct((T, D), F32),
        compiler_params=_params(("arbitrary",)),
        name="moe_combine",
    )(dest, x2d, route, y_slots)


def _moe_layer(x, gain, wg, bg, we, be, e_gate, e_up, e_down):
    B, S, D = x.shape
    T = B * S
    x2d = x.reshape(T, D)
    xn, route = _router(x2d, gain, wg, bg, we, be)
    flat_e = route[:, 0:2].astype(jnp.int32).reshape(-1)
    n_pairs = 2 * T
    onehot = (flat_e[:, None] == jnp.arange(N_EXPERTS, dtype=jnp.int32)[None, :]).astype(jnp.int32)
    csum = jnp.cumsum(onehot, axis=0)
    rank = jnp.take_along_axis(csum, flat_e[:, None], axis=1)[:, 0] - 1
    counts = csum[-1]
    padded = (counts + MOE_ROWS - 1) // MOE_ROWS * MOE_ROWS
    pad_end = jnp.cumsum(padded)
    dest = (pad_end - padded)[flat_e] + rank
    n_blocks = -(-n_pairs // MOE_ROWS) + N_EXPERTS
    dest = dest.astype(jnp.int32)
    slot_tok = _slot_tokens(dest, n_blocks * MOE_ROWS)
    block_start = jnp.arange(n_blocks, dtype=jnp.int32) * MOE_ROWS
    block_e = jnp.minimum(jnp.sum((pad_end[None, :] <= block_start[:, None]).astype(jnp.int32), axis=1),
                          N_EXPERTS - 1)
    n_used = (pad_end[-1:] // MOE_ROWS).astype(jnp.int32)
    y_slots = _experts(xn, slot_tok, block_e, n_used, e_gate, e_up, e_down)
    return _combine(x2d, y_slots, dest, route).reshape(B, S, D)


def kernel(x, mix_norm, attn_w_in, attn_gate_bias, diff_q_norm, diff_k_norm, diff_lambda, diff_subln, nsa_q_norm, nsa_k_norm, nsa_cmp_pe, nsa_cmp_w1, nsa_cmp_w2, attn_w_out, rwkv_mix, rwkv_w_r, rwkv_w_k, rwkv_w_v, rwkv_decay_w0, rwkv_decay_w1, rwkv_decay_w2, rwkv_iclr_a0, rwkv_iclr_a1, rwkv_iclr_a2, rwkv_gate_g1, rwkv_gate_g2, rwkv_k_k, rwkv_k_a, rwkv_r_k, rwkv_ln_w, rwkv_ln_b, rwkv_w_o, ffn_norm, router_group_w, router_group_b, router_expert_w, router_expert_b, expert_w_gate, expert_w_up, expert_w_down):
    depth = mix_norm.shape[0]
    for layer in range(depth):
        i = layer // 2
        if layer % 2 == 0:
            x = _hybrid_attention_layer(
                x, layer, mix_norm[layer], attn_w_in[i], attn_gate_bias[i], diff_q_norm[i], diff_k_norm[i],
                diff_lambda[i], diff_subln[i], nsa_q_norm[i], nsa_k_norm[i], nsa_cmp_pe[i], nsa_cmp_w1[i],
                nsa_cmp_w2[i], attn_w_out[i])
        else:
            x = _rwkv_layer(
                x, mix_norm[layer], rwkv_mix[i], rwkv_w_r[i], rwkv_w_k[i], rwkv_w_v[i], rwkv_decay_w0[i],
                rwkv_decay_w1[i], rwkv_decay_w2[i], rwkv_iclr_a0[i], rwkv_iclr_a1[i], rwkv_iclr_a2[i],
                rwkv_gate_g1[i], rwkv_gate_g2[i], rwkv_k_k[i], rwkv_k_a[i], rwkv_r_k[i], rwkv_ln_w[i],
                rwkv_ln_b[i], rwkv_w_o[i])
        x = _moe_layer(x, ffn_norm[layer], router_group_w[layer], router_group_b[layer], router_expert_w[layer],
                       router_expert_b[layer], expert_w_gate[layer], expert_w_up[layer], expert_w_down[layer])
    return x
```

```python
import functools
import math

import jax
import jax.numpy as jnp
from jax import lax
from jax.experimental import pallas as pl
from jax.experimental.pallas import tpu as pltpu

F32 = jnp.float32
BF16 = jnp.bfloat16

D_MODEL = 1024
HEAD_DIM = 64
NORM_EPS = 1e-6
NEG_INF = -1e30
FORCE_SCORE = 1e4

DIFF_HEADS = 8
DIFF_QK_DIM = 32
NSA_HEADS = 8
NSA_KV_GROUPS = 2
NSA_HPG = 4
NSA_CMP_BLOCK = 32
NSA_CMP_STRIDE = 16
NSA_SEL_BLOCK = 64
NSA_TOP_N = 16
NSA_WINDOW = 512
IN_COLS = 2840
IN_COLS_PAD = 2944
GATE_COLS = 24

RWKV_HEAD = 64
RWKV_LNX_EPS = 64e-5
RWKV_CHUNK = 64

N_GROUPS = 4
EXPERTS_PER_GROUP = 8
N_EXPERTS = 32
D_EXPERT = 256
MOE_ROWS = 256
MOE_BUFFERS = 3

LANES = 128
VMEM_LIMIT = 56 * 1024 * 1024


def _bf(x):
    return x.astype(BF16)


def _dot(a, b):
    return jnp.dot(a, b, preferred_element_type=F32)


def _dot_nt(a, b):
    return lax.dot_general(a, b, (((1,), (1,)), ((), ())), preferred_element_type=F32)


def _dot_tn(a, b):
    return lax.dot_general(a, b, (((0,), (0,)), ((), ())), preferred_element_type=F32)


def _split2(x):
    hi = _bf(x)
    lo = _bf(x - hi.astype(F32))
    return hi, lo


def _params(sem):
    return pltpu.CompilerParams(dimension_semantics=sem, vmem_limit_bytes=VMEM_LIMIT)


def _block_diag(n, group):
    r = jnp.arange(n) // group
    return ((r[:, None] == r[None, :]).astype(F32) / group).astype(BF16)


def _rms(x, gain):
    return x * lax.rsqrt(jnp.mean(x * x, axis=-1, keepdims=True) + NORM_EPS) * gain


def _inproj_kernel(x_ref, g_ref, wa_ref, wbt_ref, b_ref, bd32_ref, bd64_ref, gk_ref, gnk_ref, gq_ref, gnq_ref,
                   qd_ref, kd_ref, vd_ref, qn_ref, cmp_ref, ks_ref, vs_ref, kw_ref, vw_ref, gt_ref):
    xn = _bf(_rms(x_ref[0], g_ref[...]))
    ha = _dot(xn, wa_ref[...])
    hb = _dot_nt(wbt_ref[...], xn)
    bd32 = bd32_ref[...]
    bd64 = bd64_ref[...]

    def gnorm(seg, bd, gain):
        hi, lo = _split2(seg * seg)
        return seg * lax.rsqrt(_dot(hi, bd) + _dot(lo, bd) + NORM_EPS) * gain

    def gnorm_t(seg, bd, gain):
        hi, lo = _split2(seg * seg)
        return seg * lax.rsqrt(_dot(bd, hi) + _dot(bd, lo) + NORM_EPS) * gain

    kd_ref[0] = _bf(gnorm(ha[:, 0:512], bd32, gk_ref[...]))
    cmp_ref[0, 0] = ha[:, 512:640]
    cmp_ref[0, 1] = ha[:, 640:768]
    ksw = gnorm(ha[:, 768:1024], bd64[0:256, 0:256], gnk_ref[...])
    ks_ref[0] = _bf(ksw[:, 0:128])
    kw_ref[0] = _bf(ksw[:, 128:256])
    gt_ref[0] = jax.nn.sigmoid(ha[:, 1024:1152] + b_ref[...])
    qd_ref[0, 0] = _bf(gnorm_t(hb[0:512], bd32, gq_ref[...]))
    vd_ref[0, 0] = _bf(hb[512:1024])
    qn_ref[0, 0] = _bf(gnorm_t(hb[1024:1536], bd64, gnq_ref[...]))
    vs_ref[0, 0] = _bf(hb[1536:1664])
    vw_ref[0, 0] = _bf(hb[1664:1792])


def _attn_inproj(x, gain, w_in, gate_bias, dq_gain, dk_gain, nq_gain, nk_gain, tm):
    B, S, D = x.shape
    n = S // tm
    c = lambda lo, hi: w_in[:, lo:hi]
    gate_w = jnp.pad(c(2816, IN_COLS), ((0, 0), (0, LANES - GATE_COLS)))
    wa = _bf(jnp.concatenate([c(512, 1024), c(2048, 2304), c(2304, 2432), c(2560, 2688), gate_w], axis=1))
    wbt = _bf(jnp.concatenate([c(0, 512), c(1024, 1536), c(1536, 2048), c(2432, 2560), c(2688, 2816)], axis=1).T)
    bias = jnp.pad(gate_bias, (0, LANES - GATE_COLS)).reshape(1, LANES)
    log2e = math.log2(math.e)
    gk = jnp.tile(dk_gain, 16).reshape(1, 512)
    gnk = jnp.concatenate([jnp.tile(nk_gain[1], 2), jnp.tile(nk_gain[2], 2)]).reshape(1, 256)
    gq = (jnp.tile(dq_gain, 16) * (DIFF_QK_DIM ** -0.5 * log2e)).reshape(512, 1)
    gnq = (jnp.tile(nq_gain, 8) * (HEAD_DIM ** -0.5 * log2e)).reshape(512, 1)
    consts = (gain.reshape(1, D), wa, wbt, bias, _block_diag(512, 32), _block_diag(512, 64), gk, gnk, gq, gnq)
    full = lambda a: pl.BlockSpec(a.shape, lambda b, i: (0,) * a.ndim)
    tok = lambda w: pl.BlockSpec((1, tm, w), lambda b, i: (b, i, 0))
    tr = lambda r: pl.BlockSpec((1, 1, r, tm), lambda b, i: (b, i, 0, 0))
    tok_shape = lambda w, dt: jax.ShapeDtypeStruct((B, S, w), dt)
    tr_shape = lambda r: jax.ShapeDtypeStruct((B, n, r, tm), BF16)
    return pl.pallas_call(
        _inproj_kernel,
        grid=(B, n),
        in_specs=[tok(D)] + [full(a) for a in consts],
        out_specs=[tr(512), tok(512), tr(512), tr(512), pl.BlockSpec((1, 2, tm, LANES), lambda b, i: (b, 0, i, 0)),
                   tok(LANES), tr(LANES), tok(LANES), tr(LANES), tok(LANES)],
        out_shape=[tr_shape(512), tok_shape(512, BF16), tr_shape(512), tr_shape(512),
                   jax.ShapeDtypeStruct((B, 2, S, LANES), F32),
                   tok_shape(LANES, BF16), tr_shape(LANES), tok_shape(LANES, BF16), tr_shape(LANES),
                   tok_shape(LANES, F32)],
        compiler_params=_params(("arbitrary", "arbitrary")),
        name="attn_inproj",
    )(x, *consts)


def _flash_tiles(lo, hi, scores, values, mask_body, mask_last, s_ref, p_ref):
    tk, cols = s_ref.shape
    dv = HEAD_DIM
    p_ref[...] = jnp.zeros_like(p_ref)
    s_ref[...] = scores(lo)

    def process(j, carry, mask, s_next):
        m, l, acc = carry
        s = s_ref[...]
        if mask is not None:
            s = mask(j, s)
        pv = values(jnp.maximum(j - 1, lo), p_ref[...])
        m_new = jnp.maximum(m, jnp.max(s, axis=0, keepdims=True))
        alpha = jnp.exp2(m - m_new)
        p = jnp.exp2(s - m_new)
        l = alpha * l + jnp.sum(p, axis=0, keepdims=True)
        acc = (acc + pv) * alpha
        p_ref[...] = _bf(p)
        if s_next is not None:
            s_ref[...] = s_next
        return m_new, l, acc

    init = (jnp.full((1, cols), NEG_INF, F32), jnp.zeros((1, cols), F32), jnp.zeros((dv, cols), F32))
    carry = lax.fori_loop(lo, hi, lambda j, c: process(j, c, mask_body, scores(j + 1)), init)
    m, l, acc = process(hi, carry, mask_last, None)
    return (acc + values(hi, p_ref[...])) / l


def _diff_attn_kernel(qt_ref, k_ref, vt_ref, lam_ref, sg_ref, o_ref, s_ref, p_ref, *, tq, lambda_init):
    i = pl.program_id(2)
    q = qt_ref[0, 0]
    zero = jnp.zeros((DIFF_QK_DIM, tq), BF16)
    qb = jnp.concatenate(
        [jnp.concatenate([q[r * DIFF_QK_DIM:(r + 1) * DIFF_QK_DIM] if c == r else zero for c in range(4)], axis=1)
         for r in range(4)], axis=0)

    def causal(j, s):
        kpos = lax.broadcasted_iota(jnp.int32, (tq, 4 * tq), 0)
        qpos = lax.broadcasted_iota(jnp.int32, (tq, 4 * tq), 1) & (tq - 1)
        return jnp.where(kpos <= qpos, s, NEG_INF)

    def values(j, p):
        vt = vt_ref[0, j]
        return jnp.concatenate([_dot(vt[0:HEAD_DIM], p[:, 0:2 * tq]),
                                _dot(vt[HEAD_DIM:2 * HEAD_DIM], p[:, 2 * tq:4 * tq])], axis=1)

    o = _flash_tiles(0, i, lambda j: _dot(k_ref[0, j], qb), values, None, causal, s_ref, p_ref)
    lp = lam_ref[...]
    lam = (jnp.exp(jnp.sum(lp[0:1] * lp[1:2], axis=-1, keepdims=True))
           - jnp.exp(jnp.sum(lp[2:3] * lp[3:4], axis=-1, keepdims=True)) + lambda_init)
    heads = []
    for h in range(2):
        oh = o[:, 2 * h * tq:(2 * h + 1) * tq] - lam * o[:, (2 * h + 1) * tq:(2 * h + 2) * tq]
        oh = oh * lax.rsqrt(jnp.mean(oh * oh, axis=0, keepdims=True) + NORM_EPS) * sg_ref[...]
        heads.append(oh * (1.0 - lambda_init))
    o_ref[0] = jnp.concatenate(heads, axis=0).T


def _diff_attention(qd_t, kd, vd_t, lam_p, subln, lambda_init, tq):
    B, n = qd_t.shape[:2]
    S = n * tq
    pairs = DIFF_HEADS // 2
    return pl.pallas_call(
        functools.partial(_diff_attn_kernel, tq=tq, lambda_init=lambda_init),
        grid=(B, pairs, n),
        in_specs=[
            pl.BlockSpec((1, 1, LANES, tq), lambda b, h, i: (b, i, h, 0)),
            pl.BlockSpec((1, n, tq, LANES), lambda b, h, i: (b, 0, 0, h)),
            pl.BlockSpec((1, n, LANES, tq), lambda b, h, i: (b, 0, h, 0)),
            pl.BlockSpec((4, DIFF_QK_DIM), lambda b, h, i: (0, 0)),
            pl.BlockSpec((HEAD_DIM, 1), lambda b, h, i: (0, 0)),
        ],
        out_specs=pl.BlockSpec((1, tq, LANES), lambda b, h, i: (b, i, h)),
        out_shape=jax.ShapeDtypeStruct((B, S, DIFF_HEADS * HEAD_DIM), F32),
        scratch_shapes=[pltpu.VMEM((tq, 4 * tq), F32), pltpu.VMEM((tq, 4 * tq), BF16)],
        compiler_params=_params(("arbitrary", "arbitrary", "arbitrary")),
        name="diff_attention",
    )(qd_t, kd.reshape(B, n, tq, DIFF_HEADS * HEAD_DIM), vd_t, lam_p, subln.reshape(HEAD_DIM, 1))


def _nsa_compress_kernel(t_ref, pe_ref, w1_ref, w2_ref, bd_ref, kg_ref, kc_ref, vct_ref, *, nch):
    half = NSA_CMP_BLOCK // 2
    ya = jnp.zeros((nch, 2 * LANES), F32)
    yb = jnp.zeros((nch, 2 * LANES), F32)
    for l in range(half):
        rows = pl.ds(l, nch, stride=NSA_CMP_STRIDE)
        piece = jnp.concatenate([t_ref[0, 0, rows, :], t_ref[0, 1, rows, :]], axis=1)
        ya = ya + _dot(_bf(piece + pe_ref[l:l + 1]), w1_ref[l])
        yb = yb + _dot(_bf(piece + pe_ref[half + l:half + l + 1]), w1_ref[half + l])
    pre = ya + pltpu.roll(yb, nch - 1, 0)
    out = _dot(_bf(jax.nn.gelu(pre)), w2_ref[...])
    k = out[:, 0:LANES]
    hi, lo = _split2(k * k)
    ms = _dot(hi, bd_ref[...]) + _dot(lo, bd_ref[...])
    kc_ref[0] = _bf(k * lax.rsqrt(ms + NORM_EPS) * kg_ref[...])
    vct_ref[0] = _bf(out[:, LANES:2 * LANES].T)


def _nsa_compress(cmp_kv, cmp_pe, cmp_w1, cmp_w2, k_gain0):
    B, _, S, _ = cmp_kv.shape
    nch = S // NSA_CMP_STRIDE

    def over_groups(w):
        parts = [w[0], w[0], w[1], w[1]]
        zero = jnp.zeros_like(w[0])
        return jnp.concatenate(
            [jnp.concatenate([parts[r] if c == r else zero for c in range(4)], axis=-1) for r in range(4)], axis=-2)

    w1 = _bf(over_groups(cmp_w1.reshape(2, NSA_CMP_BLOCK, HEAD_DIM, HEAD_DIM)))
    w2 = _bf(over_groups(cmp_w2))
    pe = jnp.concatenate([cmp_pe[0], cmp_pe[0], cmp_pe[1], cmp_pe[1]], axis=-1)
    full = lambda a: pl.BlockSpec(a.shape, lambda b: (0,) * a.ndim)
    consts = (pe, w1, w2, _block_diag(LANES, HEAD_DIM), jnp.tile(k_gain0, 2).reshape(1, LANES))
    return pl.pallas_call(
        functools.partial(_nsa_compress_kernel, nch=nch),
        grid=(B,),
        in_specs=[pl.BlockSpec((1, 2, S, LANES), lambda b: (b, 0, 0, 0))] + [full(a) for a in consts],
        out_specs=[pl.BlockSpec((1, nch, LANES), lambda b: (b, 0, 0)), pl.BlockSpec((1, LANES, nch), lambda b: (b, 0, 0))],
        out_shape=[jax.ShapeDtypeStruct((B, nch, LANES), BF16), jax.ShapeDtypeStruct((B, LANES, nch), BF16)],
        compiler_params=_params(("arbitrary",)),
        name="nsa_compress",
    )(cmp_kv, *consts)


def _group_queries(q_ref, g):
    q = q_ref[0, 0]
    qg = jnp.concatenate([q[h * HEAD_DIM:(h + 1) * HEAD_DIM] for h in range(NSA_HPG)], axis=1)
    zero = jnp.zeros_like(qg)
    return jnp.where(g == 0, jnp.concatenate([qg, zero], axis=0), jnp.concatenate([zero, qg], axis=0))


def _store_heads(o_ref, o, tq):
    o_ref[0] = jnp.concatenate([o[:, h * tq:(h + 1) * tq] for h in range(NSA_HPG)], axis=0).T


def _nsa_cmp_attn_kernel(q_ref, k_ref, vt_ref, o_ref, sel_ref, *, tq, nch, n_sel, top_n):
    i = pl.program_id(2)
    cols = NSA_HPG * tq
    n_cmp = nch - 1
    s = _dot(k_ref[0], _group_queries(q_ref, pl.program_id(1)))
    pos = i * tq + (lax.broadcasted_iota(jnp.int32, (nch, cols), 1) & (tq - 1))
    c = lax.broadcasted_iota(jnp.int32, (nch, cols), 0)
    ok = jnp.where(c < n_cmp, c * NSA_CMP_STRIDE + (NSA_CMP_BLOCK - 1), 1 << 30) <= pos
    s = jnp.where(ok, s, NEG_INF)
    p = jnp.where(ok, jnp.exp2(s - jnp.max(s, axis=0, keepdims=True)), 0.0)
    l = jnp.sum(p, axis=0, keepdims=True)
    p = p / jnp.where(l > 0.0, l, 1.0)
    _store_heads(o_ref, _dot(vt_ref[0], _bf(p)), tq)

    pg = p[:, 0:tq]
    for h in range(1, NSA_HPG):
        pg = pg + p[:, h * tq:(h + 1) * tq]
    jj = lax.broadcasted_iota(jnp.int32, (n_sel, nch), 0) * NSA_SEL_BLOCK
    cc = lax.broadcasted_iota(jnp.int32, (n_sel, nch), 1) * NSA_CMP_STRIDE
    cover = jnp.where(cc < jj + NSA_SEL_BLOCK,
                      jnp.where(cc + NSA_CMP_BLOCK > jj, jnp.where(cc < n_cmp * NSA_CMP_STRIDE, 1.0, 0.0), 0.0), 0.0)
    cover = _bf(cover)
    hi, lo = _split2(pg)
    imp = _dot(cover, hi) + _dot(cover, lo)
    blk = lax.broadcasted_iota(jnp.int32, (n_sel, tq), 0)
    cur = (i * tq + lax.broadcasted_iota(jnp.int32, (n_sel, tq), 1)) >> 6
    imp = jnp.where(blk == cur, FORCE_SCORE, jnp.where(blk == 0, FORCE_SCORE, jnp.where(blk > cur, NEG_INF, imp)))
    blk_f = blk.astype(F32)
    sel = jnp.zeros((n_sel, tq), F32)
    for _ in range(top_n):
        mx = jnp.max(imp, axis=0, keepdims=True)
        first = jnp.min(jnp.where(imp == mx, blk_f, float(n_sel)), axis=0, keepdims=True)
        hit = blk_f == first
        sel = jnp.where(hit, 1.0, sel)
        imp = jnp.where(hit, -jnp.inf, imp)
    sel_ref[0, 0] = _bf(sel)


def _nsa_cmp_attention(qn_t, kc, vc_t, tq):
    B, n = qn_t.shape[:2]
    S = n * tq
    G = NSA_KV_GROUPS
    nch = S // NSA_CMP_STRIDE
    n_sel = S // NSA_SEL_BLOCK
    top_n = min(NSA_TOP_N, n_sel)
    gcols = NSA_HPG * HEAD_DIM
    return pl.pallas_call(
        functools.partial(_nsa_cmp_attn_kernel, tq=tq, nch=nch, n_sel=n_sel, top_n=top_n),
        grid=(B, G, n),
        in_specs=[
            pl.BlockSpec((1, 1, gcols, tq), lambda b, g, i: (b, i, g, 0)),
            pl.BlockSpec((1, nch, LANES), lambda b, g, i: (b, 0, 0)),
            pl.BlockSpec((1, HEAD_DIM, nch), lambda b, g, i: (b, g, 0)),
        ],
        out_specs=[pl.BlockSpec((1, tq, gcols), lambda b, g, i: (b, i, g)),
                   pl.BlockSpec((1, 1, n_sel, tq), lambda b, g, i: (b, g, 0, i))],
        out_shape=[jax.ShapeDtypeStruct((B, S, G * gcols), F32),
                   jax.ShapeDtypeStruct((B, G, n_sel, S), BF16)],
        compiler_params=_params(("arbitrary", "arbitrary", "arbitrary")),
        name="nsa_cmp_attention",
    )(qn_t, kc, vc_t)


def _nsa_sel_win_kernel(q_ref, ks_ref, vst_ref, kw_ref, vwt_ref, sel_ref, os_ref, ow_ref, s_ref, p_ref, *, tq, n_sel):
    i = pl.program_id(2)
    qt = _group_queries(q_ref, pl.program_id(1))
    selm = sel_ref[0, 0]
    kloc = lax.broadcasted_iota(jnp.int32, (tq, tq), 0)
    qpos = i * tq + lax.broadcasted_iota(jnp.int32, (tq, tq), 1)
    blk_row = lax.broadcasted_iota(jnp.int32, (tq, n_sel), 0)
    blk_col = lax.broadcasted_iota(jnp.int32, (tq, n_sel), 1)

    def masked(s, keep):
        return jnp.where(jnp.concatenate([keep] * NSA_HPG, axis=1) > 0.5, s, NEG_INF)

    def sel_mask(j, s):
        expand = _bf(jnp.where(blk_col == ((j * tq + blk_row) >> 6), 1.0, 0.0))
        chosen = _dot(expand, selm)
        return masked(s, jnp.where(j * tq + kloc <= qpos, chosen, 0.0))

    o = _flash_tiles(0, i, lambda j: _dot(ks_ref[0, j], qt), lambda j, p: _dot(vst_ref[0, j], p),
                     sel_mask, sel_mask, s_ref, p_ref)
    _store_heads(os_ref, o, tq)

    def win_mask(j, s):
        dist = qpos - (j * tq + kloc)
        return masked(s, jnp.where(dist >= 0, jnp.where(dist < NSA_WINDOW, 1.0, 0.0), 0.0))

    first = jnp.maximum(i - (NSA_WINDOW - 1 + tq - 1) // tq, 0)
    o = _flash_tiles(first, i, lambda j: _dot(kw_ref[0, j], qt), lambda j, p: _dot(vwt_ref[0, j], p),
                     win_mask, win_mask, s_ref, p_ref)
    _store_heads(ow_ref, o, tq)


def _nsa_sel_win(qn_t, ks, vs_t, kw, vw_t, sel, tq):
    B, n = qn_t.shape[:2]
    S = n * tq
    G = NSA_KV_GROUPS
    n_sel = S // NSA_SEL_BLOCK
    gcols = NSA_HPG * HEAD_DIM
    kspec = pl.BlockSpec((1, n, tq, LANES), lambda b, g, i: (b, 0, 0, 0))
    vtspec = pl.BlockSpec((1, n, HEAD_DIM, tq), lambda b, g, i: (b, 0, g, 0))
    ospec = pl.BlockSpec((1, tq, gcols), lambda b, g, i: (b, i, g))
    return pl.pallas_call(
        functools.partial(_nsa_sel_win_kernel, tq=tq, n_sel=n_sel),
        grid=(B, G, n),
        in_specs=[pl.BlockSpec((1, 1, gcols, tq), lambda b, g, i: (b, i, g, 0)), kspec, vtspec, kspec, vtspec,
                  pl.BlockSpec((1, 1, n_sel, tq), lambda b, g, i: (b, g, 0, i))],
        out_specs=[ospec, ospec],
        out_shape=[jax.ShapeDtypeStruct((B, S, G * gcols), F32)] * 2,
        scratch_shapes=[pltpu.VMEM((tq, NSA_HPG * tq), F32), pltpu.VMEM((tq, NSA_HPG * tq), BF16)],
        compiler_params=_params(("arbitrary", "arbitrary", "arbitrary")),
        name="nsa_sel_win_attention",
    )(qn_t, ks.reshape(B, n, tq, LANES), vs_t, kw.reshape(B, n, tq, LANES), vw_t, sel)


def _attn_outproj_kernel(x_ref, d_ref, oc_ref, os_ref, ow_ref, gt_ref, ge_ref, w_ref, o_ref):
    ghi, glo = _split2(gt_ref[...])

    def gate(r):
        return _dot(ghi, ge_ref[r]) + _dot(glo, ge_ref[r])

    nsa = gate(0) * oc_ref[...] + gate(1) * os_ref[...] + gate(2) * ow_ref[...]
    y = _dot(_bf(d_ref[...]), w_ref[0:512, :]) + _dot(_bf(nsa), w_ref[512:1024, :])
    o_ref[...] = x_ref[...] + y


def _attn_outproj(x2d, diff_o, o_cmp, o_sel, o_win, gates, w_out):
    T, D = x2d.shape
    tm = min(256, T)
    col = jnp.arange(512) // HEAD_DIM
    src = jnp.arange(LANES)
    ge = jnp.stack([_bf((src[:, None] == col[None, :] * 3 + r).astype(F32)) for r in range(3)])
    row = lambda n: pl.BlockSpec((tm, n), lambda i: (i, 0))
    return pl.pallas_call(
        _attn_outproj_kernel,
        grid=(T // tm,),
        in_specs=[row(D), row(512), row(512), row(512), row(512), row(LANES),
                  pl.BlockSpec((3, LANES, 512), lambda i: (0, 0, 0)),
                  pl.BlockSpec((D, D), lambda i: (0, 0))],
        out_specs=row(D),
        out_shape=jax.ShapeDtypeStruct((T, D), F32),
        compiler_params=_params(("arbitrary",)),
        name="attn_outproj",
    )(x2d, diff_o, o_cmp, o_sel, o_win, gates, ge, _bf(w_out))


def _hybrid_attention_layer(x, layer, norm_gain, w_in, gate_bias, dq_gain, dk_gain, lam_p, subln, nq_gain, nk_gain,
                            cmp_pe, cmp_w1, cmp_w2, w_out):
    B, S, D = x.shape
    T = B * S
    tq = min(256, S)
    qd_t, kd, vd_t, qn_t, cmp_kv, ks, vs_t, kw, vw_t, gates = _attn_inproj(
        x, norm_gain, w_in, gate_bias, dq_gain, dk_gain, nq_gain, nk_gain, tq)
    lambda_init = 0.8 - 0.6 * math.exp(-0.3 * layer)
    diff_o = _diff_attention(qd_t, kd, vd_t, lam_p, subln, lambda_init, tq)
    kc, vc_t = _nsa_compress(cmp_kv, cmp_pe, cmp_w1, cmp_w2, nk_gain[0])
    o_cmp, sel = _nsa_cmp_attention(qn_t, kc, vc_t, tq)
    o_sel, o_win = _nsa_sel_win(qn_t, ks, vs_t, kw, vw_t, sel, tq)
    flat = lambda a: a.reshape(T, a.shape[-1])
    out = _attn_outproj(flat(x), flat(diff_o), flat(o_cmp), flat(o_sel), flat(o_win), flat(gates), w_out)
    return out.reshape(B, S, D)


def _softplus(z):
    return jnp.maximum(z, 0.0) + jnp.log(1.0 + jnp.exp(-jnp.abs(z)))


def _head_pool(d, head):
    member = (jnp.arange(d)[:, None] // head == jnp.arange(LANES)[None, :]).astype(BF16)
    return member, member.T


def _head_sum(parts, pool, expand):
    sums = _dot(parts[0], pool)
    for part in parts[1:]:
        sums = sums + _dot(part, pool)
    hi, lo = _split2(sums)
    return _dot(hi, expand) + _dot(lo, expand)


def _rwkv_proj_kernel(x_ref, xp_ref, g_ref, mix_ref, wr_ref, wk_ref, wv_ref, w1_ref, w2_ref, a1_ref, a2_ref,
                      g1_ref, g2_ref, vec_ref, pool_ref, expand_ref,
                      r_ref, wl_ref, k_ref, v_ref, kk_ref, b_ref, bonus_ref, gate_ref):
    i = pl.program_id(1)
    gain = g_ref[...]
    xn = _rms(x_ref[0], gain)
    prev = _rms(xp_ref[0], gain)[7:8] * jnp.where(i > 0, 1.0, 0.0)
    shifted = pltpu.roll(xn, 1, 0)
    first_row = lax.broadcasted_iota(jnp.int32, xn.shape, 0) == 0
    dx = jnp.where(first_row, prev, shifted) - xn
    mix = mix_ref[...]
    xr, xw, xk, xv, xa, xg = (_bf(xn + dx * mix[j:j + 1]) for j in range(6))
    vec = vec_ref[...]
    w0, a0, k_k, k_a, r_k = (vec[j:j + 1] for j in range(5))
    r = _dot(xr, wr_ref[...])
    k = _dot(xk, wk_ref[...])
    v = _dot(xv, wv_ref[...])
    w = -_softplus(-(w0 + _dot(_bf(jnp.tanh(_dot(xw, w1_ref[...]))), w2_ref[...]))) - 0.5
    a = jax.nn.sigmoid(a0 + _dot(_bf(_dot(xa, a1_ref[...])), a2_ref[...]))
    gate_ref[0] = _dot(_bf(jax.nn.sigmoid(_dot(xg, g1_ref[...]))), g2_ref[...])
    pool = pool_ref[...]
    expand = expand_ref[...]
    kk = k * k_k
    kk = kk / jnp.maximum(jnp.sqrt(_head_sum([_bf(kk * kk)], pool, expand)), 1e-12)
    k = k * (1.0 + (a - 1.0) * k_a)
    r_ref[0] = r
    wl_ref[0] = -jnp.exp(w)
    k_ref[0] = k
    v_ref[0] = v
    kk_ref[0] = kk
    b_ref[0] = kk * a
    bonus_ref[0] = _head_sum([_bf(r * k * r_k)], pool, expand) * v


def _rwkv_proj(x, gain, mix, w_r, w_k, w_v, w0, w1, w2, a0, a1, a2, g1, g2, k_k, k_a, r_k):
    B, S, D = x.shape
    tm = min(256, S)
    pad_c = lambda m, n: _bf(jnp.pad(m, ((0, 0), (0, n - m.shape[1]))))
    pad_r = lambda m, n: _bf(jnp.pad(m, ((0, n - m.shape[0]), (0, 0))))
    lw = LANES
    lg = 2 * LANES
    consts = (gain.reshape(1, D), jnp.pad(mix, ((0, 2), (0, 0))), _bf(w_r), _bf(w_k), _bf(w_v),
              pad_c(w1, lw), pad_r(w2, lw), pad_c(a1, lw), pad_r(a2, lw), pad_c(g1, lg), pad_r(g2, lg),
              jnp.pad(jnp.stack([w0, a0, k_k, k_a, r_k]), ((0, 3), (0, 0)))) + _head_pool(D, RWKV_HEAD)
    full = lambda a: pl.BlockSpec(a.shape, lambda b, i: (0,) * a.ndim)
    tile = pl.BlockSpec((1, tm, D), lambda b, i: (b, i, 0))
    return pl.pallas_call(
        _rwkv_proj_kernel,
        grid=(B, S // tm),
        in_specs=[tile, pl.BlockSpec((1, 8, D), lambda b, i: (b, jnp.maximum(i * (tm // 8) - 1, 0), 0))]
        + [full(a) for a in consts],
        out_specs=[tile] * 8,
        out_shape=[jax.ShapeDtypeStruct((B, S, D), F32)] * 8,
        compiler_params=_params(("arbitrary", "arbitrary")),
        name="rwkv_proj",
    )(x, x, *consts)


def _rwkv_chunk_kernel(r_ref, wl_ref, k_ref, v_ref, kk_ref, b_ref, m_ref, g0_ref, rq_ref, y0_ref, *, cpb):
    C = RWKV_CHUNK
    lane = lax.broadcasted_iota(jnp.int32, (C, LANES), 1)
    head0 = lane < RWKV_HEAD
    ti = lax.broadcasted_iota(jnp.int32, (2 * C, 2 * C), 0) & (C - 1)
    tj = lax.broadcasted_iota(jnp.int32, (2 * C, 2 * C), 1) & (C - 1)
    strict = ti > tj
    incl = ti >= tj
    eye = lax.broadcasted_iota(jnp.int32, (LANES, LANES), 0) == lax.broadcasted_iota(jnp.int32, (LANES, LANES), 1)
    ltri = _bf(jnp.where(lax.broadcasted_iota(jnp.int32, (C, C), 0) >= lax.broadcasted_iota(jnp.int32, (C, C), 1),
                         1.0, 0.0))

    def stack(x):
        return jnp.concatenate([jnp.where(head0, x, 0.0), jnp.where(head0, 0.0, x)], axis=0)

    chunks = range(cpb)
    rows = [slice(c * C, (c + 1) * C) for c in chunks]
    wl = [wl_ref[0, rows[c], :] for c in chunks]
    cum = []
    for c in chunks:
        h1 = _bf(wl[c])
        r1 = wl[c] - h1.astype(F32)
        h2 = _bf(r1)
        h3 = _bf(r1 - h2.astype(F32))
        cum.append(_dot(ltri, h1) + _dot(ltri, h2) + _dot(ltri, h3))
    total = [cum[c][C - 1:C, :] for c in chunks]
    lhs_kk, lhs_r, vs, a = [], [], [], []
    for c in chunks:
        p_inv = jnp.exp(-cum[c])
        lhs_kk.append(stack(kk_ref[0, rows[c], :] * jnp.exp(cum[c] - wl[c])))
        lhs_r.append(stack(r_ref[0, rows[c], :] * jnp.exp(cum[c])))
        vs.append(_bf(stack(v_ref[0, rows[c], :])))
        a.append(_dot_nt(_bf(jnp.concatenate([lhs_kk[c], lhs_r[c]], axis=0)),
                         _bf(jnp.concatenate([stack(b_ref[0, rows[c], :] * p_inv),
                                              stack(k_ref[0, rows[c], :] * p_inv)], axis=0))))
    npow = [_bf(jnp.where(strict, a[c][0:2 * C, 0:2 * C], 0.0)) for c in chunks]
    a_k = [_bf(jnp.where(strict, a[c][0:2 * C, 2 * C:4 * C], 0.0)) for c in chunks]
    a_rb = [_bf(jnp.where(incl, a[c][2 * C:4 * C, 0:2 * C], 0.0)) for c in chunks]
    a_rk = [_bf(jnp.where(incl, a[c][2 * C:4 * C, 2 * C:4 * C], 0.0)) for c in chunks]
    x = [jnp.concatenate([_dot(a_k[c], vs[c]), lhs_kk[c]], axis=1) for c in chunks]
    x = [x[c] - _dot(npow[c], _bf(x[c])) for c in chunks]
    for _ in range(5):
        npow = [_bf(_dot(npow[c], npow[c])) for c in chunks]
        x = [x[c] + _dot(npow[c], _bf(x[c])) for c in chunks]
    uw = [_bf(-x[c]) for c in chunks]
    for c in chunks:
        t = _dot(a_rb[c], uw[c])
        y0_ref[0, 0, c] = _bf(_dot(a_rk[c], vs[c]) + t[:, 0:LANES])
        rq_ref[0, 0, c] = _bf(lhs_r[c] + t[:, LANES:2 * LANES])
    for c in chunks:
        p_end = jnp.exp(total[c] - cum[c])
        bc = _bf(stack(b_ref[0, rows[c], :] * p_end))
        kc = _bf(stack(k_ref[0, rows[c], :] * p_end))
        t = _dot_tn(bc, uw[c])
        g0_ref[0, 0, c] = _bf(_dot_tn(kc, vs[c]) + t[:, 0:LANES])
        m_ref[0, 0, c] = jnp.where(eye, jnp.exp(total[c]), 0.0) + t[:, LANES:2 * LANES]


def _rwkv_chunks(r, wl, k, v, kk, b):
    B, S, D = r.shape
    C = RWKV_CHUNK
    nc = S // C
    cpb = min(8, nc)
    hp = D // LANES
    tile = pl.BlockSpec((1, cpb * C, LANES), lambda bi, h, c: (bi, c, h))
    out = pl.BlockSpec((1, 1, cpb, LANES, LANES), lambda bi, h, c: (bi, h, c, 0, 0))
    return pl.pallas_call(
        functools.partial(_rwkv_chunk_kernel, cpb=cpb),
        grid=(B, hp, nc // cpb),
        in_specs=[tile] * 6,
        out_specs=[out] * 4,
        out_shape=[jax.ShapeDtypeStruct((B, hp, nc, LANES, LANES), F32)]
        + [jax.ShapeDtypeStruct((B, hp, nc, LANES, LANES), BF16)] * 3,
        compiler_params=_params(("arbitrary", "arbitrary", "arbitrary")),
        name="rwkv_chunk_summaries",
    )(r, wl, k, v, kk, b)


def _rwkv_scan_kernel(m_ref, g0_ref, rq_ref, y0_ref, y_ref, st_ref, *, cpb, hp):
    C = RWKV_CHUNK

    @pl.when(pl.program_id(1) == 0)
    def _():
        st_ref[...] = jnp.zeros_like(st_ref)

    for c in range(cpb):
        for h in range(hp):
            st = st_ref[h]
            s_hi, s_lo = _split2(st)
            y = y0_ref[0, h, c].astype(F32) + _dot(rq_ref[0, h, c], s_hi)
            y_ref[0, c * C:(c + 1) * C, h * LANES:(h + 1) * LANES] = y[0:C] + y[C:2 * C]
            m_hi, m_lo = _split2(m_ref[0, h, c])
            st_ref[h] = g0_ref[0, h, c] + _dot(m_hi, s_hi) + _dot(m_hi, s_lo) + _dot(m_lo, s_hi)


def _rwkv_scan(m, g0, rq, y0, S):
    B, hp, nc = m.shape[:3]
    C = RWKV_CHUNK
    cpb = min(4, nc)
    blk = pl.BlockSpec((1, hp, cpb, LANES, LANES), lambda bi, c: (bi, 0, c, 0, 0))
    return pl.pallas_call(
        functools.partial(_rwkv_scan_kernel, cpb=cpb, hp=hp),
        grid=(B, nc // cpb),
        in_specs=[blk] * 4,
        out_specs=pl.BlockSpec((1, cpb * C, hp * LANES), lambda bi, c: (bi, c, 0)),
        out_shape=jax.ShapeDtypeStruct((B, S, hp * LANES), F32),
        scratch_shapes=[pltpu.VMEM((hp, LANES, LANES), F32)],
        compiler_params=_params(("arbitrary", "arbitrary")),
        name="rwkv_state_scan",
    )(m, g0, rq, y0)


def _rwkv_out_kernel(x_ref, y_ref, bonus_ref, gate_ref, lnw_ref, lnb_ref, pool_ref, expand_ref, wo_ref, o_ref):
    y = y_ref[...]
    pool = pool_ref[...]
    expand = expand_ref[...]
    mu = _head_sum(_split2(y), pool, expand) * (1.0 / RWKV_HEAD)
    dev = y - mu
    var = _head_sum(_split2(dev * dev), pool, expand) * (1.0 / RWKV_HEAD)
    yn = dev * lax.rsqrt(var + RWKV_LNX_EPS) * lnw_ref[...] + lnb_ref[...]
    z = (yn + bonus_ref[...]) * gate_ref[...]
    o_ref[...] = x_ref[...] + _dot(_bf(z), wo_ref[...])


def _rwkv_out(x2d, y, bonus, gate, ln_w, ln_b, w_o):
    T, D = x2d.shape
    tm = min(256, T)
    row = pl.BlockSpec((tm, D), lambda i: (i, 0))
    vecs = pl.BlockSpec((1, D), lambda i: (0, 0))
    full = lambda shp: pl.BlockSpec(shp, lambda i: (0, 0))
    pool, expand = _head_pool(D, RWKV_HEAD)
    return pl.pallas_call(
        _rwkv_out_kernel,
        grid=(T // tm,),
        in_specs=[row, row, row, row, vecs, vecs, full((D, LANES)), full((LANES, D)), full((D, D))],
        out_specs=row,
        out_shape=jax.ShapeDtypeStruct((T, D), F32),
        compiler_params=_params(("arbitrary",)),
        name="rwkv_out",
    )(x2d, y, bonus, gate, ln_w.reshape(1, D), ln_b.reshape(1, D), pool, expand, _bf(w_o))


def _rwkv_layer(x, norm_gain, mix, w_r, w_k, w_v, w0, w1, w2, a0, a1, a2, g1, g2, k_k, k_a, r_k, ln_w, ln_b, w_o):
    B, S, D = x.shape
    r, wl, k, v, kk, b, bonus, gate = _rwkv_proj(x, norm_gain, mix, w_r, w_k, w_v, w0, w1, w2, a0, a1, a2,
                                                 g1, g2, k_k, k_a, r_k)
    m, g0, rq, y0 = _rwkv_chunks(r, wl, k, v, kk, b)
    y = _rwkv_scan(m, g0, rq, y0, S)
    T = B * S
    out = _rwkv_out(x.reshape(T, D), y.reshape(T, D), bonus.reshape(T, D), gate.reshape(T, D), ln_w, ln_b, w_o)
    return out.reshape(B, S, D)


def _router_kernel(x_ref, g_ref, whi_ref, wlo_ref, b_ref, xn_ref, route_ref):
    xn = _rms(x_ref[...], g_ref[...])
    xn_ref[...] = xn
    hi, lo = _split2(xn)
    logits = _dot(hi, whi_ref[...]) + _dot(hi, wlo_ref[...]) + _dot(lo, whi_ref[...]) + b_ref[...]
    lane = lax.broadcasted_iota(jnp.int32, logits.shape, 1)
    lane_f = lane.astype(F32)

    def top(vals):
        mx = jnp.max(vals, axis=-1, keepdims=True)
        return mx, jnp.min(jnp.where(vals == mx, lane_f, float(LANES)), axis=-1, keepdims=True)

    glog = jnp.where(lane < N_GROUPS, logits, NEG_INF)
    gmax, gidx = top(glog)
    gsum = jnp.sum(jnp.where(lane < N_GROUPS, jnp.exp(glog - gmax), 0.0), axis=-1, keepdims=True)
    grp_p = 1.0 / gsum
    first = float(N_GROUPS) + EXPERTS_PER_GROUP * gidx
    elog = jnp.where(lane_f >= first, jnp.where(lane_f < first + EXPERTS_PER_GROUP, logits, NEG_INF), NEG_INF)
    v1, i1 = top(elog)
    v2, i2 = top(jnp.where(lane_f == i1, NEG_INF, elog))
    e2 = jnp.exp(v2 - v1)
    gate1 = grp_p / (1.0 + e2)
    gate2 = grp_p * e2 / (1.0 + e2)
    route_ref[...] = jnp.where(lane == 0, i1 - N_GROUPS, jnp.where(lane == 1, i2 - N_GROUPS,
                               jnp.where(lane == 2, gate1, jnp.where(lane == 3, gate2, 0.0))))


def _router(x2d, gain, wg, bg, we, be):
    T, D = x2d.shape
    tm = min(256, T)
    w = jnp.pad(jnp.concatenate([wg, we], axis=1), ((0, 0), (0, LANES - N_GROUPS - N_EXPERTS)))
    w_hi = _bf(w)
    w_lo = _bf(w - w_hi.astype(F32))
    bias = jnp.pad(jnp.concatenate([bg, be]), (0, LANES - N_GROUPS - N_EXPERTS)).reshape(1, LANES)
    row = lambda n: pl.BlockSpec((tm, n), lambda i: (i, 0))
    full = lambda a: pl.BlockSpec(a.shape, lambda i: (0, 0))
    consts = (gain.reshape(1, D), w_hi, w_lo, bias)
    return pl.pallas_call(
        _router_kernel,
        grid=(T // tm,),
        in_specs=[row(D)] + [full(a) for a in consts],
        out_specs=[row(D), row(LANES)],
        out_shape=[jax.ShapeDtypeStruct((T, D), F32), jax.ShapeDtypeStruct((T, LANES), F32)],
        compiler_params=_params(("arbitrary",)),
        name="moe_router",
    )(x2d, *consts)


def _slot_tokens_kernel(dest_ref, slot_ref):
    def clear(s, carry):
        slot_ref[s] = 0
        return carry
    lax.fori_loop(0, slot_ref.shape[0], clear, 0, unroll=16)

    def place(p, carry):
        slot_ref[dest_ref[p]] = lax.shift_right_logical(p, 1)
        return carry
    lax.fori_loop(0, dest_ref.shape[0], place, 0, unroll=16)


def _slot_tokens(dest, n_slots):
    return pl.pallas_call(
        _slot_tokens_kernel,
        in_specs=[pl.BlockSpec(memory_space=pltpu.SMEM)],
        out_specs=pl.BlockSpec(memory_space=pltpu.SMEM),
        out_shape=jax.ShapeDtypeStruct((n_slots,), jnp.int32),
        name="moe_slot_tokens",
    )(dest)


def _row_gather_start(src_hbm, idx_ref, idx_base, idx_stride, dst, sem, rows):
    for r in range(rows):
        src_row = idx_ref[idx_base + r * idx_stride]
        pltpu.make_async_copy(src_hbm.at[pl.ds(src_row, 1)], dst.at[pl.ds(r, 1)], sem).start(priority=r % 2)


def _row_gather_wait(src_hbm, dst, sem, rows):
    pltpu.make_async_copy(src_hbm.at[pl.ds(0, rows)], dst, sem).wait()


def _expert_kernel(be_ref, nu_ref, tok_ref, xn_hbm, wg_ref, wu_ref, wd_ref, o_ref, xbuf, sem):
    i = pl.program_id(0)
    nu = nu_ref[0]
    last = pl.num_programs(0) - 1
    cur = lax.rem(i, MOE_BUFFERS)
    nxt1 = lax.rem(i + 1, MOE_BUFFERS)
    nxt2 = lax.rem(i + 2, MOE_BUFFERS)

    @pl.when(i == 0)
    def _():
        for blk in range(2):
            _row_gather_start(xn_hbm, tok_ref, blk * MOE_ROWS, 1, xbuf.at[blk], sem.at[blk], MOE_ROWS)

    @pl.when(i < nu)
    def _():
        _row_gather_wait(xn_hbm, xbuf.at[cur], sem.at[cur], MOE_ROWS)
        x = _bf(xbuf[cur])
        hid = jax.nn.silu(_dot(x, _bf(wg_ref[0, 0]))) * _dot(x, _bf(wu_ref[0, 0]))
        o_ref[...] = _dot(_bf(hid), _bf(wd_ref[0, 0]))
        ahead = jnp.minimum(i + 2, last)
        _row_gather_start(xn_hbm, tok_ref, ahead * MOE_ROWS, 1, xbuf.at[nxt2], sem.at[nxt2], MOE_ROWS)

        @pl.when(i == nu - 1)
        def _():
            _row_gather_wait(xn_hbm, xbuf.at[nxt1], sem.at[nxt1], MOE_ROWS)
            _row_gather_wait(xn_hbm, xbuf.at[nxt2], sem.at[nxt2], MOE_ROWS)

    @pl.when(i >= nu)
    def _():
        o_ref[...] = jnp.zeros_like(o_ref)


def _experts(xn, slot_tok, block_e, n_used, layer, e_gate, e_up, e_down):
    T, D = xn.shape
    n_blocks = slot_tok.shape[0] // MOE_ROWS
    wspec = lambda shp: pl.BlockSpec((1, 1) + shp, lambda i, be, nu, tok: (layer, be[i], 0, 0))
    return pl.pallas_call(
        _expert_kernel,
        grid_spec=pltpu.PrefetchScalarGridSpec(
            num_scalar_prefetch=3,
            grid=(n_blocks,),
            in_specs=[pl.BlockSpec(memory_space=pl.ANY), wspec((D, D_EXPERT)), wspec((D, D_EXPERT)),
                      wspec((D_EXPERT, D))],
            out_specs=pl.BlockSpec((MOE_ROWS, D), lambda i, be, nu, tok: (i, 0)),
            scratch_shapes=[pltpu.VMEM((MOE_BUFFERS, MOE_ROWS, D), F32), pltpu.SemaphoreType.DMA((MOE_BUFFERS,))],
        ),
        out_shape=jax.ShapeDtypeStruct((n_blocks * MOE_ROWS, D), F32),
        compiler_params=_params(("arbitrary",)),
        name="moe_experts",
    )(block_e, n_used, slot_tok, xn, e_gate, e_up, e_down)


def _combine_kernel(dest_ref, x_ref, route_ref, ys_hbm, o_ref, buf, sem, *, tm):
    i = pl.program_id(0)
    cur = i & 1

    def start(tile, par):
        for k in range(2):
            _row_gather_start(ys_hbm, dest_ref, tile * (2 * tm) + k, 2, buf.at[par, k], sem.at[par], tm)

    @pl.when(i == 0)
    def _():
        start(0, 0)

    @pl.when(i + 1 < pl.num_programs(0))
    def _():
        start(i + 1, 1 - cur)

    for k in range(2):
        _row_gather_wait(ys_hbm, buf.at[cur, k], sem.at[cur], tm)
    route = route_ref[...]
    o_ref[...] = x_ref[...] + route[:, 2:3] * buf[cur, 0] + route[:, 3:4] * buf[cur, 1]


def _combine(x2d, y_slots, dest, route):
    T, D = x2d.shape
    tm = min(256, T)
    row = lambda n: pl.BlockSpec((tm, n), lambda i, dest: (i, 0))
    return pl.pallas_call(
        functools.partial(_combine_kernel, tm=tm),
        grid_spec=pltpu.PrefetchScalarGridSpec(
            num_scalar_prefetch=1,
            grid=(T // tm,),
            in_specs=[row(D), row(LANES), pl.BlockSpec(memory_space=pl.ANY)],
            out_specs=row(D),
            scratch_shapes=[pltpu.VMEM((2, 2, tm, D), F32), pltpu.SemaphoreType.DMA((2,))],
        ),
        out_shape=jax.ShapeDtypeStruct((T, D), F32),
        compiler_params=_params(("arbitrary",)),
        name="moe_combine",
    )(dest, x2d, route, y_slots)


def _moe_layer(x, gain, wg, bg, we, be, layer, e_gate, e_up, e_down):
    B, S, D = x.shape
    T = B * S
    x2d = x.reshape(T, D)
    xn, route = _router(x2d, gain, wg, bg, we, be)
    flat_e = route[:, 0:2].astype(jnp.int32).reshape(-1)
    n_pairs = 2 * T
    onehot = (flat_e[:, None] == jnp.arange(N_EXPERTS, dtype=jnp.int32)[None, :]).astype(jnp.int32)
    csum = jnp.cumsum(onehot, axis=0)
    rank = jnp.take_along_axis(csum, flat_e[:, None], axis=1)[:, 0] - 1
    counts = csum[-1]
    padded = (counts + MOE_ROWS - 1) // MOE_ROWS * MOE_ROWS
    pad_end = jnp.cumsum(padded)
    dest = (pad_end - padded)[flat_e] + rank
    n_blocks = -(-n_pairs // MOE_ROWS) + N_EXPERTS
    dest = dest.astype(jnp.int32)
    slot_tok = _slot_tokens(dest, n_blocks * MOE_ROWS)
    block_start = jnp.arange(n_blocks, dtype=jnp.int32) * MOE_ROWS
    block_e = jnp.minimum(jnp.sum((pad_end[None, :] <= block_start[:, None]).astype(jnp.int32), axis=1),
                          N_EXPERTS - 1)
    n_used = (pad_end[-1:] // MOE_ROWS).astype(jnp.int32)
    y_slots = _experts(xn, slot_tok, block_e, n_used, layer, e_gate, e_up, e_down)
    return _combine(x2d, y_slots, dest, route).reshape(B, S, D)


def kernel(x, mix_norm, attn_w_in, attn_gate_bias, diff_q_norm, diff_k_norm, diff_lambda, diff_subln, nsa_q_norm, nsa_k_norm, nsa_cmp_pe, nsa_cmp_w1, nsa_cmp_w2, attn_w_out, rwkv_mix, rwkv_w_r, rwkv_w_k, rwkv_w_v, rwkv_decay_w0, rwkv_decay_w1, rwkv_decay_w2, rwkv_iclr_a0, rwkv_iclr_a1, rwkv_iclr_a2, rwkv_gate_g1, rwkv_gate_g2, rwkv_k_k, rwkv_k_a, rwkv_r_k, rwkv_ln_w, rwkv_ln_b, rwkv_w_o, ffn_norm, router_group_w, router_group_b, router_expert_w, router_expert_b, expert_w_gate, expert_w_up, expert_w_down):
    depth = mix_norm.shape[0]
    for layer in range(depth):
        i = layer // 2
        if layer % 2 == 0:
            x = _hybrid_attention_layer(
                x, layer, mix_norm[layer], attn_w_in[i], attn_gate_bias[i], diff_q_norm[i], diff_k_norm[i],
                diff_lambda[i], diff_subln[i], nsa_q_norm[i], nsa_k_norm[i], nsa_cmp_pe[i], nsa_cmp_w1[i],
                nsa_cmp_w2[i], attn_w_out[i])
        else:
            x = _rwkv_layer(
                x, mix_norm[layer], rwkv_mix[i], rwkv_w_r[i], rwkv_w_k[i], rwkv_w_v[i], rwkv_decay_w0[i],
                rwkv_decay_w1[i], rwkv_decay_w2[i], rwkv_iclr_a0[i], rwkv_iclr_a1[i], rwkv_iclr_a2[i],
                rwkv_gate_g1[i], rwkv_gate_g2[i], rwkv_k_k[i], rwkv_k_a[i], rwkv_r_k[i], rwkv_ln_w[i],
                rwkv_ln_b[i], rwkv_w_o[i])
        x = _moe_layer(x, ffn_norm[layer], router_group_w[layer], router_group_b[layer], router_expert_w[layer],
                       router_expert_b[layer], layer, expert_w_gate, expert_w_up, expert_w_down)
    return x
```

```python
import functools
import math

import jax
import jax.numpy as jnp
from jax import lax
from jax.experimental import pallas as pl
from jax.experimental.pallas import tpu as pltpu

F32 = jnp.float32
BF16 = jnp.bfloat16

D_MODEL = 1024
HEAD_DIM = 64
NORM_EPS = 1e-6
NEG_INF = -1e30
FORCE_SCORE = 1e4

DIFF_HEADS = 8
DIFF_QK_DIM = 32
NSA_HEADS = 8
NSA_KV_GROUPS = 2
NSA_HPG = 4
NSA_CMP_BLOCK = 32
NSA_CMP_STRIDE = 16
NSA_SEL_BLOCK = 64
NSA_TOP_N = 16
NSA_WINDOW = 512
IN_COLS = 2840
IN_COLS_PAD = 2944
GATE_COLS = 24

RWKV_HEAD = 64
RWKV_LNX_EPS = 64e-5
RWKV_CHUNK = 64

N_GROUPS = 4
EXPERTS_PER_GROUP = 8
N_EXPERTS = 32
D_EXPERT = 256
MOE_ROWS = 256
MOE_BUFFERS = 3

LANES = 128
VMEM_LIMIT = 56 * 1024 * 1024


def _bf(x):
    return x.astype(BF16)


def _dot(a, b):
    return jnp.dot(a, b, preferred_element_type=F32)


def _dot_nt(a, b):
    return lax.dot_general(a, b, (((1,), (1,)), ((), ())), preferred_element_type=F32)


def _dot_tn(a, b):
    return lax.dot_general(a, b, (((0,), (0,)), ((), ())), preferred_element_type=F32)


def _split2(x):
    hi = _bf(x)
    lo = _bf(x - hi.astype(F32))
    return hi, lo


def _params(sem):
    return pltpu.CompilerParams(dimension_semantics=sem, vmem_limit_bytes=VMEM_LIMIT)


def _block_diag(n, group):
    r = jnp.arange(n) // group
    return ((r[:, None] == r[None, :]).astype(F32) / group).astype(BF16)


def _rms(x, gain):
    return x * lax.rsqrt(jnp.mean(x * x, axis=-1, keepdims=True) + NORM_EPS) * gain


def _inproj_kernel(x_ref, g_ref, wa_ref, wbt_ref, b_ref, bd32_ref, bd64_ref, gk_ref, gnk_ref, gq_ref, gnq_ref,
                   qd_ref, kd_ref, vd_ref, qn_ref, cmp_ref, ks_ref, vs_ref, kw_ref, vw_ref, gt_ref):
    xn = _bf(_rms(x_ref[0], g_ref[...]))
    ha = _dot(xn, wa_ref[...])
    hb = _dot_nt(wbt_ref[...], xn)
    bd32 = bd32_ref[...]
    bd64 = bd64_ref[...]

    def gnorm(seg, bd, gain):
        hi, lo = _split2(seg * seg)
        return seg * lax.rsqrt(_dot(hi, bd) + _dot(lo, bd) + NORM_EPS) * gain

    def gnorm_t(seg, bd, gain):
        hi, lo = _split2(seg * seg)
        return seg * lax.rsqrt(_dot(bd, hi) + _dot(bd, lo) + NORM_EPS) * gain

    kd_ref[0] = _bf(gnorm(ha[:, 0:512], bd32, gk_ref[...]))
    cmp_ref[0, 0] = ha[:, 512:640]
    cmp_ref[0, 1] = ha[:, 640:768]
    ksw = gnorm(ha[:, 768:1024], bd64[0:256, 0:256], gnk_ref[...])
    ks_ref[0] = _bf(ksw[:, 0:128])
    kw_ref[0] = _bf(ksw[:, 128:256])
    gt_ref[0] = jax.nn.sigmoid(ha[:, 1024:1152] + b_ref[...])
    qd_ref[0, 0] = _bf(gnorm_t(hb[0:512], bd32, gq_ref[...]))
    vd_ref[0, 0] = _bf(hb[512:1024])
    qn_ref[0, 0] = _bf(gnorm_t(hb[1024:1536], bd64, gnq_ref[...]))
    vs_ref[0, 0] = _bf(hb[1536:1664])
    vw_ref[0, 0] = _bf(hb[1664:1792])


def _attn_inproj(x, gain, w_in, gate_bias, dq_gain, dk_gain, nq_gain, nk_gain, tm):
    B, S, D = x.shape
    n = S // tm
    c = lambda lo, hi: w_in[:, lo:hi]
    gate_w = jnp.pad(c(2816, IN_COLS), ((0, 0), (0, LANES - GATE_COLS)))
    wa = _bf(jnp.concatenate([c(512, 1024), c(2048, 2304), c(2304, 2432), c(2560, 2688), gate_w], axis=1))
    wbt = _bf(jnp.concatenate([c(0, 512), c(1024, 1536), c(1536, 2048), c(2432, 2560), c(2688, 2816)], axis=1).T)
    bias = jnp.pad(gate_bias, (0, LANES - GATE_COLS)).reshape(1, LANES)
    log2e = math.log2(math.e)
    gk = jnp.tile(dk_gain, 16).reshape(1, 512)
    gnk = jnp.concatenate([jnp.tile(nk_gain[1], 2), jnp.tile(nk_gain[2], 2)]).reshape(1, 256)
    gq = (jnp.tile(dq_gain, 16) * (DIFF_QK_DIM ** -0.5 * log2e)).reshape(512, 1)
    gnq = (jnp.tile(nq_gain, 8) * (HEAD_DIM ** -0.5 * log2e)).reshape(512, 1)
    consts = (gain.reshape(1, D), wa, wbt, bias, _block_diag(512, 32), _block_diag(512, 64), gk, gnk, gq, gnq)
    full = lambda a: pl.BlockSpec(a.shape, lambda b, i: (0,) * a.ndim)
    tok = lambda w: pl.BlockSpec((1, tm, w), lambda b, i: (b, i, 0))
    tr = lambda r: pl.BlockSpec((1, 1, r, tm), lambda b, i: (b, i, 0, 0))
    tok_shape = lambda w, dt: jax.ShapeDtypeStruct((B, S, w), dt)
    tr_shape = lambda r: jax.ShapeDtypeStruct((B, n, r, tm), BF16)
    return pl.pallas_call(
        _inproj_kernel,
        grid=(B, n),
        in_specs=[tok(D)] + [full(a) for a in consts],
        out_specs=[tr(512), tok(512), tr(512), tr(512), pl.BlockSpec((1, 2, tm, LANES), lambda b, i: (b, 0, i, 0)),
                   tok(LANES), tr(LANES), tok(LANES), tr(LANES), tok(LANES)],
        out_shape=[tr_shape(512), tok_shape(512, BF16), tr_shape(512), tr_shape(512),
                   jax.ShapeDtypeStruct((B, 2, S, LANES), F32),
                   tok_shape(LANES, BF16), tr_shape(LANES), tok_shape(LANES, BF16), tr_shape(LANES),
                   tok_shape(LANES, F32)],
        compiler_params=_params(("arbitrary", "arbitrary")),
        name="attn_inproj",
    )(x, *consts)


SHIFT_LIMIT = 40.0


def _score_bound(q_gain, k_gain, dim):
    bound = dim * jnp.max(jnp.abs(q_gain)) * jnp.max(jnp.abs(k_gain)) * 1.02 + 0.25
    return jnp.stack([bound, (bound <= SHIFT_LIMIT).astype(F32)]).astype(F32)


def _flash_pass(lo, hi, scores, values, mask_body, mask_last, s_ref, p_ref, shift):
    tk, cols = s_ref.shape
    dv = HEAD_DIM
    p_ref[...] = jnp.zeros_like(p_ref)
    s_ref[...] = scores(lo)

    def process(j, carry, mask, s_next):
        s = s_ref[...]
        if mask is not None:
            s = mask(j, s)
        pv = values(jnp.maximum(j - 1, lo), p_ref[...])
        if shift is None:
            m, l, acc = carry
            m_new = jnp.maximum(m, jnp.max(s, axis=0, keepdims=True))
            alpha = jnp.exp2(m - m_new)
            p = jnp.exp2(s - m_new)
            carry = (m_new, alpha * l + jnp.sum(p, axis=0, keepdims=True), (acc + pv) * alpha)
        else:
            l, acc = carry
            p = jnp.exp2(s - shift)
            carry = (l + jnp.sum(p, axis=0, keepdims=True), acc + pv)
        p_ref[...] = _bf(p)
        if s_next is not None:
            s_ref[...] = s_next
        return carry

    init = (jnp.zeros((1, cols), F32), jnp.zeros((dv, cols), F32))
    if shift is None:
        init = (jnp.full((1, cols), NEG_INF, F32),) + init
    carry = lax.fori_loop(lo, hi, lambda j, c: process(j, c, mask_body, scores(j + 1)), init)
    l, acc = process(hi, carry, mask_last, None)[-2:]
    return (acc + values(hi, p_ref[...])) / l


def _flash_tiles(lo, hi, scores, values, mask_body, mask_last, s_ref, p_ref, o_ref, bound_ref):
    @pl.when(bound_ref[1] > 0.5)
    def _():
        o_ref[...] = _flash_pass(lo, hi, scores, values, mask_body, mask_last, s_ref, p_ref, bound_ref[0])

    @pl.when(bound_ref[1] <= 0.5)
    def _():
        o_ref[...] = _flash_pass(lo, hi, scores, values, mask_body, mask_last, s_ref, p_ref, None)

    return o_ref[...]


def _diff_attn_kernel(qt_ref, k_ref, vt_ref, lam_ref, sg_ref, bound_ref, o_ref, s_ref, p_ref, acc_ref, *,
                      tq, lambda_init):
    i = pl.program_id(2)
    q = qt_ref[0, 0]
    zero = jnp.zeros((DIFF_QK_DIM, tq), BF16)
    qb = jnp.concatenate(
        [jnp.concatenate([q[r * DIFF_QK_DIM:(r + 1) * DIFF_QK_DIM] if c == r else zero for c in range(4)], axis=1)
         for r in range(4)], axis=0)

    def causal(j, s):
        kpos = lax.broadcasted_iota(jnp.int32, (tq, 4 * tq), 0)
        qpos = lax.broadcasted_iota(jnp.int32, (tq, 4 * tq), 1) & (tq - 1)
        return jnp.where(kpos <= qpos, s, NEG_INF)

    def values(j, p):
        vt = vt_ref[0, j]
        return jnp.concatenate([_dot(vt[0:HEAD_DIM], p[:, 0:2 * tq]),
                                _dot(vt[HEAD_DIM:2 * HEAD_DIM], p[:, 2 * tq:4 * tq])], axis=1)

    o = _flash_tiles(0, i, lambda j: _dot(k_ref[0, j], qb), values, None, causal, s_ref, p_ref, acc_ref,
                     bound_ref)
    lp = lam_ref[...]
    lam = (jnp.exp(jnp.sum(lp[0:1] * lp[1:2], axis=-1, keepdims=True))
           - jnp.exp(jnp.sum(lp[2:3] * lp[3:4], axis=-1, keepdims=True)) + lambda_init)
    heads = []
    for h in range(2):
        oh = o[:, 2 * h * tq:(2 * h + 1) * tq] - lam * o[:, (2 * h + 1) * tq:(2 * h + 2) * tq]
        oh = oh * lax.rsqrt(jnp.mean(oh * oh, axis=0, keepdims=True) + NORM_EPS) * sg_ref[...]
        heads.append(oh * (1.0 - lambda_init))
    o_ref[0] = jnp.concatenate(heads, axis=0).T


def _diff_attention(qd_t, kd, vd_t, lam_p, subln, bound, lambda_init, tq):
    B, n = qd_t.shape[:2]
    S = n * tq
    pairs = DIFF_HEADS // 2
    return pl.pallas_call(
        functools.partial(_diff_attn_kernel, tq=tq, lambda_init=lambda_init),
        grid=(B, pairs, n),
        in_specs=[
            pl.BlockSpec((1, 1, LANES, tq), lambda b, h, i: (b, i, h, 0)),
            pl.BlockSpec((1, n, tq, LANES), lambda b, h, i: (b, 0, 0, h)),
            pl.BlockSpec((1, n, LANES, tq), lambda b, h, i: (b, 0, h, 0)),
            pl.BlockSpec((4, DIFF_QK_DIM), lambda b, h, i: (0, 0)),
            pl.BlockSpec((HEAD_DIM, 1), lambda b, h, i: (0, 0)),
            pl.BlockSpec(memory_space=pltpu.SMEM),
        ],
        out_specs=pl.BlockSpec((1, tq, LANES), lambda b, h, i: (b, i, h)),
        out_shape=jax.ShapeDtypeStruct((B, S, DIFF_HEADS * HEAD_DIM), F32),
        scratch_shapes=[pltpu.VMEM((tq, 4 * tq), F32), pltpu.VMEM((tq, 4 * tq), BF16),
                        pltpu.VMEM((HEAD_DIM, 4 * tq), F32)],
        compiler_params=_params(("arbitrary", "arbitrary", "arbitrary")),
        name="diff_attention",
    )(qd_t, kd.reshape(B, n, tq, DIFF_HEADS * HEAD_DIM), vd_t, lam_p, subln.reshape(HEAD_DIM, 1), bound)


def _nsa_compress_kernel(t_ref, pe_ref, w1_ref, w2_ref, bd_ref, kg_ref, kc_ref, vct_ref, *, nch):
    half = NSA_CMP_BLOCK // 2
    ya = jnp.zeros((nch, 2 * LANES), F32)
    yb = jnp.zeros((nch, 2 * LANES), F32)
    for l in range(half):
        rows = pl.ds(l, nch, stride=NSA_CMP_STRIDE)
        piece = jnp.concatenate([t_ref[0, 0, rows, :], t_ref[0, 1, rows, :]], axis=1)
        ya = ya + _dot(_bf(piece + pe_ref[l:l + 1]), w1_ref[l])
        yb = yb + _dot(_bf(piece + pe_ref[half + l:half + l + 1]), w1_ref[half + l])
    pre = ya + pltpu.roll(yb, nch - 1, 0)
    out = _dot(_bf(jax.nn.gelu(pre)), w2_ref[...])
    k = out[:, 0:LANES]
    hi, lo = _split2(k * k)
    ms = _dot(hi, bd_ref[...]) + _dot(lo, bd_ref[...])
    kc_ref[0] = _bf(k * lax.rsqrt(ms + NORM_EPS) * kg_ref[...])
    vct_ref[0] = _bf(out[:, LANES:2 * LANES].T)


def _nsa_compress(cmp_kv, cmp_pe, cmp_w1, cmp_w2, k_gain0):
    B, _, S, _ = cmp_kv.shape
    nch = S // NSA_CMP_STRIDE

    def over_groups(w):
        parts = [w[0], w[0], w[1], w[1]]
        zero = jnp.zeros_like(w[0])
        return jnp.concatenate(
            [jnp.concatenate([parts[r] if c == r else zero for c in range(4)], axis=-1) for r in range(4)], axis=-2)

    w1 = _bf(over_groups(cmp_w1.reshape(2, NSA_CMP_BLOCK, HEAD_DIM, HEAD_DIM)))
    w2 = _bf(over_groups(cmp_w2))
    pe = jnp.concatenate([cmp_pe[0], cmp_pe[0], cmp_pe[1], cmp_pe[1]], axis=-1)
    full = lambda a: pl.BlockSpec(a.shape, lambda b: (0,) * a.ndim)
    consts = (pe, w1, w2, _block_diag(LANES, HEAD_DIM), jnp.tile(k_gain0, 2).reshape(1, LANES))
    return pl.pallas_call(
        functools.partial(_nsa_compress_kernel, nch=nch),
        grid=(B,),
        in_specs=[pl.BlockSpec((1, 2, S, LANES), lambda b: (b, 0, 0, 0))] + [full(a) for a in consts],
        out_specs=[pl.BlockSpec((1, nch, LANES), lambda b: (b, 0, 0)), pl.BlockSpec((1, LANES, nch), lambda b: (b, 0, 0))],
        out_shape=[jax.ShapeDtypeStruct((B, nch, LANES), BF16), jax.ShapeDtypeStruct((B, LANES, nch), BF16)],
        compiler_params=_params(("arbitrary",)),
        name="nsa_compress",
    )(cmp_kv, *consts)


def _group_queries(q_ref, g):
    q = q_ref[0, 0]
    qg = jnp.concatenate([q[h * HEAD_DIM:(h + 1) * HEAD_DIM] for h in range(NSA_HPG)], axis=1)
    zero = jnp.zeros_like(qg)
    return jnp.where(g == 0, jnp.concatenate([qg, zero], axis=0), jnp.concatenate([zero, qg], axis=0))


def _store_heads(o_ref, o, tq):
    o_ref[0] = jnp.concatenate([o[:, h * tq:(h + 1) * tq] for h in range(NSA_HPG)], axis=0).T


def _nsa_cmp_attn_kernel(q_ref, k_ref, vt_ref, o_ref, sel_ref, *, tq, nch, n_sel, top_n):
    i = pl.program_id(2)
    cols = NSA_HPG * tq
    n_cmp = nch - 1
    s = _dot(k_ref[0], _group_queries(q_ref, pl.program_id(1)))
    pos = i * tq + (lax.broadcasted_iota(jnp.int32, (nch, cols), 1) & (tq - 1))
    c = lax.broadcasted_iota(jnp.int32, (nch, cols), 0)
    ok = jnp.where(c < n_cmp, c * NSA_CMP_STRIDE + (NSA_CMP_BLOCK - 1), 1 << 30) <= pos
    s = jnp.where(ok, s, NEG_INF)
    p = jnp.where(ok, jnp.exp2(s - jnp.max(s, axis=0, keepdims=True)), 0.0)
    l = jnp.sum(p, axis=0, keepdims=True)
    p = p / jnp.where(l > 0.0, l, 1.0)
    _store_heads(o_ref, _dot(vt_ref[0], _bf(p)), tq)

    pg = p[:, 0:tq]
    for h in range(1, NSA_HPG):
        pg = pg + p[:, h * tq:(h + 1) * tq]
    jj = lax.broadcasted_iota(jnp.int32, (n_sel, nch), 0) * NSA_SEL_BLOCK
    cc = lax.broadcasted_iota(jnp.int32, (n_sel, nch), 1) * NSA_CMP_STRIDE
    cover = jnp.where(cc < jj + NSA_SEL_BLOCK,
                      jnp.where(cc + NSA_CMP_BLOCK > jj, jnp.where(cc < n_cmp * NSA_CMP_STRIDE, 1.0, 0.0), 0.0), 0.0)
    cover = _bf(cover)
    hi, lo = _split2(pg)
    imp = _dot(cover, hi) + _dot(cover, lo)
    blk = lax.broadcasted_iota(jnp.int32, (n_sel, tq), 0)
    cur = (i * tq + lax.broadcasted_iota(jnp.int32, (n_sel, tq), 1)) >> 6
    imp = jnp.where(blk == cur, FORCE_SCORE, jnp.where(blk == 0, FORCE_SCORE, jnp.where(blk > cur, NEG_INF, imp)))
    blk_f = blk.astype(F32)
    sel = jnp.zeros((n_sel, tq), F32)
    for _ in range(top_n):
        mx = jnp.max(imp, axis=0, keepdims=True)
        first = jnp.min(jnp.where(imp == mx, blk_f, float(n_sel)), axis=0, keepdims=True)
        hit = blk_f == first
        sel = jnp.where(hit, 1.0, sel)
        imp = jnp.where(hit, -jnp.inf, imp)
    sel_ref[0, 0] = _bf(sel)


def _nsa_cmp_attention(qn_t, kc, vc_t, tq):
    B, n = qn_t.shape[:2]
    S = n * tq
    G = NSA_KV_GROUPS
    nch = S // NSA_CMP_STRIDE
    n_sel = S // NSA_SEL_BLOCK
    top_n = min(NSA_TOP_N, n_sel)
    gcols = NSA_HPG * HEAD_DIM
    return pl.pallas_call(
        functools.partial(_nsa_cmp_attn_kernel, tq=tq, nch=nch, n_sel=n_sel, top_n=top_n),
        grid=(B, G, n),
        in_specs=[
            pl.BlockSpec((1, 1, gcols, tq), lambda b, g, i: (b, i, g, 0)),
            pl.BlockSpec((1, nch, LANES), lambda b, g, i: (b, 0, 0)),
            pl.BlockSpec((1, HEAD_DIM, nch), lambda b, g, i: (b, g, 0)),
        ],
        out_specs=[pl.BlockSpec((1, tq, gcols), lambda b, g, i: (b, i, g)),
                   pl.BlockSpec((1, 1, n_sel, tq), lambda b, g, i: (b, g, 0, i))],
        out_shape=[jax.ShapeDtypeStruct((B, S, G * gcols), F32),
                   jax.ShapeDtypeStruct((B, G, n_sel, S), BF16)],
        compiler_params=_params(("arbitrary", "arbitrary", "arbitrary")),
        name="nsa_cmp_attention",
    )(qn_t, kc, vc_t)


def _nsa_sel_win_kernel(q_ref, ks_ref, vst_ref, kw_ref, vwt_ref, sel_ref, bs_ref, bw_ref, os_ref, ow_ref,
                        s_ref, p_ref, acc_ref, *, tq, n_sel):
    i = pl.program_id(2)
    qt = _group_queries(q_ref, pl.program_id(1))
    selm = sel_ref[0, 0]
    kloc = lax.broadcasted_iota(jnp.int32, (tq, tq), 0)
    qpos = i * tq + lax.broadcasted_iota(jnp.int32, (tq, tq), 1)
    blk_row = lax.broadcasted_iota(jnp.int32, (tq, n_sel), 0)
    blk_col = lax.broadcasted_iota(jnp.int32, (tq, n_sel), 1)

    def masked(s, keep):
        return jnp.where(jnp.concatenate([keep] * NSA_HPG, axis=1) > 0.5, s, NEG_INF)

    def sel_mask(j, s):
        expand = _bf(jnp.where(blk_col == ((j * tq + blk_row) >> 6), 1.0, 0.0))
        chosen = _dot(expand, selm)
        return masked(s, jnp.where(j * tq + kloc <= qpos, chosen, 0.0))

    o = _flash_tiles(0, i, lambda j: _dot(ks_ref[0, j], qt), lambda j, p: _dot(vst_ref[0, j], p),
                     sel_mask, sel_mask, s_ref, p_ref, acc_ref, bs_ref)
    _store_heads(os_ref, o, tq)

    def win_mask(j, s):
        dist = qpos - (j * tq + kloc)
        return masked(s, jnp.where(dist >= 0, jnp.where(dist < NSA_WINDOW, 1.0, 0.0), 0.0))

    first = jnp.maximum(i - (NSA_WINDOW - 1 + tq - 1) // tq, 0)
    o = _flash_tiles(first, i, lambda j: _dot(kw_ref[0, j], qt), lambda j, p: _dot(vwt_ref[0, j], p),
                     win_mask, win_mask, s_ref, p_ref, acc_ref, bw_ref)
    _store_heads(ow_ref, o, tq)


def _nsa_sel_win(qn_t, ks, vs_t, kw, vw_t, sel, bound_sel, bound_win, tq):
    B, n = qn_t.shape[:2]
    S = n * tq
    G = NSA_KV_GROUPS
    n_sel = S // NSA_SEL_BLOCK
    gcols = NSA_HPG * HEAD_DIM
    kspec = pl.BlockSpec((1, n, tq, LANES), lambda b, g, i: (b, 0, 0, 0))
    vtspec = pl.BlockSpec((1, n, HEAD_DIM, tq), lambda b, g, i: (b, 0, g, 0))
    ospec = pl.BlockSpec((1, tq, gcols), lambda b, g, i: (b, i, g))
    return pl.pallas_call(
        functools.partial(_nsa_sel_win_kernel, tq=tq, n_sel=n_sel),
        grid=(B, G, n),
        in_specs=[pl.BlockSpec((1, 1, gcols, tq), lambda b, g, i: (b, i, g, 0)), kspec, vtspec, kspec, vtspec,
                  pl.BlockSpec((1, 1, n_sel, tq), lambda b, g, i: (b, g, 0, i)),
                  pl.BlockSpec(memory_space=pltpu.SMEM), pl.BlockSpec(memory_space=pltpu.SMEM)],
        out_specs=[ospec, ospec],
        out_shape=[jax.ShapeDtypeStruct((B, S, G * gcols), F32)] * 2,
        scratch_shapes=[pltpu.VMEM((tq, NSA_HPG * tq), F32), pltpu.VMEM((tq, NSA_HPG * tq), BF16),
                        pltpu.VMEM((HEAD_DIM, NSA_HPG * tq), F32)],
        compiler_params=_params(("arbitrary", "arbitrary", "arbitrary")),
        name="nsa_sel_win_attention",
    )(qn_t, ks.reshape(B, n, tq, LANES), vs_t, kw.reshape(B, n, tq, LANES), vw_t, sel, bound_sel, bound_win)


def _attn_outproj_kernel(x_ref, d_ref, oc_ref, os_ref, ow_ref, gt_ref, ge_ref, w_ref, o_ref):
    ghi, glo = _split2(gt_ref[...])

    def gate(r):
        return _dot(ghi, ge_ref[r]) + _dot(glo, ge_ref[r])

    nsa = gate(0) * oc_ref[...] + gate(1) * os_ref[...] + gate(2) * ow_ref[...]
    y = _dot(_bf(d_ref[...]), w_ref[0:512, :]) + _dot(_bf(nsa), w_ref[512:1024, :])
    o_ref[...] = x_ref[...] + y


def _attn_outproj(x2d, diff_o, o_cmp, o_sel, o_win, gates, w_out):
    T, D = x2d.shape
    tm = min(256, T)
    col = jnp.arange(512) // HEAD_DIM
    src = jnp.arange(LANES)
    ge = jnp.stack([_bf((src[:, None] == col[None, :] * 3 + r).astype(F32)) for r in range(3)])
    row = lambda n: pl.BlockSpec((tm, n), lambda i: (i, 0))
    return pl.pallas_call(
        _attn_outproj_kernel,
        grid=(T // tm,),
        in_specs=[row(D), row(512), row(512), row(512), row(512), row(LANES),
                  pl.BlockSpec((3, LANES, 512), lambda i: (0, 0, 0)),
                  pl.BlockSpec((D, D), lambda i: (0, 0))],
        out_specs=row(D),
        out_shape=jax.ShapeDtypeStruct((T, D), F32),
        compiler_params=_params(("arbitrary",)),
        name="attn_outproj",
    )(x2d, diff_o, o_cmp, o_sel, o_win, gates, ge, _bf(w_out))


def _hybrid_attention_layer(x, layer, norm_gain, w_in, gate_bias, dq_gain, dk_gain, lam_p, subln, nq_gain, nk_gain,
                            cmp_pe, cmp_w1, cmp_w2, w_out):
    B, S, D = x.shape
    T = B * S
    tq = min(256, S)
    qd_t, kd, vd_t, qn_t, cmp_kv, ks, vs_t, kw, vw_t, gates = _attn_inproj(
        x, norm_gain, w_in, gate_bias, dq_gain, dk_gain, nq_gain, nk_gain, tq)
    lambda_init = 0.8 - 0.6 * math.exp(-0.3 * layer)
    log2e = math.log2(math.e)
    dq_scaled = dq_gain * (DIFF_QK_DIM ** -0.5 * log2e)
    nq_scaled = nq_gain * (HEAD_DIM ** -0.5 * log2e)
    diff_o = _diff_attention(qd_t, kd, vd_t, lam_p, subln, _score_bound(dq_scaled, dk_gain, DIFF_QK_DIM),
                             lambda_init, tq)
    kc, vc_t = _nsa_compress(cmp_kv, cmp_pe, cmp_w1, cmp_w2, nk_gain[0])
    o_cmp, sel = _nsa_cmp_attention(qn_t, kc, vc_t, tq)
    o_sel, o_win = _nsa_sel_win(qn_t, ks, vs_t, kw, vw_t, sel, _score_bound(nq_scaled, nk_gain[1], HEAD_DIM),
                                _score_bound(nq_scaled, nk_gain[2], HEAD_DIM), tq)
    flat = lambda a: a.reshape(T, a.shape[-1])
    out = _attn_outproj(flat(x), flat(diff_o), flat(o_cmp), flat(o_sel), flat(o_win), flat(gates), w_out)
    return out.reshape(B, S, D)


def _softplus(z):
    return jnp.maximum(z, 0.0) + jnp.log(1.0 + jnp.exp(-jnp.abs(z)))


def _head_pool(d, head):
    member = (jnp.arange(d)[:, None] // head == jnp.arange(LANES)[None, :]).astype(BF16)
    return member, member.T


def _head_sum(parts, pool, expand):
    sums = _dot(parts[0], pool)
    for part in parts[1:]:
        sums = sums + _dot(part, pool)
    hi, lo = _split2(sums)
    return _dot(hi, expand) + _dot(lo, expand)


def _rwkv_proj_kernel(x_ref, xp_ref, g_ref, mix_ref, wr_ref, wk_ref, wv_ref, w1_ref, w2_ref, a1_ref, a2_ref,
                      g1_ref, g2_ref, vec_ref, pool_ref, expand_ref,
                      r_ref, wl_ref, k_ref, v_ref, kk_ref, b_ref, bonus_ref, gate_ref):
    i = pl.program_id(1)
    gain = g_ref[...]
    xn = _rms(x_ref[0], gain)
    prev = _rms(xp_ref[0], gain)[7:8] * jnp.where(i > 0, 1.0, 0.0)
    shifted = pltpu.roll(xn, 1, 0)
    first_row = lax.broadcasted_iota(jnp.int32, xn.shape, 0) == 0
    dx = jnp.where(first_row, prev, shifted) - xn
    mix = mix_ref[...]
    xr, xw, xk, xv, xa, xg = (_bf(xn + dx * mix[j:j + 1]) for j in range(6))
    vec = vec_ref[...]
    w0, a0, k_k, k_a, r_k = (vec[j:j + 1] for j in range(5))
    r = _dot(xr, wr_ref[...])
    k = _dot(xk, wk_ref[...])
    v = _dot(xv, wv_ref[...])
    w = -_softplus(-(w0 + _dot(_bf(jnp.tanh(_dot(xw, w1_ref[...]))), w2_ref[...]))) - 0.5
    a = jax.nn.sigmoid(a0 + _dot(_bf(_dot(xa, a1_ref[...])), a2_ref[...]))
    gate_ref[0] = _dot(_bf(jax.nn.sigmoid(_dot(xg, g1_ref[...]))), g2_ref[...])
    pool = pool_ref[...]
    expand = expand_ref[...]
    kk = k * k_k
    kk = kk / jnp.maximum(jnp.sqrt(_head_sum([_bf(kk * kk)], pool, expand)), 1e-12)
    k = k * (1.0 + (a - 1.0) * k_a)
    r_ref[0] = r
    wl_ref[0] = -jnp.exp(w)
    k_ref[0] = k
    v_ref[0] = v
    kk_ref[0] = kk
    b_ref[0] = kk * a
    bonus_ref[0] = _head_sum([_bf(r * k * r_k)], pool, expand) * v


def _rwkv_proj(x, gain, mix, w_r, w_k, w_v, w0, w1, w2, a0, a1, a2, g1, g2, k_k, k_a, r_k):
    B, S, D = x.shape
    tm = min(256, S)
    pad_c = lambda m, n: _bf(jnp.pad(m, ((0, 0), (0, n - m.shape[1]))))
    pad_r = lambda m, n: _bf(jnp.pad(m, ((0, n - m.shape[0]), (0, 0))))
    lw = LANES
    lg = 2 * LANES
    consts = (gain.reshape(1, D), jnp.pad(mix, ((0, 2), (0, 0))), _bf(w_r), _bf(w_k), _bf(w_v),
              pad_c(w1, lw), pad_r(w2, lw), pad_c(a1, lw), pad_r(a2, lw), pad_c(g1, lg), pad_r(g2, lg),
              jnp.pad(jnp.stack([w0, a0, k_k, k_a, r_k]), ((0, 3), (0, 0)))) + _head_pool(D, RWKV_HEAD)
    full = lambda a: pl.BlockSpec(a.shape, lambda b, i: (0,) * a.ndim)
    tile = pl.BlockSpec((1, tm, D), lambda b, i: (b, i, 0))
    return pl.pallas_call(
        _rwkv_proj_kernel,
        grid=(B, S // tm),
        in_specs=[tile, pl.BlockSpec((1, 8, D), lambda b, i: (b, jnp.maximum(i * (tm // 8) - 1, 0), 0))]
        + [full(a) for a in consts],
        out_specs=[tile] * 8,
        out_shape=[jax.ShapeDtypeStruct((B, S, D), F32)] * 8,
        compiler_params=_params(("arbitrary", "arbitrary")),
        name="rwkv_proj",
    )(x, x, *consts)


def _rwkv_chunk_kernel(r_ref, wl_ref, k_ref, v_ref, kk_ref, b_ref, m_ref, g0_ref, rq_ref, y0_ref, *, cpb):
    C = RWKV_CHUNK
    lane = lax.broadcasted_iota(jnp.int32, (C, LANES), 1)
    head0 = lane < RWKV_HEAD
    ti = lax.broadcasted_iota(jnp.int32, (2 * C, 2 * C), 0) & (C - 1)
    tj = lax.broadcasted_iota(jnp.int32, (2 * C, 2 * C), 1) & (C - 1)
    strict = ti > tj
    incl = ti >= tj
    eye = lax.broadcasted_iota(jnp.int32, (LANES, LANES), 0) == lax.broadcasted_iota(jnp.int32, (LANES, LANES), 1)
    ltri = _bf(jnp.where(lax.broadcasted_iota(jnp.int32, (C, C), 0) >= lax.broadcasted_iota(jnp.int32, (C, C), 1),
                         1.0, 0.0))

    def stack(x):
        return jnp.concatenate([jnp.where(head0, x, 0.0), jnp.where(head0, 0.0, x)], axis=0)

    chunks = range(cpb)
    rows = [slice(c * C, (c + 1) * C) for c in chunks]
    wl = [wl_ref[0, rows[c], :] for c in chunks]
    cum = []
    for c in chunks:
        h1 = _bf(wl[c])
        r1 = wl[c] - h1.astype(F32)
        h2 = _bf(r1)
        h3 = _bf(r1 - h2.astype(F32))
        cum.append(_dot(ltri, h1) + _dot(ltri, h2) + _dot(ltri, h3))
    total = [cum[c][C - 1:C, :] for c in chunks]
    lhs_kk, lhs_r, vs, a = [], [], [], []
    for c in chunks:
        p_inv = jnp.exp(-cum[c])
        lhs_kk.append(stack(kk_ref[0, rows[c], :] * jnp.exp(cum[c] - wl[c])))
        lhs_r.append(stack(r_ref[0, rows[c], :] * jnp.exp(cum[c])))
        vs.append(_bf(stack(v_ref[0, rows[c], :])))
        a.append(_dot_nt(_bf(jnp.concatenate([lhs_kk[c], lhs_r[c]], axis=0)),
                         _bf(jnp.concatenate([stack(b_ref[0, rows[c], :] * p_inv),
                                              stack(k_ref[0, rows[c], :] * p_inv)], axis=0))))
    npow = [_bf(jnp.where(strict, a[c][0:2 * C, 0:2 * C], 0.0)) for c in chunks]
    a_k = [_bf(jnp.where(strict, a[c][0:2 * C, 2 * C:4 * C], 0.0)) for c in chunks]
    a_rb = [_bf(jnp.where(incl, a[c][2 * C:4 * C, 0:2 * C], 0.0)) for c in chunks]
    a_rk = [_bf(jnp.where(incl, a[c][2 * C:4 * C, 2 * C:4 * C], 0.0)) for c in chunks]
    x = [jnp.concatenate([_dot(a_k[c], vs[c]), lhs_kk[c]], axis=1) for c in chunks]
    x = [x[c] - _dot(npow[c], _bf(x[c])) for c in chunks]
    for _ in range(5):
        npow = [_bf(_dot(npow[c], npow[c])) for c in chunks]
        x = [x[c] + _dot(npow[c], _bf(x[c])) for c in chunks]
    uw = [_bf(-x[c]) for c in chunks]
    for c in chunks:
        t = _dot(a_rb[c], uw[c])
        y0_ref[0, 0, c] = _bf(_dot(a_rk[c], vs[c]) + t[:, 0:LANES])
        rq_ref[0, 0, c] = _bf(lhs_r[c] + t[:, LANES:2 * LANES])
    for c in chunks:
        p_end = jnp.exp(total[c] - cum[c])
        bc = _bf(stack(b_ref[0, rows[c], :] * p_end))
        kc = _bf(stack(k_ref[0, rows[c], :] * p_end))
        t = _dot_tn(bc, uw[c])
        g0_ref[0, 0, c] = _bf(_dot_tn(kc, vs[c]) + t[:, 0:LANES])
        m_ref[0, 0, c] = jnp.where(eye, jnp.exp(total[c]), 0.0) + t[:, LANES:2 * LANES]


def _rwkv_chunks(r, wl, k, v, kk, b):
    B, S, D = r.shape
    C = RWKV_CHUNK
    nc = S // C
    cpb = min(8, nc)
    hp = D // LANES
    tile = pl.BlockSpec((1, cpb * C, LANES), lambda bi, h, c: (bi, c, h))
    out = pl.BlockSpec((1, 1, cpb, LANES, LANES), lambda bi, h, c: (bi, h, c, 0, 0))
    return pl.pallas_call(
        functools.partial(_rwkv_chunk_kernel, cpb=cpb),
        grid=(B, hp, nc // cpb),
        in_specs=[tile] * 6,
        out_specs=[out] * 4,
        out_shape=[jax.ShapeDtypeStruct((B, hp, nc, LANES, LANES), F32)]
        + [jax.ShapeDtypeStruct((B, hp, nc, LANES, LANES), BF16)] * 3,
        compiler_params=_params(("arbitrary", "arbitrary", "arbitrary")),
        name="rwkv_chunk_summaries",
    )(r, wl, k, v, kk, b)


def _rwkv_scan_kernel(m_ref, g0_ref, rq_ref, y0_ref, y_ref, st_ref, *, cpb, hp):
    C = RWKV_CHUNK

    @pl.when(pl.program_id(1) == 0)
    def _():
        st_ref[...] = jnp.zeros_like(st_ref)

    for c in range(cpb):
        for h in range(hp):
            st = st_ref[h]
            s_hi, s_lo = _split2(st)
            y = y0_ref[0, h, c].astype(F32) + _dot(rq_ref[0, h, c], s_hi)
            y_ref[0, c * C:(c + 1) * C, h * LANES:(h + 1) * LANES] = y[0:C] + y[C:2 * C]
            m_hi, m_lo = _split2(m_ref[0, h, c])
            st_ref[h] = g0_ref[0, h, c] + _dot(m_hi, s_hi) + _dot(m_hi, s_lo) + _dot(m_lo, s_hi)


def _rwkv_scan(m, g0, rq, y0, S):
    B, hp, nc = m.shape[:3]
    C = RWKV_CHUNK
    cpb = min(4, nc)
    blk = pl.BlockSpec((1, hp, cpb, LANES, LANES), lambda bi, c: (bi, 0, c, 0, 0))
    return pl.pallas_call(
        functools.partial(_rwkv_scan_kernel, cpb=cpb, hp=hp),
        grid=(B, nc // cpb),
        in_specs=[blk] * 4,
        out_specs=pl.BlockSpec((1, cpb * C, hp * LANES), lambda bi, c: (bi, c, 0)),
        out_shape=jax.ShapeDtypeStruct((B, S, hp * LANES), F32),
        scratch_shapes=[pltpu.VMEM((hp, LANES, LANES), F32)],
        compiler_params=_params(("arbitrary", "arbitrary")),
        name="rwkv_state_scan",
    )(m, g0, rq, y0)


def _rwkv_out_kernel(x_ref, y_ref, bonus_ref, gate_ref, lnw_ref, lnb_ref, pool_ref, expand_ref, wo_ref, o_ref):
    y = y_ref[...]
    pool = pool_ref[...]
    expand = expand_ref[...]
    mu = _head_sum(_split2(y), pool, expand) * (1.0 / RWKV_HEAD)
    dev = y - mu
    var = _head_sum(_split2(dev * dev), pool, expand) * (1.0 / RWKV_HEAD)
    yn = dev * lax.rsqrt(var + RWKV_LNX_EPS) * lnw_ref[...] + lnb_ref[...]
    z = (yn + bonus_ref[...]) * gate_ref[...]
    o_ref[...] = x_ref[...] + _dot(_bf(z), wo_ref[...])


def _rwkv_out(x2d, y, bonus, gate, ln_w, ln_b, w_o):
    T, D = x2d.shape
    tm = min(256, T)
    row = pl.BlockSpec((tm, D), lambda i: (i, 0))
    vecs = pl.BlockSpec((1, D), lambda i: (0, 0))
    full = lambda shp: pl.BlockSpec(shp, lambda i: (0, 0))
    pool, expand = _head_pool(D, RWKV_HEAD)
    return pl.pallas_call(
        _rwkv_out_kernel,
        grid=(T // tm,),
        in_specs=[row, row, row, row, vecs, vecs, full((D, LANES)), full((LANES, D)), full((D, D))],
        out_specs=row,
        out_shape=jax.ShapeDtypeStruct((T, D), F32),
        compiler_params=_params(("arbitrary",)),
        name="rwkv_out",
    )(x2d, y, bonus, gate, ln_w.reshape(1, D), ln_b.reshape(1, D), pool, expand, _bf(w_o))


def _rwkv_layer(x, norm_gain, mix, w_r, w_k, w_v, w0, w1, w2, a0, a1, a2, g1, g2, k_k, k_a, r_k, ln_w, ln_b, w_o):
    B, S, D = x.shape
    r, wl, k, v, kk, b, bonus, gate = _rwkv_proj(x, norm_gain, mix, w_r, w_k, w_v, w0, w1, w2, a0, a1, a2,
                                                 g1, g2, k_k, k_a, r_k)
    m, g0, rq, y0 = _rwkv_chunks(r, wl, k, v, kk, b)
    y = _rwkv_scan(m, g0, rq, y0, S)
    T = B * S
    out = _rwkv_out(x.reshape(T, D), y.reshape(T, D), bonus.reshape(T, D), gate.reshape(T, D), ln_w, ln_b, w_o)
    return out.reshape(B, S, D)


def _router_kernel(x_ref, g_ref, whi_ref, wlo_ref, b_ref, xn_ref, route_ref):
    xn = _rms(x_ref[...], g_ref[...])
    xn_ref[...] = xn
    hi, lo = _split2(xn)
    logits = _dot(hi, whi_ref[...]) + _dot(hi, wlo_ref[...]) + _dot(lo, whi_ref[...]) + b_ref[...]
    lane = lax.broadcasted_iota(jnp.int32, logits.shape, 1)
    lane_f = lane.astype(F32)

    def top(vals):
        mx = jnp.max(vals, axis=-1, keepdims=True)
        return mx, jnp.min(jnp.where(vals == mx, lane_f, float(LANES)), axis=-1, keepdims=True)

    glog = jnp.where(lane < N_GROUPS, logits, NEG_INF)
    gmax, gidx = top(glog)
    gsum = jnp.sum(jnp.where(lane < N_GROUPS, jnp.exp(glog - gmax), 0.0), axis=-1, keepdims=True)
    grp_p = 1.0 / gsum
    first = float(N_GROUPS) + EXPERTS_PER_GROUP * gidx
    elog = jnp.where(lane_f >= first, jnp.where(lane_f < first + EXPERTS_PER_GROUP, logits, NEG_INF), NEG_INF)
    v1, i1 = top(elog)
    v2, i2 = top(jnp.where(lane_f == i1, NEG_INF, elog))
    e2 = jnp.exp(v2 - v1)
    gate1 = grp_p / (1.0 + e2)
    gate2 = grp_p * e2 / (1.0 + e2)
    route_ref[...] = jnp.where(lane == 0, i1 - N_GROUPS, jnp.where(lane == 1, i2 - N_GROUPS,
                               jnp.where(lane == 2, gate1, jnp.where(lane == 3, gate2, 0.0))))


def _router(x2d, gain, wg, bg, we, be):
    T, D = x2d.shape
    tm = min(256, T)
    w = jnp.pad(jnp.concatenate([wg, we], axis=1), ((0, 0), (0, LANES - N_GROUPS - N_EXPERTS)))
    w_hi = _bf(w)
    w_lo = _bf(w - w_hi.astype(F32))
    bias = jnp.pad(jnp.concatenate([bg, be]), (0, LANES - N_GROUPS - N_EXPERTS)).reshape(1, LANES)
    row = lambda n: pl.BlockSpec((tm, n), lambda i: (i, 0))
    full = lambda a: pl.BlockSpec(a.shape, lambda i: (0, 0))
    consts = (gain.reshape(1, D), w_hi, w_lo, bias)
    return pl.pallas_call(
        _router_kernel,
        grid=(T // tm,),
        in_specs=[row(D)] + [full(a) for a in consts],
        out_specs=[row(D), row(LANES)],
        out_shape=[jax.ShapeDtypeStruct((T, D), F32), jax.ShapeDtypeStruct((T, LANES), F32)],
        compiler_params=_params(("arbitrary",)),
        name="moe_router",
    )(x2d, *consts)


def _slot_tokens_kernel(dest_ref, slot_ref):
    def clear(s, carry):
        slot_ref[s] = 0
        return carry
    lax.fori_loop(0, slot_ref.shape[0], clear, 0, unroll=16)

    def place(p, carry):
        slot_ref[dest_ref[p]] = lax.shift_right_logical(p, 1)
        return carry
    lax.fori_loop(0, dest_ref.shape[0], place, 0, unroll=16)


def _slot_tokens(dest, n_slots):
    return pl.pallas_call(
        _slot_tokens_kernel,
        in_specs=[pl.BlockSpec(memory_space=pltpu.SMEM)],
        out_specs=pl.BlockSpec(memory_space=pltpu.SMEM),
        out_shape=jax.ShapeDtypeStruct((n_slots,), jnp.int32),
        name="moe_slot_tokens",
    )(dest)


def _row_gather_start(src_hbm, idx_ref, idx_base, idx_stride, dst, sem, rows, queues=(0, 1)):
    for r in range(rows):
        src_row = idx_ref[idx_base + r * idx_stride]
        copy = pltpu.make_async_copy(src_hbm.at[pl.ds(src_row, 1)], dst.at[pl.ds(r, 1)], sem)
        copy.start(priority=queues[r % len(queues)])


def _row_gather_wait(src_hbm, dst, sem, rows):
    pltpu.make_async_copy(src_hbm.at[pl.ds(0, rows)], dst, sem).wait()


def _expert_kernel(be_ref, nu_ref, tok_ref, xn_hbm, wg_ref, wu_ref, wd_ref, o_ref, xbuf, sem):
    i = pl.program_id(0)
    nu = nu_ref[0]
    last = pl.num_programs(0) - 1
    cur = lax.rem(i, MOE_BUFFERS)
    nxt1 = lax.rem(i + 1, MOE_BUFFERS)
    nxt2 = lax.rem(i + 2, MOE_BUFFERS)

    @pl.when(i == 0)
    def _():
        for blk in range(2):
            _row_gather_start(xn_hbm, tok_ref, blk * MOE_ROWS, 1, xbuf.at[blk], sem.at[blk], MOE_ROWS)

    @pl.when(i < nu)
    def _():
        _row_gather_wait(xn_hbm, xbuf.at[cur], sem.at[cur], MOE_ROWS)
        x = _bf(xbuf[cur])
        hid = jax.nn.silu(_dot(x, _bf(wg_ref[0, 0]))) * _dot(x, _bf(wu_ref[0, 0]))
        o_ref[...] = _dot(_bf(hid), _bf(wd_ref[0, 0]))
        ahead = jnp.minimum(i + 2, last)
        _row_gather_start(xn_hbm, tok_ref, ahead * MOE_ROWS, 1, xbuf.at[nxt2], sem.at[nxt2], MOE_ROWS,
                          queues=(1,))

        @pl.when(i == nu - 1)
        def _():
            _row_gather_wait(xn_hbm, xbuf.at[nxt1], sem.at[nxt1], MOE_ROWS)
            _row_gather_wait(xn_hbm, xbuf.at[nxt2], sem.at[nxt2], MOE_ROWS)

    @pl.when(i >= nu)
    def _():
        o_ref[...] = jnp.zeros_like(o_ref)


def _experts(xn, slot_tok, block_e, n_used, layer, e_gate, e_up, e_down):
    T, D = xn.shape
    n_blocks = slot_tok.shape[0] // MOE_ROWS
    wspec = lambda shp: pl.BlockSpec((1, 1) + shp, lambda i, be, nu, tok: (layer, be[i], 0, 0))
    return pl.pallas_call(
        _expert_kernel,
        grid_spec=pltpu.PrefetchScalarGridSpec(
            num_scalar_prefetch=3,
            grid=(n_blocks,),
            in_specs=[pl.BlockSpec(memory_space=pl.ANY), wspec((D, D_EXPERT)), wspec((D, D_EXPERT)),
                      wspec((D_EXPERT, D))],
            out_specs=pl.BlockSpec((MOE_ROWS, D), lambda i, be, nu, tok: (i, 0)),
            scratch_shapes=[pltpu.VMEM((MOE_BUFFERS, MOE_ROWS, D), F32), pltpu.SemaphoreType.DMA((MOE_BUFFERS,))],
        ),
        out_shape=jax.ShapeDtypeStruct((n_blocks * MOE_ROWS, D), F32),
        compiler_params=_params(("arbitrary",)),
        name="moe_experts",
    )(block_e, n_used, slot_tok, xn, e_gate, e_up, e_down)


def _combine_kernel(dest_ref, x_ref, route_ref, ys_hbm, o_ref, buf, sem, *, tm):
    i = pl.program_id(0)
    cur = i & 1

    def start(tile, par):
        for k in range(2):
            _row_gather_start(ys_hbm, dest_ref, tile * (2 * tm) + k, 2, buf.at[par, k], sem.at[par], tm)

    @pl.when(i == 0)
    def _():
        start(0, 0)

    @pl.when(i + 1 < pl.num_programs(0))
    def _():
        start(i + 1, 1 - cur)

    for k in range(2):
        _row_gather_wait(ys_hbm, buf.at[cur, k], sem.at[cur], tm)
    route = route_ref[...]
    o_ref[...] = x_ref[...] + route[:, 2:3] * buf[cur, 0] + route[:, 3:4] * buf[cur, 1]


def _combine(x2d, y_slots, dest, route):
    T, D = x2d.shape
    tm = min(256, T)
    row = lambda n: pl.BlockSpec((tm, n), lambda i, dest: (i, 0))
    return pl.pallas_call(
        functools.partial(_combine_kernel, tm=tm),
        grid_spec=pltpu.PrefetchScalarGridSpec(
            num_scalar_prefetch=1,
            grid=(T // tm,),
            in_specs=[row(D), row(LANES), pl.BlockSpec(memory_space=pl.ANY)],
            out_specs=row(D),
            scratch_shapes=[pltpu.VMEM((2, 2, tm, D), F32), pltpu.SemaphoreType.DMA((2,))],
        ),
        out_shape=jax.ShapeDtypeStruct((T, D), F32),
        compiler_params=_params(("arbitrary",)),
        name="moe_combine",
    )(dest, x2d, route, y_slots)


def _moe_layer(x, gain, wg, bg, we, be, layer, e_gate, e_up, e_down):
    B, S, D = x.shape
    T = B * S
    x2d = x.reshape(T, D)
    xn, route = _router(x2d, gain, wg, bg, we, be)
    flat_e = route[:, 0:2].astype(jnp.int32).reshape(-1)
    n_pairs = 2 * T
    onehot = (flat_e[:, None] == jnp.arange(N_EXPERTS, dtype=jnp.int32)[None, :]).astype(jnp.int32)
    csum = jnp.cumsum(onehot, axis=0)
    rank = jnp.take_along_axis(csum, flat_e[:, None], axis=1)[:, 0] - 1
    counts = csum[-1]
    padded = (counts + MOE_ROWS - 1) // MOE_ROWS * MOE_ROWS
    pad_end = jnp.cumsum(padded)
    dest = (pad_end - padded)[flat_e] + rank
    n_blocks = -(-n_pairs // MOE_ROWS) + N_EXPERTS
    dest = dest.astype(jnp.int32)
    slot_tok = _slot_tokens(dest, n_blocks * MOE_ROWS)
    block_start = jnp.arange(n_blocks, dtype=jnp.int32) * MOE_ROWS
    block_e = jnp.minimum(jnp.sum((pad_end[None, :] <= block_start[:, None]).astype(jnp.int32), axis=1),
                          N_EXPERTS - 1)
    n_used = (pad_end[-1:] // MOE_ROWS).astype(jnp.int32)
    y_slots = _experts(xn, slot_tok, block_e, n_used, layer, e_gate, e_up, e_down)
    return _combine(x2d, y_slots, dest, route).reshape(B, S, D)


def kernel(x, mix_norm, attn_w_in, attn_gate_bias, diff_q_norm, diff_k_norm, diff_lambda, diff_subln, nsa_q_norm, nsa_k_norm, nsa_cmp_pe, nsa_cmp_w1, nsa_cmp_w2, attn_w_out, rwkv_mix, rwkv_w_r, rwkv_w_k, rwkv_w_v, rwkv_decay_w0, rwkv_decay_w1, rwkv_decay_w2, rwkv_iclr_a0, rwkv_iclr_a1, rwkv_iclr_a2, rwkv_gate_g1, rwkv_gate_g2, rwkv_k_k, rwkv_k_a, rwkv_r_k, rwkv_ln_w, rwkv_ln_b, rwkv_w_o, ffn_norm, router_group_w, router_group_b, router_expert_w, router_expert_b, expert_w_gate, expert_w_up, expert_w_down):
    depth = mix_norm.shape[0]
    for layer in range(depth):
        i = layer // 2
        if layer % 2 == 0:
            x = _hybrid_attention_layer(
                x, layer, mix_norm[layer], attn_w_in[i], attn_gate_bias[i], diff_q_norm[i], diff_k_norm[i],
                diff_lambda[i], diff_subln[i], nsa_q_norm[i], nsa_k_norm[i], nsa_cmp_pe[i], nsa_cmp_w1[i],
                nsa_cmp_w2[i], attn_w_out[i])
        else:
            x = _rwkv_layer(
                x, mix_norm[layer], rwkv_mix[i], rwkv_w_r[i], rwkv_w_k[i], rwkv_w_v[i], rwkv_decay_w0[i],
                rwkv_decay_w1[i], rwkv_decay_w2[i], rwkv_iclr_a0[i], rwkv_iclr_a1[i], rwkv_iclr_a2[i],
                rwkv_gate_g1[i], rwkv_gate_g2[i], rwkv_k_k[i], rwkv_k_a[i], rwkv_r_k[i], rwkv_ln_w[i],
                rwkv_ln_b[i], rwkv_w_o[i])
        x = _moe_layer(x, ffn_norm[layer], router_group_w[layer], router_group_b[layer], router_expert_w[layer],
                       router_expert_b[layer], layer, expert_w_gate, expert_w_up, expert_w_down)
    return x
```

```python
import functools
import math

import jax
import jax.numpy as jnp
from jax import lax
from jax.experimental import pallas as pl
from jax.experimental.pallas import tpu as pltpu

F32 = jnp.float32
BF16 = jnp.bfloat16

D_MODEL = 1024
HEAD_DIM = 64
NORM_EPS = 1e-6
NEG_INF = -1e30
FORCE_SCORE = 1e4

DIFF_HEADS = 8
DIFF_QK_DIM = 32
NSA_HEADS = 8
NSA_KV_GROUPS = 2
NSA_HPG = 4
NSA_CMP_BLOCK = 32
NSA_CMP_STRIDE = 16
NSA_SEL_BLOCK = 64
NSA_TOP_N = 16
NSA_WINDOW = 512
IN_COLS = 2840
IN_COLS_PAD = 2944
GATE_COLS = 24

RWKV_HEAD = 64
RWKV_LNX_EPS = 64e-5
RWKV_CHUNK = 64

N_GROUPS = 4
EXPERTS_PER_GROUP = 8
N_EXPERTS = 32
D_EXPERT = 256
MOE_ROWS = 256
MOE_BUFFERS = 3

LANES = 128
MXU_TILE = 256
VMEM_LIMIT = 56 * 1024 * 1024


def _bf(x):
    return x.astype(BF16)


def _dot(a, b):
    return jnp.dot(a, b, preferred_element_type=F32)


def _dot_nt(a, b):
    return lax.dot_general(a, b, (((1,), (1,)), ((), ())), preferred_element_type=F32)


def _dot_tn(a, b):
    return lax.dot_general(a, b, (((0,), (0,)), ((), ())), preferred_element_type=F32)


def _split2(x):
    hi = _bf(x)
    lo = _bf(x - hi.astype(F32))
    return hi, lo


def _params(sem):
    return pltpu.CompilerParams(dimension_semantics=sem, vmem_limit_bytes=VMEM_LIMIT)


def _block_diag(n, group):
    r = jnp.arange(n) // group
    return ((r[:, None] == r[None, :]).astype(F32) / group).astype(BF16)


def _rms(x, gain):
    return x * lax.rsqrt(jnp.mean(x * x, axis=-1, keepdims=True) + NORM_EPS) * gain


def _inproj_kernel(x_ref, g_ref, wa_ref, wbt_ref, b_ref, bd32_ref, bd64_ref, gk_ref, gnk_ref, gq_ref, gnq_ref,
                   qd_ref, kd_ref, vd_ref, qn_ref, cmp_ref, ks_ref, vs_ref, kw_ref, vw_ref, gt_ref):
    xn = _bf(_rms(x_ref[0], g_ref[...]))
    ha = _dot(xn, wa_ref[...])
    hb = _dot_nt(wbt_ref[...], xn)
    bd32 = bd32_ref[...]
    bd64 = bd64_ref[...]

    def gnorm(seg, bd, gain):
        hi, lo = _split2(seg * seg)
        return seg * lax.rsqrt(_dot(hi, bd) + _dot(lo, bd) + NORM_EPS) * gain

    def gnorm_t(seg, bd, gain):
        hi, lo = _split2(seg * seg)
        return seg * lax.rsqrt(_dot(bd, hi) + _dot(bd, lo) + NORM_EPS) * gain

    kd_ref[0] = _bf(gnorm(ha[:, 0:512], bd32, gk_ref[...]))
    cmp_ref[0, 0] = ha[:, 512:640]
    cmp_ref[0, 1] = ha[:, 640:768]
    ksw = gnorm(ha[:, 768:1024], bd64[0:256, 0:256], gnk_ref[...])
    ks_ref[0] = _bf(ksw[:, 0:128])
    kw_ref[0] = _bf(ksw[:, 128:256])
    gt_ref[0] = jax.nn.sigmoid(ha[:, 1024:1152] + b_ref[...])
    qd_ref[0, 0] = _bf(gnorm_t(hb[0:512], bd32, gq_ref[...]))
    vd_ref[0, 0] = _bf(hb[512:1024])
    qn_ref[0, 0] = _bf(gnorm_t(hb[1024:1536], bd64, gnq_ref[...]))
    vs_ref[0, 0] = _bf(hb[1536:1664])
    vw_ref[0, 0] = _bf(hb[1664:1792])


def _attn_inproj(x, gain, w_in, gate_bias, dq_gain, dk_gain, nq_gain, nk_gain, tm):
    B, S, D = x.shape
    n = S // tm
    c = lambda lo, hi: w_in[:, lo:hi]
    gate_w = jnp.pad(c(2816, IN_COLS), ((0, 0), (0, LANES - GATE_COLS)))
    wa = _bf(jnp.concatenate([c(512, 1024), c(2048, 2304), c(2304, 2432), c(2560, 2688), gate_w], axis=1))
    wbt = _bf(jnp.concatenate([c(0, 512), c(1024, 1536), c(1536, 2048), c(2432, 2560), c(2688, 2816)], axis=1).T)
    bias = jnp.pad(gate_bias, (0, LANES - GATE_COLS)).reshape(1, LANES)
    log2e = math.log2(math.e)
    gk = jnp.tile(dk_gain, 16).reshape(1, 512)
    gnk = jnp.concatenate([jnp.tile(nk_gain[1], 2), jnp.tile(nk_gain[2], 2)]).reshape(1, 256)
    gq = (jnp.tile(dq_gain, 16) * (DIFF_QK_DIM ** -0.5 * log2e)).reshape(512, 1)
    gnq = (jnp.tile(nq_gain, 8) * (HEAD_DIM ** -0.5 * log2e)).reshape(512, 1)
    consts = (gain.reshape(1, D), wa, wbt, bias, _block_diag(512, 32), _block_diag(512, 64), gk, gnk, gq, gnq)
    full = lambda a: pl.BlockSpec(a.shape, lambda b, i: (0,) * a.ndim)
    tok = lambda w: pl.BlockSpec((1, tm, w), lambda b, i: (b, i, 0))
    tr = lambda r: pl.BlockSpec((1, 1, r, tm), lambda b, i: (b, i, 0, 0))
    tok_shape = lambda w, dt: jax.ShapeDtypeStruct((B, S, w), dt)
    tr_shape = lambda r: jax.ShapeDtypeStruct((B, n, r, tm), BF16)
    return pl.pallas_call(
        _inproj_kernel,
        grid=(B, n),
        in_specs=[tok(D)] + [full(a) for a in consts],
        out_specs=[tr(512), tok(512), tr(512), tr(512), pl.BlockSpec((1, 2, tm, LANES), lambda b, i: (b, 0, i, 0)),
                   tok(LANES), tr(LANES), tok(LANES), tr(LANES), tok(LANES)],
        out_shape=[tr_shape(512), tok_shape(512, BF16), tr_shape(512), tr_shape(512),
                   jax.ShapeDtypeStruct((B, 2, S, LANES), F32),
                   tok_shape(LANES, BF16), tr_shape(LANES), tok_shape(LANES, BF16), tr_shape(LANES),
                   tok_shape(LANES, F32)],
        compiler_params=_params(("arbitrary", "arbitrary")),
        name="attn_inproj",
    )(x, *consts)


SHIFT_LIMIT = 40.0


def _score_bound(q_gain, k_gain, dim):
    bound = dim * jnp.max(jnp.abs(q_gain)) * jnp.max(jnp.abs(k_gain)) * 1.02 + 0.25
    return jnp.stack([bound, (bound <= SHIFT_LIMIT).astype(F32)]).astype(F32)


def _flash_pass(lo, hi, scores, values, mask_body, mask_last, s_ref, p_ref, shift):
    tk, cols = s_ref.shape
    dv = HEAD_DIM
    p_ref[...] = jnp.zeros_like(p_ref)
    s_ref[...] = scores(lo)

    def process(j, carry, mask, s_next):
        s = s_ref[...]
        if mask is not None:
            s = mask(j, s)
        pv = values(jnp.maximum(j - 1, lo), p_ref[...])
        if shift is None:
            m, l, acc = carry
            m_new = jnp.maximum(m, jnp.max(s, axis=0, keepdims=True))
            alpha = jnp.exp2(m - m_new)
            p = jnp.exp2(s - m_new)
            carry = (m_new, alpha * l + jnp.sum(p, axis=0, keepdims=True), (acc + pv) * alpha)
        else:
            l, acc = carry
            p = jnp.exp2(s - shift)
            carry = (l + jnp.sum(p, axis=0, keepdims=True), acc + pv)
        p_ref[...] = _bf(p)
        if s_next is not None:
            s_ref[...] = s_next
        return carry

    init = (jnp.zeros((1, cols), F32), jnp.zeros((dv, cols), F32))
    if shift is None:
        init = (jnp.full((1, cols), NEG_INF, F32),) + init
    carry = lax.fori_loop(lo, hi, lambda j, c: process(j, c, mask_body, scores(j + 1)), init)
    l, acc = process(hi, carry, mask_last, None)[-2:]
    return (acc + values(hi, p_ref[...])) / l


def _flash_tiles(lo, hi, scores, values, mask_body, mask_last, s_ref, p_ref, o_ref, bound_ref):
    @pl.when(bound_ref[1] > 0.5)
    def _():
        o_ref[...] = _flash_pass(lo, hi, scores, values, mask_body, mask_last, s_ref, p_ref, bound_ref[0])

    @pl.when(bound_ref[1] <= 0.5)
    def _():
        o_ref[...] = _flash_pass(lo, hi, scores, values, mask_body, mask_last, s_ref, p_ref, None)

    return o_ref[...]


def _diff_attn_kernel(qt_ref, k_ref, vt_ref, lam_ref, sg_ref, bound_ref, o_ref, s_ref, p_ref, acc_ref, *,
                      tq, lambda_init):
    i = pl.program_id(2)
    q = qt_ref[0, 0]
    zero = jnp.zeros((DIFF_QK_DIM, tq), BF16)
    qb = jnp.concatenate(
        [jnp.concatenate([q[r * DIFF_QK_DIM:(r + 1) * DIFF_QK_DIM] if c == r else zero for c in range(4)], axis=1)
         for r in range(4)], axis=0)

    def causal(j, s):
        kpos = lax.broadcasted_iota(jnp.int32, (tq, 4 * tq), 0)
        qpos = lax.broadcasted_iota(jnp.int32, (tq, 4 * tq), 1) & (tq - 1)
        return jnp.where(kpos <= qpos, s, NEG_INF)

    def values(j, p):
        vt = vt_ref[0, j]
        return jnp.concatenate([_dot(vt[0:HEAD_DIM], p[:, 0:2 * tq]),
                                _dot(vt[HEAD_DIM:2 * HEAD_DIM], p[:, 2 * tq:4 * tq])], axis=1)

    o = _flash_tiles(0, i, lambda j: _dot(k_ref[0, j], qb), values, None, causal, s_ref, p_ref, acc_ref,
                     bound_ref)
    lp = lam_ref[...]
    lam = (jnp.exp(jnp.sum(lp[0:1] * lp[1:2], axis=-1, keepdims=True))
           - jnp.exp(jnp.sum(lp[2:3] * lp[3:4], axis=-1, keepdims=True)) + lambda_init)
    heads = []
    for h in range(2):
        oh = o[:, 2 * h * tq:(2 * h + 1) * tq] - lam * o[:, (2 * h + 1) * tq:(2 * h + 2) * tq]
        oh = oh * lax.rsqrt(jnp.mean(oh * oh, axis=0, keepdims=True) + NORM_EPS) * sg_ref[...]
        heads.append(oh * (1.0 - lambda_init))
    o_ref[0] = jnp.concatenate(heads, axis=0).T


def _diff_attention(qd_t, kd, vd_t, lam_p, subln, bound, lambda_init, tq):
    B, n = qd_t.shape[:2]
    S = n * tq
    pairs = DIFF_HEADS // 2
    return pl.pallas_call(
        functools.partial(_diff_attn_kernel, tq=tq, lambda_init=lambda_init),
        grid=(B, pairs, n),
        in_specs=[
            pl.BlockSpec((1, 1, LANES, tq), lambda b, h, i: (b, i, h, 0)),
            pl.BlockSpec((1, n, tq, LANES), lambda b, h, i: (b, 0, 0, h)),
            pl.BlockSpec((1, n, LANES, tq), lambda b, h, i: (b, 0, h, 0)),
            pl.BlockSpec((4, DIFF_QK_DIM), lambda b, h, i: (0, 0)),
            pl.BlockSpec((HEAD_DIM, 1), lambda b, h, i: (0, 0)),
            pl.BlockSpec(memory_space=pltpu.SMEM),
        ],
        out_specs=pl.BlockSpec((1, tq, LANES), lambda b, h, i: (b, i, h)),
        out_shape=jax.ShapeDtypeStruct((B, S, DIFF_HEADS * HEAD_DIM), F32),
        scratch_shapes=[pltpu.VMEM((tq, 4 * tq), F32), pltpu.VMEM((tq, 4 * tq), BF16),
                        pltpu.VMEM((HEAD_DIM, 4 * tq), F32)],
        compiler_params=_params(("arbitrary", "arbitrary", "arbitrary")),
        name="diff_attention",
    )(qd_t, kd.reshape(B, n, tq, DIFF_HEADS * HEAD_DIM), vd_t, lam_p, subln.reshape(HEAD_DIM, 1), bound)


def _nsa_compress_kernel(t_ref, pe_ref, w1_ref, w2_ref, bd_ref, kg_ref, kc_ref, vct_ref, *, nch):
    half = NSA_CMP_BLOCK // 2
    ya = jnp.zeros((nch, 2 * LANES), F32)
    yb = jnp.zeros((nch, 2 * LANES), F32)
    for l in range(half):
        rows = pl.ds(l, nch, stride=NSA_CMP_STRIDE)
        piece = jnp.concatenate([t_ref[0, 0, rows, :], t_ref[0, 1, rows, :]], axis=1)
        ya = ya + _dot(_bf(piece + pe_ref[l:l + 1]), w1_ref[l])
        yb = yb + _dot(_bf(piece + pe_ref[half + l:half + l + 1]), w1_ref[half + l])
    pre = ya + pltpu.roll(yb, nch - 1, 0)
    out = _dot(_bf(jax.nn.gelu(pre)), w2_ref[...])
    k = out[:, 0:LANES]
    hi, lo = _split2(k * k)
    ms = _dot(hi, bd_ref[...]) + _dot(lo, bd_ref[...])
    kc_ref[0] = _bf(k * lax.rsqrt(ms + NORM_EPS) * kg_ref[...])
    vct_ref[0] = _bf(out[:, LANES:2 * LANES].T)


def _nsa_compress(cmp_kv, cmp_pe, cmp_w1, cmp_w2, k_gain0):
    B, _, S, _ = cmp_kv.shape
    nch = S // NSA_CMP_STRIDE

    def over_groups(w):
        parts = [w[0], w[0], w[1], w[1]]
        zero = jnp.zeros_like(w[0])
        return jnp.concatenate(
            [jnp.concatenate([parts[r] if c == r else zero for c in range(4)], axis=-1) for r in range(4)], axis=-2)

    w1 = _bf(over_groups(cmp_w1.reshape(2, NSA_CMP_BLOCK, HEAD_DIM, HEAD_DIM)))
    w2 = _bf(over_groups(cmp_w2))
    pe = jnp.concatenate([cmp_pe[0], cmp_pe[0], cmp_pe[1], cmp_pe[1]], axis=-1)
    full = lambda a: pl.BlockSpec(a.shape, lambda b: (0,) * a.ndim)
    consts = (pe, w1, w2, _block_diag(LANES, HEAD_DIM), jnp.tile(k_gain0, 2).reshape(1, LANES))
    return pl.pallas_call(
        functools.partial(_nsa_compress_kernel, nch=nch),
        grid=(B,),
        in_specs=[pl.BlockSpec((1, 2, S, LANES), lambda b: (b, 0, 0, 0))] + [full(a) for a in consts],
        out_specs=[pl.BlockSpec((1, nch, LANES), lambda b: (b, 0, 0)), pl.BlockSpec((1, LANES, nch), lambda b: (b, 0, 0))],
        out_shape=[jax.ShapeDtypeStruct((B, nch, LANES), BF16), jax.ShapeDtypeStruct((B, LANES, nch), BF16)],
        compiler_params=_params(("arbitrary",)),
        name="nsa_compress",
    )(cmp_kv, *consts)


def _group_queries(q_ref, g):
    q = q_ref[0, 0]
    qg = jnp.concatenate([q[h * HEAD_DIM:(h + 1) * HEAD_DIM] for h in range(NSA_HPG)], axis=1)
    zero = jnp.zeros_like(qg)
    return jnp.where(g == 0, jnp.concatenate([qg, zero], axis=0), jnp.concatenate([zero, qg], axis=0))


def _store_heads(o_ref, o, tq):
    o_ref[0] = jnp.concatenate([o[:, h * tq:(h + 1) * tq] for h in range(NSA_HPG)], axis=0).T


def _nsa_cmp_attn_kernel(q_ref, k_ref, vt_ref, o_ref, sel_ref, *, tq, nch, n_sel, top_n):
    i = pl.program_id(2)
    cols = NSA_HPG * tq
    n_cmp = nch - 1
    s = _dot(k_ref[0], _group_queries(q_ref, pl.program_id(1)))
    pos = i * tq + (lax.broadcasted_iota(jnp.int32, (nch, cols), 1) & (tq - 1))
    c = lax.broadcasted_iota(jnp.int32, (nch, cols), 0)
    ok = jnp.where(c < n_cmp, c * NSA_CMP_STRIDE + (NSA_CMP_BLOCK - 1), 1 << 30) <= pos
    s = jnp.where(ok, s, NEG_INF)
    p = jnp.where(ok, jnp.exp2(s - jnp.max(s, axis=0, keepdims=True)), 0.0)
    l = jnp.sum(p, axis=0, keepdims=True)
    p = p / jnp.where(l > 0.0, l, 1.0)
    _store_heads(o_ref, _dot(vt_ref[0], _bf(p)), tq)

    pg = p[:, 0:tq]
    for h in range(1, NSA_HPG):
        pg = pg + p[:, h * tq:(h + 1) * tq]
    jj = lax.broadcasted_iota(jnp.int32, (n_sel, nch), 0) * NSA_SEL_BLOCK
    cc = lax.broadcasted_iota(jnp.int32, (n_sel, nch), 1) * NSA_CMP_STRIDE
    cover = jnp.where(cc < jj + NSA_SEL_BLOCK,
                      jnp.where(cc + NSA_CMP_BLOCK > jj, jnp.where(cc < n_cmp * NSA_CMP_STRIDE, 1.0, 0.0), 0.0), 0.0)
    cover = _bf(cover)
    hi, lo = _split2(pg)
    imp = _dot(cover, hi) + _dot(cover, lo)
    blk = lax.broadcasted_iota(jnp.int32, (n_sel, tq), 0)
    cur = (i * tq + lax.broadcasted_iota(jnp.int32, (n_sel, tq), 1)) >> 6
    imp = jnp.where(blk == cur, FORCE_SCORE, jnp.where(blk == 0, FORCE_SCORE, jnp.where(blk > cur, NEG_INF, imp)))
    blk_f = blk.astype(F32)
    sel = jnp.zeros((n_sel, tq), F32)
    for _ in range(top_n):
        mx = jnp.max(imp, axis=0, keepdims=True)
        first = jnp.min(jnp.where(imp == mx, blk_f, float(n_sel)), axis=0, keepdims=True)
        hit = blk_f == first
        sel = jnp.where(hit, 1.0, sel)
        imp = jnp.where(hit, -jnp.inf, imp)
    sel_ref[0, 0] = _bf(sel)


def _nsa_cmp_attention(qn_t, kc, vc_t, tq):
    B, n = qn_t.shape[:2]
    S = n * tq
    G = NSA_KV_GROUPS
    nch = S // NSA_CMP_STRIDE
    n_sel = S // NSA_SEL_BLOCK
    top_n = min(NSA_TOP_N, n_sel)
    gcols = NSA_HPG * HEAD_DIM
    return pl.pallas_call(
        functools.partial(_nsa_cmp_attn_kernel, tq=tq, nch=nch, n_sel=n_sel, top_n=top_n),
        grid=(B, G, n),
        in_specs=[
            pl.BlockSpec((1, 1, gcols, tq), lambda b, g, i: (b, i, g, 0)),
            pl.BlockSpec((1, nch, LANES), lambda b, g, i: (b, 0, 0)),
            pl.BlockSpec((1, HEAD_DIM, nch), lambda b, g, i: (b, g, 0)),
        ],
        out_specs=[pl.BlockSpec((1, tq, gcols), lambda b, g, i: (b, i, g)),
                   pl.BlockSpec((1, 1, n_sel, tq), lambda b, g, i: (b, g, 0, i))],
        out_shape=[jax.ShapeDtypeStruct((B, S, G * gcols), F32),
                   jax.ShapeDtypeStruct((B, G, n_sel, S), BF16)],
        compiler_params=_params(("arbitrary", "arbitrary", "arbitrary")),
        name="nsa_cmp_attention",
    )(qn_t, kc, vc_t)


def _nsa_sel_win_kernel(q_ref, ks_ref, vst_ref, kw_ref, vwt_ref, sel_ref, bs_ref, bw_ref, os_ref, ow_ref,
                        s_ref, p_ref, acc_ref, *, tq, n_sel):
    i = pl.program_id(2)
    qt = _group_queries(q_ref, pl.program_id(1))
    selm = sel_ref[0, 0]
    kloc = lax.broadcasted_iota(jnp.int32, (tq, tq), 0)
    qpos = i * tq + lax.broadcasted_iota(jnp.int32, (tq, tq), 1)
    blk_row = lax.broadcasted_iota(jnp.int32, (tq, n_sel), 0)
    blk_col = lax.broadcasted_iota(jnp.int32, (tq, n_sel), 1)

    def masked(s, keep):
        return jnp.where(jnp.concatenate([keep] * NSA_HPG, axis=1) > 0.5, s, NEG_INF)

    def sel_mask(j, s):
        expand = _bf(jnp.where(blk_col == ((j * tq + blk_row) >> 6), 1.0, 0.0))
        chosen = _dot(expand, selm)
        return masked(s, jnp.where(j * tq + kloc <= qpos, chosen, 0.0))

    o = _flash_tiles(0, i, lambda j: _dot(ks_ref[0, j], qt), lambda j, p: _dot(vst_ref[0, j], p),
                     sel_mask, sel_mask, s_ref, p_ref, acc_ref, bs_ref)
    _store_heads(os_ref, o, tq)

    def win_mask(j, s):
        dist = qpos - (j * tq + kloc)
        return masked(s, jnp.where(dist >= 0, jnp.where(dist < NSA_WINDOW, 1.0, 0.0), 0.0))

    first = jnp.maximum(i - (NSA_WINDOW - 1 + tq - 1) // tq, 0)
    o = _flash_tiles(first, i, lambda j: _dot(kw_ref[0, j], qt), lambda j, p: _dot(vwt_ref[0, j], p),
                     win_mask, win_mask, s_ref, p_ref, acc_ref, bw_ref)
    _store_heads(ow_ref, o, tq)


def _nsa_sel_win(qn_t, ks, vs_t, kw, vw_t, sel, bound_sel, bound_win, tq):
    B, n = qn_t.shape[:2]
    S = n * tq
    G = NSA_KV_GROUPS
    n_sel = S // NSA_SEL_BLOCK
    gcols = NSA_HPG * HEAD_DIM
    kspec = pl.BlockSpec((1, n, tq, LANES), lambda b, g, i: (b, 0, 0, 0))
    vtspec = pl.BlockSpec((1, n, HEAD_DIM, tq), lambda b, g, i: (b, 0, g, 0))
    ospec = pl.BlockSpec((1, tq, gcols), lambda b, g, i: (b, i, g))
    return pl.pallas_call(
        functools.partial(_nsa_sel_win_kernel, tq=tq, n_sel=n_sel),
        grid=(B, G, n),
        in_specs=[pl.BlockSpec((1, 1, gcols, tq), lambda b, g, i: (b, i, g, 0)), kspec, vtspec, kspec, vtspec,
                  pl.BlockSpec((1, 1, n_sel, tq), lambda b, g, i: (b, g, 0, i)),
                  pl.BlockSpec(memory_space=pltpu.SMEM), pl.BlockSpec(memory_space=pltpu.SMEM)],
        out_specs=[ospec, ospec],
        out_shape=[jax.ShapeDtypeStruct((B, S, G * gcols), F32)] * 2,
        scratch_shapes=[pltpu.VMEM((tq, NSA_HPG * tq), F32), pltpu.VMEM((tq, NSA_HPG * tq), BF16),
                        pltpu.VMEM((HEAD_DIM, NSA_HPG * tq), F32)],
        compiler_params=_params(("arbitrary", "arbitrary", "arbitrary")),
        name="nsa_sel_win_attention",
    )(qn_t, ks.reshape(B, n, tq, LANES), vs_t, kw.reshape(B, n, tq, LANES), vw_t, sel, bound_sel, bound_win)


def _attn_outproj_kernel(x_ref, d_ref, oc_ref, os_ref, ow_ref, gt_ref, ge_ref, w_ref, o_ref):
    ghi, glo = _split2(gt_ref[...])

    def gate(r):
        return _dot(ghi, ge_ref[r]) + _dot(glo, ge_ref[r])

    nsa = gate(0) * oc_ref[...] + gate(1) * os_ref[...] + gate(2) * ow_ref[...]
    y = _dot(_bf(d_ref[...]), w_ref[0:512, :]) + _dot(_bf(nsa), w_ref[512:1024, :])
    o_ref[...] = x_ref[...] + y


def _attn_outproj(x2d, diff_o, o_cmp, o_sel, o_win, gates, w_out):
    T, D = x2d.shape
    tm = min(256, T)
    col = jnp.arange(512) // HEAD_DIM
    src = jnp.arange(LANES)
    ge = jnp.stack([_bf((src[:, None] == col[None, :] * 3 + r).astype(F32)) for r in range(3)])
    row = lambda n: pl.BlockSpec((tm, n), lambda i: (i, 0))
    return pl.pallas_call(
        _attn_outproj_kernel,
        grid=(T // tm,),
        in_specs=[row(D), row(512), row(512), row(512), row(512), row(LANES),
                  pl.BlockSpec((3, LANES, 512), lambda i: (0, 0, 0)),
                  pl.BlockSpec((D, D), lambda i: (0, 0))],
        out_specs=row(D),
        out_shape=jax.ShapeDtypeStruct((T, D), F32),
        compiler_params=_params(("arbitrary",)),
        name="attn_outproj",
    )(x2d, diff_o, o_cmp, o_sel, o_win, gates, ge, _bf(w_out))


def _hybrid_attention_layer(x, layer, norm_gain, w_in, gate_bias, dq_gain, dk_gain, lam_p, subln, nq_gain, nk_gain,
                            cmp_pe, cmp_w1, cmp_w2, w_out):
    B, S, D = x.shape
    T = B * S
    tq = min(256, S)
    qd_t, kd, vd_t, qn_t, cmp_kv, ks, vs_t, kw, vw_t, gates = _attn_inproj(
        x, norm_gain, w_in, gate_bias, dq_gain, dk_gain, nq_gain, nk_gain, tq)
    lambda_init = 0.8 - 0.6 * math.exp(-0.3 * layer)
    log2e = math.log2(math.e)
    dq_scaled = dq_gain * (DIFF_QK_DIM ** -0.5 * log2e)
    nq_scaled = nq_gain * (HEAD_DIM ** -0.5 * log2e)
    diff_o = _diff_attention(qd_t, kd, vd_t, lam_p, subln, _score_bound(dq_scaled, dk_gain, DIFF_QK_DIM),
                             lambda_init, tq)
    kc, vc_t = _nsa_compress(cmp_kv, cmp_pe, cmp_w1, cmp_w2, nk_gain[0])
    o_cmp, sel = _nsa_cmp_attention(qn_t, kc, vc_t, tq)
    o_sel, o_win = _nsa_sel_win(qn_t, ks, vs_t, kw, vw_t, sel, _score_bound(nq_scaled, nk_gain[1], HEAD_DIM),
                                _score_bound(nq_scaled, nk_gain[2], HEAD_DIM), tq)
    flat = lambda a: a.reshape(T, a.shape[-1])
    out = _attn_outproj(flat(x), flat(diff_o), flat(o_cmp), flat(o_sel), flat(o_win), flat(gates), w_out)
    return out.reshape(B, S, D)


def _softplus(z):
    return jnp.maximum(z, 0.0) + jnp.log(1.0 + jnp.exp(-jnp.abs(z)))


def _head_pool(d, head):
    member = (jnp.arange(d)[:, None] // head == jnp.arange(LANES)[None, :]).astype(BF16)
    return member, member.T


def _head_sum(parts, pool, expand):
    sums = _dot(parts[0], pool)
    for part in parts[1:]:
        sums = sums + _dot(part, pool)
    hi, lo = _split2(sums)
    return _dot(hi, expand) + _dot(lo, expand)


def _rwkv_proj_kernel(x_ref, xp_ref, g_ref, mix_ref, wr_ref, wk_ref, wv_ref, w1_ref, w2_ref, a1_ref, a2_ref,
                      g1_ref, g2_ref, vec_ref, pool_ref, expand_ref,
                      r_ref, wl_ref, k_ref, v_ref, kk_ref, b_ref, bonus_ref, gate_ref):
    i = pl.program_id(1)
    gain = g_ref[...]
    xn = _rms(x_ref[0], gain)
    prev = _rms(xp_ref[0], gain)[7:8] * jnp.where(i > 0, 1.0, 0.0)
    shifted = pltpu.roll(xn, 1, 0)
    first_row = lax.broadcasted_iota(jnp.int32, xn.shape, 0) == 0
    dx = jnp.where(first_row, prev, shifted) - xn
    mix = mix_ref[...]
    xr, xw, xk, xv, xa, xg = (_bf(xn + dx * mix[j:j + 1]) for j in range(6))
    vec = vec_ref[...]
    w0, a0, k_k, k_a, r_k = (vec[j:j + 1] for j in range(5))
    r = _dot(xr, wr_ref[...])
    k = _dot(xk, wk_ref[...])
    v = _dot(xv, wv_ref[...])
    w = -_softplus(-(w0 + _dot(_bf(jnp.tanh(_dot(xw, w1_ref[...]))), w2_ref[...]))) - 0.5
    a = jax.nn.sigmoid(a0 + _dot(_bf(_dot(xa, a1_ref[...])), a2_ref[...]))
    gate_ref[0] = _dot(_bf(jax.nn.sigmoid(_dot(xg, g1_ref[...]))), g2_ref[...])
    pool = pool_ref[...]
    expand = expand_ref[...]
    kk = k * k_k
    kk = kk / jnp.maximum(jnp.sqrt(_head_sum([_bf(kk * kk)], pool, expand)), 1e-12)
    k = k * (1.0 + (a - 1.0) * k_a)
    r_ref[0] = r
    wl_ref[0] = -jnp.exp(w)
    k_ref[0] = k
    v_ref[0] = v
    kk_ref[0] = kk
    b_ref[0] = kk * a
    bonus_ref[0] = _head_sum([_bf(r * k * r_k)], pool, expand) * v


def _rwkv_proj(x, gain, mix, w_r, w_k, w_v, w0, w1, w2, a0, a1, a2, g1, g2, k_k, k_a, r_k):
    B, S, D = x.shape
    tm = min(256, S)
    pad_c = lambda m, n: _bf(jnp.pad(m, ((0, 0), (0, n - m.shape[1]))))
    pad_r = lambda m, n: _bf(jnp.pad(m, ((0, n - m.shape[0]), (0, 0))))
    lw = LANES
    lg = 2 * LANES
    consts = (gain.reshape(1, D), jnp.pad(mix, ((0, 2), (0, 0))), _bf(w_r), _bf(w_k), _bf(w_v),
              pad_c(w1, lw), pad_r(w2, lw), pad_c(a1, lw), pad_r(a2, lw), pad_c(g1, lg), pad_r(g2, lg),
              jnp.pad(jnp.stack([w0, a0, k_k, k_a, r_k]), ((0, 3), (0, 0)))) + _head_pool(D, RWKV_HEAD)
    full = lambda a: pl.BlockSpec(a.shape, lambda b, i: (0,) * a.ndim)
    tile = pl.BlockSpec((1, tm, D), lambda b, i: (b, i, 0))
    return pl.pallas_call(
        _rwkv_proj_kernel,
        grid=(B, S // tm),
        in_specs=[tile, pl.BlockSpec((1, 8, D), lambda b, i: (b, jnp.maximum(i * (tm // 8) - 1, 0), 0))]
        + [full(a) for a in consts],
        out_specs=[tile] * 8,
        out_shape=[jax.ShapeDtypeStruct((B, S, D), F32)] * 8,
        compiler_params=_params(("arbitrary", "arbitrary")),
        name="rwkv_proj",
    )(x, x, *consts)


def _rwkv_chunk_kernel(r_ref, wl_ref, k_ref, v_ref, kk_ref, b_ref, m_ref, g0_ref, rq_ref, y0_ref, *, cpb):
    C = RWKV_CHUNK
    lane = lax.broadcasted_iota(jnp.int32, (C, LANES), 1)
    head0 = lane < RWKV_HEAD
    ti = lax.broadcasted_iota(jnp.int32, (2 * C, 2 * C), 0) & (C - 1)
    tj = lax.broadcasted_iota(jnp.int32, (2 * C, 2 * C), 1) & (C - 1)
    strict = ti > tj
    incl = ti >= tj
    eye = lax.broadcasted_iota(jnp.int32, (LANES, LANES), 0) == lax.broadcasted_iota(jnp.int32, (LANES, LANES), 1)
    ltri = _bf(jnp.where(lax.broadcasted_iota(jnp.int32, (C, C), 0) >= lax.broadcasted_iota(jnp.int32, (C, C), 1),
                         1.0, 0.0))

    def stack(x):
        return jnp.concatenate([jnp.where(head0, x, 0.0), jnp.where(head0, 0.0, x)], axis=0)

    chunks = range(cpb)
    rows = [slice(c * C, (c + 1) * C) for c in chunks]
    wl = [wl_ref[0, rows[c], :] for c in chunks]
    cum = []
    for c in chunks:
        h1 = _bf(wl[c])
        r1 = wl[c] - h1.astype(F32)
        h2 = _bf(r1)
        h3 = _bf(r1 - h2.astype(F32))
        cum.append(_dot(ltri, h1) + _dot(ltri, h2) + _dot(ltri, h3))
    total = [cum[c][C - 1:C, :] for c in chunks]
    lhs_kk, lhs_r, vs, a = [], [], [], []
    for c in chunks:
        p_inv = jnp.exp(-cum[c])
        lhs_kk.append(stack(kk_ref[0, rows[c], :] * jnp.exp(cum[c] - wl[c])))
        lhs_r.append(stack(r_ref[0, rows[c], :] * jnp.exp(cum[c])))
        vs.append(_bf(stack(v_ref[0, rows[c], :])))
        a.append(_dot_nt(_bf(jnp.concatenate([lhs_kk[c], lhs_r[c]], axis=0)),
                         _bf(jnp.concatenate([stack(b_ref[0, rows[c], :] * p_inv),
                                              stack(k_ref[0, rows[c], :] * p_inv)], axis=0))))
    npow = [_bf(jnp.where(strict, a[c][0:2 * C, 0:2 * C], 0.0)) for c in chunks]
    a_k = [_bf(jnp.where(strict, a[c][0:2 * C, 2 * C:4 * C], 0.0)) for c in chunks]
    a_rb = [_bf(jnp.where(incl, a[c][2 * C:4 * C, 0:2 * C], 0.0)) for c in chunks]
    a_rk = [_bf(jnp.where(incl, a[c][2 * C:4 * C, 2 * C:4 * C], 0.0)) for c in chunks]
    x = [jnp.concatenate([_dot(a_k[c], vs[c]), lhs_kk[c]], axis=1) for c in chunks]
    x = [x[c] - _dot(npow[c], _bf(x[c])) for c in chunks]
    for _ in range(5):
        npow = [_bf(_dot(npow[c], npow[c])) for c in chunks]
        x = [x[c] + _dot(npow[c], _bf(x[c])) for c in chunks]
    uw = [_bf(-x[c]) for c in chunks]
    for c in chunks:
        t = _dot(a_rb[c], uw[c])
        y0_ref[0, 0, c] = _bf(_dot(a_rk[c], vs[c]) + t[:, 0:LANES])
        rq_ref[0, 0, c] = _bf(lhs_r[c] + t[:, LANES:2 * LANES])
    for c in chunks:
        p_end = jnp.exp(total[c] - cum[c])
        bc = _bf(stack(b_ref[0, rows[c], :] * p_end))
        kc = _bf(stack(k_ref[0, rows[c], :] * p_end))
        t = _dot_tn(bc, uw[c])
        g0_ref[0, 0, c] = _bf(_dot_tn(kc, vs[c]) + t[:, 0:LANES])
        m_ref[0, 0, c] = jnp.where(eye, jnp.exp(total[c]), 0.0) + t[:, LANES:2 * LANES]


def _rwkv_chunks(r, wl, k, v, kk, b):
    B, S, D = r.shape
    C = RWKV_CHUNK
    nc = S // C
    cpb = min(8, nc)
    hp = D // LANES
    tile = pl.BlockSpec((1, cpb * C, LANES), lambda bi, h, c: (bi, c, h))
    out = pl.BlockSpec((1, 1, cpb, LANES, LANES), lambda bi, h, c: (bi, h, c, 0, 0))
    return pl.pallas_call(
        functools.partial(_rwkv_chunk_kernel, cpb=cpb),
        grid=(B, hp, nc // cpb),
        in_specs=[tile] * 6,
        out_specs=[out] * 4,
        out_shape=[jax.ShapeDtypeStruct((B, hp, nc, LANES, LANES), F32)]
        + [jax.ShapeDtypeStruct((B, hp, nc, LANES, LANES), BF16)] * 3,
        compiler_params=_params(("arbitrary", "arbitrary", "arbitrary")),
        name="rwkv_chunk_summaries",
    )(r, wl, k, v, kk, b)


def _rwkv_scan_kernel(m_ref, g0_ref, rq_ref, y0_ref, y_ref, st_ref, *, cpb, hp):
    C = RWKV_CHUNK

    @pl.when(pl.program_id(1) == 0)
    def _():
        st_ref[...] = jnp.zeros_like(st_ref)

    for c in range(cpb):
        for h in range(hp):
            st = st_ref[h]
            s_hi, s_lo = _split2(st)
            y = y0_ref[0, h, c].astype(F32) + _dot(rq_ref[0, h, c], s_hi)
            y_ref[0, c * C:(c + 1) * C, h * LANES:(h + 1) * LANES] = y[0:C] + y[C:2 * C]
            m_hi, m_lo = _split2(m_ref[0, h, c])
            st_ref[h] = g0_ref[0, h, c] + _dot(m_hi, s_hi) + _dot(m_hi, s_lo) + _dot(m_lo, s_hi)


def _rwkv_scan(m, g0, rq, y0, S):
    B, hp, nc = m.shape[:3]
    C = RWKV_CHUNK
    cpb = min(4, nc)
    blk = pl.BlockSpec((1, hp, cpb, LANES, LANES), lambda bi, c: (bi, 0, c, 0, 0))
    return pl.pallas_call(
        functools.partial(_rwkv_scan_kernel, cpb=cpb, hp=hp),
        grid=(B, nc // cpb),
        in_specs=[blk] * 4,
        out_specs=pl.BlockSpec((1, cpb * C, hp * LANES), lambda bi, c: (bi, c, 0)),
        out_shape=jax.ShapeDtypeStruct((B, S, hp * LANES), F32),
        scratch_shapes=[pltpu.VMEM((hp, LANES, LANES), F32)],
        compiler_params=_params(("arbitrary", "arbitrary")),
        name="rwkv_state_scan",
    )(m, g0, rq, y0)


def _rwkv_out_kernel(x_ref, y_ref, bonus_ref, gate_ref, lnw_ref, lnb_ref, pool_ref, expand_ref, wo_ref, o_ref):
    y = y_ref[...]
    pool = pool_ref[...]
    expand = expand_ref[...]
    mu = _head_sum(_split2(y), pool, expand) * (1.0 / RWKV_HEAD)
    dev = y - mu
    var = _head_sum(_split2(dev * dev), pool, expand) * (1.0 / RWKV_HEAD)
    yn = dev * lax.rsqrt(var + RWKV_LNX_EPS) * lnw_ref[...] + lnb_ref[...]
    z = (yn + bonus_ref[...]) * gate_ref[...]
    o_ref[...] = x_ref[...] + _dot(_bf(z), wo_ref[...])


def _rwkv_out(x2d, y, bonus, gate, ln_w, ln_b, w_o):
    T, D = x2d.shape
    tm = min(256, T)
    row = pl.BlockSpec((tm, D), lambda i: (i, 0))
    vecs = pl.BlockSpec((1, D), lambda i: (0, 0))
    full = lambda shp: pl.BlockSpec(shp, lambda i: (0, 0))
    pool, expand = _head_pool(D, RWKV_HEAD)
    return pl.pallas_call(
        _rwkv_out_kernel,
        grid=(T // tm,),
        in_specs=[row, row, row, row, vecs, vecs, full((D, LANES)), full((LANES, D)), full((D, D))],
        out_specs=row,
        out_shape=jax.ShapeDtypeStruct((T, D), F32),
        compiler_params=_params(("arbitrary",)),
        name="rwkv_out",
    )(x2d, y, bonus, gate, ln_w.reshape(1, D), ln_b.reshape(1, D), pool, expand, _bf(w_o))


def _rwkv_layer(x, norm_gain, mix, w_r, w_k, w_v, w0, w1, w2, a0, a1, a2, g1, g2, k_k, k_a, r_k, ln_w, ln_b, w_o):
    B, S, D = x.shape
    r, wl, k, v, kk, b, bonus, gate = _rwkv_proj(x, norm_gain, mix, w_r, w_k, w_v, w0, w1, w2, a0, a1, a2,
                                                 g1, g2, k_k, k_a, r_k)
    m, g0, rq, y0 = _rwkv_chunks(r, wl, k, v, kk, b)
    y = _rwkv_scan(m, g0, rq, y0, S)
    T = B * S
    out = _rwkv_out(x.reshape(T, D), y.reshape(T, D), bonus.reshape(T, D), gate.reshape(T, D), ln_w, ln_b, w_o)
    return out.reshape(B, S, D)


def _router_kernel(x_ref, g_ref, whi_ref, wlo_ref, b_ref, xn_ref, route_ref):
    xn = _rms(x_ref[...], g_ref[...])
    xn_ref[...] = xn
    hi, lo = _split2(xn)
    logits = _dot(hi, whi_ref[...]) + _dot(hi, wlo_ref[...]) + _dot(lo, whi_ref[...]) + b_ref[...]
    lane = lax.broadcasted_iota(jnp.int32, logits.shape, 1)
    lane_f = lane.astype(F32)

    def top(vals):
        mx = jnp.max(vals, axis=-1, keepdims=True)
        return mx, jnp.min(jnp.where(vals == mx, lane_f, float(LANES)), axis=-1, keepdims=True)

    glog = jnp.where(lane < N_GROUPS, logits, NEG_INF)
    gmax, gidx = top(glog)
    gsum = jnp.sum(jnp.where(lane < N_GROUPS, jnp.exp(glog - gmax), 0.0), axis=-1, keepdims=True)
    grp_p = 1.0 / gsum
    first = float(N_GROUPS) + EXPERTS_PER_GROUP * gidx
    elog = jnp.where(lane_f >= first, jnp.where(lane_f < first + EXPERTS_PER_GROUP, logits, NEG_INF), NEG_INF)
    v1, i1 = top(elog)
    v2, i2 = top(jnp.where(lane_f == i1, NEG_INF, elog))
    e2 = jnp.exp(v2 - v1)
    gate1 = grp_p / (1.0 + e2)
    gate2 = grp_p * e2 / (1.0 + e2)
    route_ref[...] = jnp.where(lane == 0, i1 - N_GROUPS, jnp.where(lane == 1, i2 - N_GROUPS,
                               jnp.where(lane == 2, gate1, jnp.where(lane == 3, gate2, 0.0))))


def _router(x2d, gain, wg, bg, we, be):
    T, D = x2d.shape
    tm = min(256, T)
    w = jnp.pad(jnp.concatenate([wg, we], axis=1), ((0, 0), (0, LANES - N_GROUPS - N_EXPERTS)))
    w_hi = _bf(w)
    w_lo = _bf(w - w_hi.astype(F32))
    bias = jnp.pad(jnp.concatenate([bg, be]), (0, LANES - N_GROUPS - N_EXPERTS)).reshape(1, LANES)
    row = lambda n: pl.BlockSpec((tm, n), lambda i: (i, 0))
    full = lambda a: pl.BlockSpec(a.shape, lambda i: (0, 0))
    consts = (gain.reshape(1, D), w_hi, w_lo, bias)
    return pl.pallas_call(
        _router_kernel,
        grid=(T // tm,),
        in_specs=[row(D)] + [full(a) for a in consts],
        out_specs=[row(D), row(LANES)],
        out_shape=[jax.ShapeDtypeStruct((T, D), F32), jax.ShapeDtypeStruct((T, LANES), F32)],
        compiler_params=_params(("arbitrary",)),
        name="moe_router",
    )(x2d, *consts)


def _slot_tokens_kernel(dest_ref, slot_ref):
    def clear(s, carry):
        slot_ref[s] = 0
        return carry
    lax.fori_loop(0, slot_ref.shape[0], clear, 0, unroll=16)

    def place(p, carry):
        slot_ref[dest_ref[p]] = lax.shift_right_logical(p, 1)
        return carry
    lax.fori_loop(0, dest_ref.shape[0], place, 0, unroll=16)


def _slot_tokens(dest, n_slots):
    return pl.pallas_call(
        _slot_tokens_kernel,
        in_specs=[pl.BlockSpec(memory_space=pltpu.SMEM)],
        out_specs=pl.BlockSpec(memory_space=pltpu.SMEM),
        out_shape=jax.ShapeDtypeStruct((n_slots,), jnp.int32),
        name="moe_slot_tokens",
    )(dest)


def _row_gather_start(src_hbm, idx_ref, idx_base, idx_stride, dst, sem, rows, first=0):
    for r in range(first, first + rows):
        src_row = idx_ref[idx_base + r * idx_stride]
        pltpu.make_async_copy(src_hbm.at[pl.ds(src_row, 1)], dst.at[pl.ds(r, 1)], sem).start(priority=r % 2)


def _row_gather_wait(src_hbm, dst, sem, rows):
    pltpu.make_async_copy(src_hbm.at[pl.ds(0, rows)], dst, sem).wait()


def _expert_kernel(be_ref, nu_ref, tok_ref, xn_hbm, wg_ref, wu_ref, wd_ref, o_ref, xbuf, sem):
    i = pl.program_id(0)
    nu = nu_ref[0]
    last = pl.num_programs(0) - 1
    cur = lax.rem(i, MOE_BUFFERS)
    nxt1 = lax.rem(i + 1, MOE_BUFFERS)
    nxt2 = lax.rem(i + 2, MOE_BUFFERS)

    @pl.when(i == 0)
    def _():
        for blk in range(2):
            _row_gather_start(xn_hbm, tok_ref, blk * MOE_ROWS, 1, xbuf.at[blk], sem.at[blk], MOE_ROWS)

    @pl.when(i < nu)
    def _():
        _row_gather_wait(xn_hbm, xbuf.at[cur], sem.at[cur], MOE_ROWS)
        ahead = jnp.minimum(i + 2, last)
        d_model = xbuf.shape[-1]
        pieces = 2 * (d_model // MXU_TILE) + d_model // MXU_TILE
        per_piece = -(-MOE_ROWS // pieces)
        started = [0]

        def start_some():
            rows = min(per_piece, MOE_ROWS - started[0])
            _row_gather_start(xn_hbm, tok_ref, ahead * MOE_ROWS, 1, xbuf.at[nxt2], sem.at[nxt2], rows, started[0])
            started[0] += rows

        gate = jnp.zeros((MOE_ROWS, D_EXPERT), F32)
        up = jnp.zeros((MOE_ROWS, D_EXPERT), F32)
        for c in range(d_model // MXU_TILE):
            cols = slice(c * MXU_TILE, (c + 1) * MXU_TILE)
            xc = _bf(xbuf[cur, :, cols])
            gate = gate + _dot(xc, _bf(wg_ref[0, 0, cols, :]))
            start_some()
            up = up + _dot(xc, _bf(wu_ref[0, 0, cols, :]))
            start_some()
        hid = _bf(jax.nn.silu(gate) * up)
        for c in range(d_model // MXU_TILE):
            cols = slice(c * MXU_TILE, (c + 1) * MXU_TILE)
            y = _dot(hid, _bf(wd_ref[0, 0, :, cols]))
            start_some()
            o_ref[:, cols] = y
        assert started[0] == MOE_ROWS

        @pl.when(i == nu - 1)
        def _():
            _row_gather_wait(xn_hbm, xbuf.at[nxt1], sem.at[nxt1], MOE_ROWS)
            _row_gather_wait(xn_hbm, xbuf.at[nxt2], sem.at[nxt2], MOE_ROWS)

    @pl.when(i >= nu)
    def _():
        o_ref[...] = jnp.zeros_like(o_ref)


def _experts(xn, slot_tok, block_e, n_used, layer, e_gate, e_up, e_down):
    T, D = xn.shape
    n_blocks = slot_tok.shape[0] // MOE_ROWS
    wspec = lambda shp: pl.BlockSpec((1, 1) + shp, lambda i, be, nu, tok: (layer, be[i], 0, 0))
    return pl.pallas_call(
        _expert_kernel,
        grid_spec=pltpu.PrefetchScalarGridSpec(
            num_scalar_prefetch=3,
            grid=(n_blocks,),
            in_specs=[pl.BlockSpec(memory_space=pl.ANY), wspec((D, D_EXPERT)), wspec((D, D_EXPERT)),
                      wspec((D_EXPERT, D))],
            out_specs=pl.BlockSpec((MOE_ROWS, D), lambda i, be, nu, tok: (i, 0)),
            scratch_shapes=[pltpu.VMEM((MOE_BUFFERS, MOE_ROWS, D), F32), pltpu.SemaphoreType.DMA((MOE_BUFFERS,))],
        ),
        out_shape=jax.ShapeDtypeStruct((n_blocks * MOE_ROWS, D), F32),
        compiler_params=_params(("arbitrary",)),
        name="moe_experts",
    )(block_e, n_used, slot_tok, xn, e_gate, e_up, e_down)


def _combine_kernel(dest_ref, x_ref, route_ref, ys_hbm, o_ref, buf, sem, *, tm):
    i = pl.program_id(0)
    cur = i & 1

    def start(tile, par):
        for k in range(2):
            _row_gather_start(ys_hbm, dest_ref, tile * (2 * tm) + k, 2, buf.at[par, k], sem.at[par], tm)

    @pl.when(i == 0)
    def _():
        start(0, 0)

    @pl.when(i + 1 < pl.num_programs(0))
    def _():
        start(i + 1, 1 - cur)

    for k in range(2):
        _row_gather_wait(ys_hbm, buf.at[cur, k], sem.at[cur], tm)
    route = route_ref[...]
    o_ref[...] = x_ref[...] + route[:, 2:3] * buf[cur, 0] + route[:, 3:4] * buf[cur, 1]


def _combine(x2d, y_slots, dest, route):
    T, D = x2d.shape
    tm = min(256, T)
    row = lambda n: pl.BlockSpec((tm, n), lambda i, dest: (i, 0))
    return pl.pallas_call(
        functools.partial(_combine_kernel, tm=tm),
        grid_spec=pltpu.PrefetchScalarGridSpec(
            num_scalar_prefetch=1,
            grid=(T // tm,),
            in_specs=[row(D), row(LANES), pl.BlockSpec(memory_space=pl.ANY)],
            out_specs=row(D),
            scratch_shapes=[pltpu.VMEM((2, 2, tm, D), F32), pltpu.SemaphoreType.DMA((2,))],
        ),
        out_shape=jax.ShapeDtypeStruct((T, D), F32),
        compiler_params=_params(("arbitrary",)),
        name="moe_combine",
    )(dest, x2d, route, y_slots)


def _moe_layer(x, gain, wg, bg, we, be, layer, e_gate, e_up, e_down):
    B, S, D = x.shape
    T = B * S
    x2d = x.reshape(T, D)
    xn, route = _router(x2d, gain, wg, bg, we, be)
    flat_e = route[:, 0:2].astype(jnp.int32).reshape(-1)
    n_pairs = 2 * T
    onehot = (flat_e[:, None] == jnp.arange(N_EXPERTS, dtype=jnp.int32)[None, :]).astype(jnp.int32)
    csum = jnp.cumsum(onehot, axis=0)
    rank = jnp.take_along_axis(csum, flat_e[:, None], axis=1)[:, 0] - 1
    counts = csum[-1]
    padded = (counts + MOE_ROWS - 1) // MOE_ROWS * MOE_ROWS
    pad_end = jnp.cumsum(padded)
    dest = (pad_end - padded)[flat_e] + rank
    n_blocks = -(-n_pairs // MOE_ROWS) + N_EXPERTS
    dest = dest.astype(jnp.int32)
    slot_tok = _slot_tokens(dest, n_blocks * MOE_ROWS)
    block_start = jnp.arange(n_blocks, dtype=jnp.int32) * MOE_ROWS
    block_e = jnp.minimum(jnp.sum((pad_end[None, :] <= block_start[:, None]).astype(jnp.int32), axis=1),
                          N_EXPERTS - 1)
    n_used = (pad_end[-1:] // MOE_ROWS).astype(jnp.int32)
    y_slots = _experts(xn, slot_tok, block_e, n_used, layer, e_gate, e_up, e_down)
    return _combine(x2d, y_slots, dest, route).reshape(B, S, D)


def kernel(x, mix_norm, attn_w_in, attn_gate_bias, diff_q_norm, diff_k_norm, diff_lambda, diff_subln, nsa_q_norm, nsa_k_norm, nsa_cmp_pe, nsa_cmp_w1, nsa_cmp_w2, attn_w_out, rwkv_mix, rwkv_w_r, rwkv_w_k, rwkv_w_v, rwkv_decay_w0, rwkv_decay_w1, rwkv_decay_w2, rwkv_iclr_a0, rwkv_iclr_a1, rwkv_iclr_a2, rwkv_gate_g1, rwkv_gate_g2, rwkv_k_k, rwkv_k_a, rwkv_r_k, rwkv_ln_w, rwkv_ln_b, rwkv_w_o, ffn_norm, router_group_w, router_group_b, router_expert_w, router_expert_b, expert_w_gate, expert_w_up, expert_w_down):
    depth = mix_norm.shape[0]
    for layer in range(depth):
        i = layer // 2
        if layer % 2 == 0:
            x = _hybrid_attention_layer(
                x, layer, mix_norm[layer], attn_w_in[i], attn_gate_bias[i], diff_q_norm[i], diff_k_norm[i],
                diff_lambda[i], diff_subln[i], nsa_q_norm[i], nsa_k_norm[i], nsa_cmp_pe[i], nsa_cmp_w1[i],
                nsa_cmp_w2[i], attn_w_out[i])
        else:
            x = _rwkv_layer(
                x, mix_norm[layer], rwkv_mix[i], rwkv_w_r[i], rwkv_w_k[i], rwkv_w_v[i], rwkv_decay_w0[i],
                rwkv_decay_w1[i], rwkv_decay_w2[i], rwkv_iclr_a0[i], rwkv_iclr_a1[i], rwkv_iclr_a2[i],
                rwkv_gate_g1[i], rwkv_gate_g2[i], rwkv_k_k[i], rwkv_k_a[i], rwkv_r_k[i], rwkv_ln_w[i],
                rwkv_ln_b[i], rwkv_w_o[i])
        x = _moe_layer(x, ffn_norm[layer], router_group_w[layer], router_group_b[layer], router_expert_w[layer],
                       router_expert_b[layer], layer, expert_w_gate, expert_w_up, expert_w_down)
    return x
```

```python
import functools
import math

import jax
import jax.numpy as jnp
from jax import lax
from jax.experimental import pallas as pl
from jax.experimental.pallas import tpu as pltpu

F32 = jnp.float32
BF16 = jnp.bfloat16

D_MODEL = 1024
HEAD_DIM = 64
NORM_EPS = 1e-6
NEG_INF = -1e30
FORCE_SCORE = 1e4

DIFF_HEADS = 8
DIFF_QK_DIM = 32
NSA_HEADS = 8
NSA_KV_GROUPS = 2
NSA_HPG = 4
NSA_CMP_BLOCK = 32
NSA_CMP_STRIDE = 16
NSA_SEL_BLOCK = 64
NSA_TOP_N = 16
NSA_WINDOW = 512
IN_COLS = 2840
IN_COLS_PAD = 2944
GATE_COLS = 24

RWKV_HEAD = 64
RWKV_LNX_EPS = 64e-5
RWKV_CHUNK = 64

N_GROUPS = 4
EXPERTS_PER_GROUP = 8
N_EXPERTS = 32
D_EXPERT = 256
MOE_ROWS = 256
MOE_BUFFERS = 3

LANES = 128
MXU_TILE = 256
VMEM_LIMIT = 56 * 1024 * 1024


def _bf(x):
    return x.astype(BF16)


def _dot(a, b):
    return jnp.dot(a, b, preferred_element_type=F32)


def _dot_nt(a, b):
    return lax.dot_general(a, b, (((1,), (1,)), ((), ())), preferred_element_type=F32)


def _dot_tn(a, b):
    return lax.dot_general(a, b, (((0,), (0,)), ((), ())), preferred_element_type=F32)


def _split2(x):
    hi = _bf(x)
    lo = _bf(x - hi.astype(F32))
    return hi, lo


def _params(sem):
    return pltpu.CompilerParams(dimension_semantics=sem, vmem_limit_bytes=VMEM_LIMIT)


def _block_diag(n, group):
    r = jnp.arange(n) // group
    return ((r[:, None] == r[None, :]).astype(F32) / group).astype(BF16)


def _rms(x, gain):
    return x * lax.rsqrt(jnp.mean(x * x, axis=-1, keepdims=True) + NORM_EPS) * gain


def _inproj_kernel(x_ref, g_ref, wa_ref, wbt_ref, b_ref, bd32_ref, bd64_ref, gk_ref, gnk_ref, gq_ref, gnq_ref,
                   qd_ref, kd_ref, vd_ref, qn_ref, cmp_ref, ks_ref, vs_ref, kw_ref, vw_ref, gt_ref):
    xn = _bf(_rms(x_ref[0], g_ref[...]))
    ha = _dot(xn, wa_ref[...])
    hb = _dot_nt(wbt_ref[...], xn)
    bd32 = bd32_ref[...]
    bd64 = bd64_ref[...]

    def gnorm(seg, bd, gain):
        hi, lo = _split2(seg * seg)
        return seg * lax.rsqrt(_dot(hi, bd) + _dot(lo, bd) + NORM_EPS) * gain

    def gnorm_t(seg, bd, gain):
        hi, lo = _split2(seg * seg)
        return seg * lax.rsqrt(_dot(bd, hi) + _dot(bd, lo) + NORM_EPS) * gain

    kd_ref[0] = _bf(gnorm(ha[:, 0:512], bd32, gk_ref[...]))
    cmp_ref[0, 0] = ha[:, 512:640]
    cmp_ref[0, 1] = ha[:, 640:768]
    ksw = gnorm(ha[:, 768:1024], bd64[0:256, 0:256], gnk_ref[...])
    ks_ref[0] = _bf(ksw[:, 0:128])
    kw_ref[0] = _bf(ksw[:, 128:256])
    gt_ref[0] = jax.nn.sigmoid(ha[:, 1024:1152] + b_ref[...])
    qd_ref[0, 0] = _bf(gnorm_t(hb[0:512], bd32, gq_ref[...]))
    vd_ref[0, 0] = _bf(hb[512:1024])
    qn_ref[0, 0] = _bf(gnorm_t(hb[1024:1536], bd64, gnq_ref[...]))
    vs_ref[0, 0] = _bf(hb[1536:1664])
    vw_ref[0, 0] = _bf(hb[1664:1792])


def _attn_inproj(x, gain, w_in, gate_bias, dq_gain, dk_gain, nq_gain, nk_gain, tm):
    B, S, D = x.shape
    n = S // tm
    c = lambda lo, hi: w_in[:, lo:hi]
    gate_w = jnp.pad(c(2816, IN_COLS), ((0, 0), (0, LANES - GATE_COLS)))
    wa = _bf(jnp.concatenate([c(512, 1024), c(2048, 2304), c(2304, 2432), c(2560, 2688), gate_w], axis=1))
    wbt = _bf(jnp.concatenate([c(0, 512), c(1024, 1536), c(1536, 2048), c(2432, 2560), c(2688, 2816)], axis=1).T)
    bias = jnp.pad(gate_bias, (0, LANES - GATE_COLS)).reshape(1, LANES)
    log2e = math.log2(math.e)
    gk = jnp.tile(dk_gain, 16).reshape(1, 512)
    gnk = jnp.concatenate([jnp.tile(nk_gain[1], 2), jnp.tile(nk_gain[2], 2)]).reshape(1, 256)
    gq = (jnp.tile(dq_gain, 16) * (DIFF_QK_DIM ** -0.5 * log2e)).reshape(512, 1)
    gnq = (jnp.tile(nq_gain, 8) * (HEAD_DIM ** -0.5 * log2e)).reshape(512, 1)
    consts = (gain.reshape(1, D), wa, wbt, bias, _block_diag(512, 32), _block_diag(512, 64), gk, gnk, gq, gnq)
    full = lambda a: pl.BlockSpec(a.shape, lambda b, i: (0,) * a.ndim)
    tok = lambda w: pl.BlockSpec((1, tm, w), lambda b, i: (b, i, 0))
    tr = lambda r: pl.BlockSpec((1, 1, r, tm), lambda b, i: (b, i, 0, 0))
    tok_shape = lambda w, dt: jax.ShapeDtypeStruct((B, S, w), dt)
    tr_shape = lambda r: jax.ShapeDtypeStruct((B, n, r, tm), BF16)
    return pl.pallas_call(
        _inproj_kernel,
        grid=(B, n),
        in_specs=[tok(D)] + [full(a) for a in consts],
        out_specs=[tr(512), tok(512), tr(512), tr(512), pl.BlockSpec((1, 2, tm, LANES), lambda b, i: (b, 0, i, 0)),
                   tok(LANES), tr(LANES), tok(LANES), tr(LANES), tok(LANES)],
        out_shape=[tr_shape(512), tok_shape(512, BF16), tr_shape(512), tr_shape(512),
                   jax.ShapeDtypeStruct((B, 2, S, LANES), F32),
                   tok_shape(LANES, BF16), tr_shape(LANES), tok_shape(LANES, BF16), tr_shape(LANES),
                   tok_shape(LANES, F32)],
        compiler_params=_params(("arbitrary", "arbitrary")),
        name="attn_inproj",
    )(x, *consts)


SHIFT_LIMIT = 40.0


def _score_bound(q_gain, k_gain, dim):
    bound = dim * jnp.max(jnp.abs(q_gain)) * jnp.max(jnp.abs(k_gain)) * 1.02 + 0.25
    return jnp.stack([bound, (bound <= SHIFT_LIMIT).astype(F32)]).astype(F32)


def _flash_pass(lo, hi, scores, values, mask_body, mask_last, s_ref, p_ref, shift):
    tk, cols = s_ref.shape
    dv = HEAD_DIM
    p_ref[...] = jnp.zeros_like(p_ref)
    s_ref[...] = scores(lo)

    def process(j, carry, mask, s_next):
        s = s_ref[...]
        if mask is not None:
            s = mask(j, s)
        pv = values(jnp.maximum(j - 1, lo), p_ref[...])
        if shift is None:
            m, l, acc = carry
            m_new = jnp.maximum(m, jnp.max(s, axis=0, keepdims=True))
            alpha = jnp.exp2(m - m_new)
            p = jnp.exp2(s - m_new)
            carry = (m_new, alpha * l + jnp.sum(p, axis=0, keepdims=True), (acc + pv) * alpha)
        else:
            l, acc = carry
            p = jnp.exp2(s - shift)
            carry = (l + jnp.sum(p, axis=0, keepdims=True), acc + pv)
        p_ref[...] = _bf(p)
        if s_next is not None:
            s_ref[...] = s_next
        return carry

    init = (jnp.zeros((1, cols), F32), jnp.zeros((dv, cols), F32))
    if shift is None:
        init = (jnp.full((1, cols), NEG_INF, F32),) + init
    carry = lax.fori_loop(lo, hi, lambda j, c: process(j, c, mask_body, scores(j + 1)), init)
    l, acc = process(hi, carry, mask_last, None)[-2:]
    return (acc + values(hi, p_ref[...])) / l


def _flash_tiles(lo, hi, scores, values, mask_body, mask_last, s_ref, p_ref, o_ref, bound_ref):
    @pl.when(bound_ref[1] > 0.5)
    def _():
        o_ref[...] = _flash_pass(lo, hi, scores, values, mask_body, mask_last, s_ref, p_ref, bound_ref[0])

    @pl.when(bound_ref[1] <= 0.5)
    def _():
        o_ref[...] = _flash_pass(lo, hi, scores, values, mask_body, mask_last, s_ref, p_ref, None)

    return o_ref[...]


def _diff_attn_kernel(qt_ref, k_ref, vt_ref, lam_ref, sg_ref, bound_ref, o_ref, s_ref, p_ref, acc_ref, *,
                      tq, lambda_init):
    lp = lam_ref[...]
    lam = (jnp.exp(jnp.sum(lp[0:1] * lp[1:2], axis=-1, keepdims=True))
           - jnp.exp(jnp.sum(lp[2:3] * lp[3:4], axis=-1, keepdims=True)) + lambda_init)

    def causal(j, s):
        kpos = lax.broadcasted_iota(jnp.int32, (tq, 4 * tq), 0)
        qpos = lax.broadcasted_iota(jnp.int32, (tq, 4 * tq), 1) & (tq - 1)
        return jnp.where(kpos <= qpos, s, NEG_INF)

    def values(j, p):
        vt = vt_ref[0, j]
        return jnp.concatenate([_dot(vt[0:HEAD_DIM], p[:, 0:2 * tq]),
                                _dot(vt[HEAD_DIM:2 * HEAD_DIM], p[:, 2 * tq:4 * tq])], axis=1)

    def query_tile(i, carry):
        q = qt_ref[0, i]
        zero = jnp.zeros((DIFF_QK_DIM, tq), BF16)
        qb = jnp.concatenate(
            [jnp.concatenate([q[r * DIFF_QK_DIM:(r + 1) * DIFF_QK_DIM] if c == r else zero for c in range(4)],
                             axis=1) for r in range(4)], axis=0)
        o = _flash_tiles(0, i, lambda j: _dot(k_ref[0, j], qb), values, None, causal, s_ref, p_ref, acc_ref,
                         bound_ref)
        heads = []
        for h in range(2):
            oh = o[:, 2 * h * tq:(2 * h + 1) * tq] - lam * o[:, (2 * h + 1) * tq:(2 * h + 2) * tq]
            oh = oh * lax.rsqrt(jnp.mean(oh * oh, axis=0, keepdims=True) + NORM_EPS) * sg_ref[...]
            heads.append(oh * (1.0 - lambda_init))
        o_ref[0, pl.ds(pl.multiple_of(i * tq, tq), tq), :] = jnp.concatenate(heads, axis=0).T
        return carry

    lax.fori_loop(0, qt_ref.shape[1], query_tile, 0)


def _diff_attention(qd_t, kd, vd_t, lam_p, subln, bound, lambda_init, tq):
    B, n = qd_t.shape[:2]
    S = n * tq
    pairs = DIFF_HEADS // 2
    return pl.pallas_call(
        functools.partial(_diff_attn_kernel, tq=tq, lambda_init=lambda_init),
        grid=(B, pairs),
        in_specs=[
            pl.BlockSpec((1, n, LANES, tq), lambda b, h: (b, 0, h, 0)),
            pl.BlockSpec((1, n, tq, LANES), lambda b, h: (b, 0, 0, h)),
            pl.BlockSpec((1, n, LANES, tq), lambda b, h: (b, 0, h, 0)),
            pl.BlockSpec((4, DIFF_QK_DIM), lambda b, h: (0, 0)),
            pl.BlockSpec((HEAD_DIM, 1), lambda b, h: (0, 0)),
            pl.BlockSpec(memory_space=pltpu.SMEM),
        ],
        out_specs=pl.BlockSpec((1, S, LANES), lambda b, h: (b, 0, h)),
        out_shape=jax.ShapeDtypeStruct((B, S, DIFF_HEADS * HEAD_DIM), F32),
        scratch_shapes=[pltpu.VMEM((tq, 4 * tq), F32), pltpu.VMEM((tq, 4 * tq), BF16),
                        pltpu.VMEM((HEAD_DIM, 4 * tq), F32)],
        compiler_params=_params(("arbitrary", "arbitrary")),
        name="diff_attention",
    )(qd_t, kd.reshape(B, n, tq, DIFF_HEADS * HEAD_DIM), vd_t, lam_p, subln.reshape(HEAD_DIM, 1), bound)


def _nsa_compress_kernel(t_ref, pe_ref, w1_ref, w2_ref, bd_ref, kg_ref, kc_ref, vct_ref, *, nch):
    half = NSA_CMP_BLOCK // 2
    ya = jnp.zeros((nch, 2 * LANES), F32)
    yb = jnp.zeros((nch, 2 * LANES), F32)
    for l in range(half):
        rows = pl.ds(l, nch, stride=NSA_CMP_STRIDE)
        piece = jnp.concatenate([t_ref[0, 0, rows, :], t_ref[0, 1, rows, :]], axis=1)
        ya = ya + _dot(_bf(piece + pe_ref[l:l + 1]), w1_ref[l])
        yb = yb + _dot(_bf(piece + pe_ref[half + l:half + l + 1]), w1_ref[half + l])
    pre = ya + pltpu.roll(yb, nch - 1, 0)
    out = _dot(_bf(jax.nn.gelu(pre)), w2_ref[...])
    k = out[:, 0:LANES]
    hi, lo = _split2(k * k)
    ms = _dot(hi, bd_ref[...]) + _dot(lo, bd_ref[...])
    kc_ref[0] = _bf(k * lax.rsqrt(ms + NORM_EPS) * kg_ref[...])
    vct_ref[0] = _bf(out[:, LANES:2 * LANES].T)


def _nsa_compress(cmp_kv, cmp_pe, cmp_w1, cmp_w2, k_gain0):
    B, _, S, _ = cmp_kv.shape
    nch = S // NSA_CMP_STRIDE

    def over_groups(w):
        parts = [w[0], w[0], w[1], w[1]]
        zero = jnp.zeros_like(w[0])
        return jnp.concatenate(
            [jnp.concatenate([parts[r] if c == r else zero for c in range(4)], axis=-1) for r in range(4)], axis=-2)

    w1 = _bf(over_groups(cmp_w1.reshape(2, NSA_CMP_BLOCK, HEAD_DIM, HEAD_DIM)))
    w2 = _bf(over_groups(cmp_w2))
    pe = jnp.concatenate([cmp_pe[0], cmp_pe[0], cmp_pe[1], cmp_pe[1]], axis=-1)
    full = lambda a: pl.BlockSpec(a.shape, lambda b: (0,) * a.ndim)
    consts = (pe, w1, w2, _block_diag(LANES, HEAD_DIM), jnp.tile(k_gain0, 2).reshape(1, LANES))
    return pl.pallas_call(
        functools.partial(_nsa_compress_kernel, nch=nch),
        grid=(B,),
        in_specs=[pl.BlockSpec((1, 2, S, LANES), lambda b: (b, 0, 0, 0))] + [full(a) for a in consts],
        out_specs=[pl.BlockSpec((1, nch, LANES), lambda b: (b, 0, 0)), pl.BlockSpec((1, LANES, nch), lambda b: (b, 0, 0))],
        out_shape=[jax.ShapeDtypeStruct((B, nch, LANES), BF16), jax.ShapeDtypeStruct((B, LANES, nch), BF16)],
        compiler_params=_params(("arbitrary",)),
        name="nsa_compress",
    )(cmp_kv, *consts)


def _group_queries(q, g):
    qg = jnp.concatenate([q[h * HEAD_DIM:(h + 1) * HEAD_DIM] for h in range(NSA_HPG)], axis=1)
    zero = jnp.zeros_like(qg)
    return jnp.where(g == 0, jnp.concatenate([qg, zero], axis=0), jnp.concatenate([zero, qg], axis=0))


def _heads_token_major(o, tq):
    return jnp.concatenate([o[:, h * tq:(h + 1) * tq] for h in range(NSA_HPG)], axis=0).T


def _nsa_cmp_attn_kernel(q_ref, k_ref, vt_ref, o_ref, sel_ref, *, tq, nch, n_sel, top_n):
    i = pl.program_id(2)
    cols = NSA_HPG * tq
    n_cmp = nch - 1
    s = _dot(k_ref[0], _group_queries(q_ref[0, 0], pl.program_id(1)))
    pos = i * tq + (lax.broadcasted_iota(jnp.int32, (nch, cols), 1) & (tq - 1))
    c = lax.broadcasted_iota(jnp.int32, (nch, cols), 0)
    ok = jnp.where(c < n_cmp, c * NSA_CMP_STRIDE + (NSA_CMP_BLOCK - 1), 1 << 30) <= pos
    s = jnp.where(ok, s, NEG_INF)
    p = jnp.where(ok, jnp.exp2(s - jnp.max(s, axis=0, keepdims=True)), 0.0)
    l = jnp.sum(p, axis=0, keepdims=True)
    p = p / jnp.where(l > 0.0, l, 1.0)
    o_ref[0] = _heads_token_major(_dot(vt_ref[0], _bf(p)), tq)

    pg = p[:, 0:tq]
    for h in range(1, NSA_HPG):
        pg = pg + p[:, h * tq:(h + 1) * tq]
    jj = lax.broadcasted_iota(jnp.int32, (n_sel, nch), 0) * NSA_SEL_BLOCK
    cc = lax.broadcasted_iota(jnp.int32, (n_sel, nch), 1) * NSA_CMP_STRIDE
    cover = jnp.where(cc < jj + NSA_SEL_BLOCK,
                      jnp.where(cc + NSA_CMP_BLOCK > jj, jnp.where(cc < n_cmp * NSA_CMP_STRIDE, 1.0, 0.0), 0.0), 0.0)
    cover = _bf(cover)
    hi, lo = _split2(pg)
    imp = _dot(cover, hi) + _dot(cover, lo)
    blk = lax.broadcasted_iota(jnp.int32, (n_sel, tq), 0)
    cur = (i * tq + lax.broadcasted_iota(jnp.int32, (n_sel, tq), 1)) >> 6
    imp = jnp.where(blk == cur, FORCE_SCORE, jnp.where(blk == 0, FORCE_SCORE, jnp.where(blk > cur, NEG_INF, imp)))
    blk_f = blk.astype(F32)
    sel = jnp.zeros((n_sel, tq), F32)
    for _ in range(top_n):
        mx = jnp.max(imp, axis=0, keepdims=True)
        first = jnp.min(jnp.where(imp == mx, blk_f, float(n_sel)), axis=0, keepdims=True)
        hit = blk_f == first
        sel = jnp.where(hit, 1.0, sel)
        imp = jnp.where(hit, -jnp.inf, imp)
    sel_ref[0, 0, 0] = _bf(sel)


def _nsa_cmp_attention(qn_t, kc, vc_t, tq):
    B, n = qn_t.shape[:2]
    S = n * tq
    G = NSA_KV_GROUPS
    nch = S // NSA_CMP_STRIDE
    n_sel = S // NSA_SEL_BLOCK
    top_n = min(NSA_TOP_N, n_sel)
    gcols = NSA_HPG * HEAD_DIM
    return pl.pallas_call(
        functools.partial(_nsa_cmp_attn_kernel, tq=tq, nch=nch, n_sel=n_sel, top_n=top_n),
        grid=(B, G, n),
        in_specs=[
            pl.BlockSpec((1, 1, gcols, tq), lambda b, g, i: (b, i, g, 0)),
            pl.BlockSpec((1, nch, LANES), lambda b, g, i: (b, 0, 0)),
            pl.BlockSpec((1, HEAD_DIM, nch), lambda b, g, i: (b, g, 0)),
        ],
        out_specs=[pl.BlockSpec((1, tq, gcols), lambda b, g, i: (b, i, g)),
                   pl.BlockSpec((1, 1, 1, n_sel, tq), lambda b, g, i: (b, g, i, 0, 0))],
        out_shape=[jax.ShapeDtypeStruct((B, S, G * gcols), F32),
                   jax.ShapeDtypeStruct((B, G, n, n_sel, tq), BF16)],
        compiler_params=_params(("arbitrary", "arbitrary", "arbitrary")),
        name="nsa_cmp_attention",
    )(qn_t, kc, vc_t)


def _nsa_sel_win_kernel(q_ref, ks_ref, vst_ref, kw_ref, vwt_ref, sel_ref, bs_ref, bw_ref, os_ref, ow_ref,
                        s_ref, p_ref, acc_ref, *, tq, n_sel):
    g = pl.program_id(1)
    kloc = lax.broadcasted_iota(jnp.int32, (tq, tq), 0)
    qloc = lax.broadcasted_iota(jnp.int32, (tq, tq), 1)
    blk_row = lax.broadcasted_iota(jnp.int32, (tq, n_sel), 0)
    blk_col = lax.broadcasted_iota(jnp.int32, (tq, n_sel), 1)

    def masked(s, keep):
        return jnp.where(jnp.concatenate([keep] * NSA_HPG, axis=1) > 0.5, s, NEG_INF)

    def query_tile(i, carry):
        qt = _group_queries(q_ref[0, i], g)
        selm = sel_ref[0, 0, i]
        qpos = i * tq + qloc
        rows = pl.ds(pl.multiple_of(i * tq, tq), tq)

        def sel_mask(j, s):
            expand = _bf(jnp.where(blk_col == ((j * tq + blk_row) >> 6), 1.0, 0.0))
            chosen = _dot(expand, selm)
            return masked(s, jnp.where(j * tq + kloc <= qpos, chosen, 0.0))

        o = _flash_tiles(0, i, lambda j: _dot(ks_ref[0, j], qt), lambda j, p: _dot(vst_ref[0, j], p),
                         sel_mask, sel_mask, s_ref, p_ref, acc_ref, bs_ref)
        os_ref[0, rows, :] = _heads_token_major(o, tq)

        def win_mask(j, s):
            dist = qpos - (j * tq + kloc)
            return masked(s, jnp.where(dist >= 0, jnp.where(dist < NSA_WINDOW, 1.0, 0.0), 0.0))

        first = jnp.maximum(i - (NSA_WINDOW - 1 + tq - 1) // tq, 0)
        o = _flash_tiles(first, i, lambda j: _dot(kw_ref[0, j], qt), lambda j, p: _dot(vwt_ref[0, j], p),
                         win_mask, win_mask, s_ref, p_ref, acc_ref, bw_ref)
        ow_ref[0, rows, :] = _heads_token_major(o, tq)
        return carry

    lax.fori_loop(0, q_ref.shape[1], query_tile, 0)


def _nsa_sel_win(qn_t, ks, vs_t, kw, vw_t, sel, bound_sel, bound_win, tq):
    B, n = qn_t.shape[:2]
    S = n * tq
    G = NSA_KV_GROUPS
    n_sel = S // NSA_SEL_BLOCK
    gcols = NSA_HPG * HEAD_DIM
    kspec = pl.BlockSpec((1, n, tq, LANES), lambda b, g: (b, 0, 0, 0))
    vtspec = pl.BlockSpec((1, n, HEAD_DIM, tq), lambda b, g: (b, 0, g, 0))
    ospec = pl.BlockSpec((1, S, gcols), lambda b, g: (b, 0, g))
    return pl.pallas_call(
        functools.partial(_nsa_sel_win_kernel, tq=tq, n_sel=n_sel),
        grid=(B, G),
        in_specs=[pl.BlockSpec((1, n, gcols, tq), lambda b, g: (b, 0, g, 0)), kspec, vtspec, kspec, vtspec,
                  pl.BlockSpec((1, 1, n, n_sel, tq), lambda b, g: (b, g, 0, 0, 0)),
                  pl.BlockSpec(memory_space=pltpu.SMEM), pl.BlockSpec(memory_space=pltpu.SMEM)],
        out_specs=[ospec, ospec],
        out_shape=[jax.ShapeDtypeStruct((B, S, G * gcols), F32)] * 2,
        scratch_shapes=[pltpu.VMEM((tq, NSA_HPG * tq), F32), pltpu.VMEM((tq, NSA_HPG * tq), BF16),
                        pltpu.VMEM((HEAD_DIM, NSA_HPG * tq), F32)],
        compiler_params=_params(("arbitrary", "arbitrary")),
        name="nsa_sel_win_attention",
    )(qn_t, ks.reshape(B, n, tq, LANES), vs_t, kw.reshape(B, n, tq, LANES), vw_t, sel, bound_sel, bound_win)


def _attn_outproj_kernel(x_ref, d_ref, oc_ref, os_ref, ow_ref, gt_ref, ge_ref, w_ref, o_ref):
    ghi, glo = _split2(gt_ref[...])

    def gate(r):
        return _dot(ghi, ge_ref[r]) + _dot(glo, ge_ref[r])

    nsa = gate(0) * oc_ref[...] + gate(1) * os_ref[...] + gate(2) * ow_ref[...]
    y = _dot(_bf(d_ref[...]), w_ref[0:512, :]) + _dot(_bf(nsa), w_ref[512:1024, :])
    o_ref[...] = x_ref[...] + y


def _attn_outproj(x2d, diff_o, o_cmp, o_sel, o_win, gates, w_out):
    T, D = x2d.shape
    tm = min(256, T)
    col = jnp.arange(512) // HEAD_DIM
    src = jnp.arange(LANES)
    ge = jnp.stack([_bf((src[:, None] == col[None, :] * 3 + r).astype(F32)) for r in range(3)])
    row = lambda n: pl.BlockSpec((tm, n), lambda i: (i, 0))
    return pl.pallas_call(
        _attn_outproj_kernel,
        grid=(T // tm,),
        in_specs=[row(D), row(512), row(512), row(512), row(512), row(LANES),
                  pl.BlockSpec((3, LANES, 512), lambda i: (0, 0, 0)),
                  pl.BlockSpec((D, D), lambda i: (0, 0))],
        out_specs=row(D),
        out_shape=jax.ShapeDtypeStruct((T, D), F32),
        compiler_params=_params(("arbitrary",)),
        name="attn_outproj",
    )(x2d, diff_o, o_cmp, o_sel, o_win, gates, ge, _bf(w_out))


def _hybrid_attention_layer(x, layer, norm_gain, w_in, gate_bias, dq_gain, dk_gain, lam_p, subln, nq_gain, nk_gain,
                            cmp_pe, cmp_w1, cmp_w2, w_out):
    B, S, D = x.shape
    T = B * S
    tq = min(256, S)
    qd_t, kd, vd_t, qn_t, cmp_kv, ks, vs_t, kw, vw_t, gates = _attn_inproj(
        x, norm_gain, w_in, gate_bias, dq_gain, dk_gain, nq_gain, nk_gain, tq)
    lambda_init = 0.8 - 0.6 * math.exp(-0.3 * layer)
    log2e = math.log2(math.e)
    dq_scaled = dq_gain * (DIFF_QK_DIM ** -0.5 * log2e)
    nq_scaled = nq_gain * (HEAD_DIM ** -0.5 * log2e)
    diff_o = _diff_attention(qd_t, kd, vd_t, lam_p, subln, _score_bound(dq_scaled, dk_gain, DIFF_QK_DIM),
                             lambda_init, tq)
    kc, vc_t = _nsa_compress(cmp_kv, cmp_pe, cmp_w1, cmp_w2, nk_gain[0])
    o_cmp, sel = _nsa_cmp_attention(qn_t, kc, vc_t, tq)
    o_sel, o_win = _nsa_sel_win(qn_t, ks, vs_t, kw, vw_t, sel, _score_bound(nq_scaled, nk_gain[1], HEAD_DIM),
                                _score_bound(nq_scaled, nk_gain[2], HEAD_DIM), tq)
    flat = lambda a: a.reshape(T, a.shape[-1])
    out = _attn_outproj(flat(x), flat(diff_o), flat(o_cmp), flat(o_sel), flat(o_win), flat(gates), w_out)
    return out.reshape(B, S, D)


def _softplus(z):
    return jnp.maximum(z, 0.0) + jnp.log(1.0 + jnp.exp(-jnp.abs(z)))


def _head_pool(d, head):
    member = (jnp.arange(d)[:, None] // head == jnp.arange(LANES)[None, :]).astype(BF16)
    return member, member.T


def _head_sum(parts, pool, expand):
    sums = _dot(parts[0], pool)
    for part in parts[1:]:
        sums = sums + _dot(part, pool)
    hi, lo = _split2(sums)
    return _dot(hi, expand) + _dot(lo, expand)


def _rwkv_proj_kernel(x_ref, xp_ref, g_ref, mix_ref, wr_ref, wk_ref, wv_ref, w1_ref, w2_ref, a1_ref, a2_ref,
                      g1_ref, g2_ref, vec_ref, pool_ref, expand_ref,
                      r_ref, wl_ref, k_ref, v_ref, kk_ref, b_ref, bonus_ref, gate_ref):
    i = pl.program_id(1)
    gain = g_ref[...]
    xn = _rms(x_ref[0], gain)
    prev = _rms(xp_ref[0], gain)[7:8] * jnp.where(i > 0, 1.0, 0.0)
    shifted = pltpu.roll(xn, 1, 0)
    first_row = lax.broadcasted_iota(jnp.int32, xn.shape, 0) == 0
    dx = jnp.where(first_row, prev, shifted) - xn
    mix = mix_ref[...]
    xr, xw, xk, xv, xa, xg = (_bf(xn + dx * mix[j:j + 1]) for j in range(6))
    vec = vec_ref[...]
    w0, a0, k_k, k_a, r_k = (vec[j:j + 1] for j in range(5))
    r = _dot(xr, wr_ref[...])
    k = _dot(xk, wk_ref[...])
    v = _dot(xv, wv_ref[...])
    w = -_softplus(-(w0 + _dot(_bf(jnp.tanh(_dot(xw, w1_ref[...]))), w2_ref[...]))) - 0.5
    a = jax.nn.sigmoid(a0 + _dot(_bf(_dot(xa, a1_ref[...])), a2_ref[...]))
    gate_ref[0] = _dot(_bf(jax.nn.sigmoid(_dot(xg, g1_ref[...]))), g2_ref[...])
    pool = pool_ref[...]
    expand = expand_ref[...]
    kk = k * k_k
    kk = kk / jnp.maximum(jnp.sqrt(_head_sum([_bf(kk * kk)], pool, expand)), 1e-12)
    k = k * (1.0 + (a - 1.0) * k_a)
    r_ref[0] = r
    wl_ref[0] = -jnp.exp(w)
    k_ref[0] = k
    v_ref[0] = v
    kk_ref[0] = kk
    b_ref[0] = kk * a
    bonus_ref[0] = _head_sum([_bf(r * k * r_k)], pool, expand) * v


def _rwkv_proj(x, gain, mix, w_r, w_k, w_v, w0, w1, w2, a0, a1, a2, g1, g2, k_k, k_a, r_k):
    B, S, D = x.shape
    tm = min(256, S)
    pad_c = lambda m, n: _bf(jnp.pad(m, ((0, 0), (0, n - m.shape[1]))))
    pad_r = lambda m, n: _bf(jnp.pad(m, ((0, n - m.shape[0]), (0, 0))))
    lw = LANES
    lg = 2 * LANES
    consts = (gain.reshape(1, D), jnp.pad(mix, ((0, 2), (0, 0))), _bf(w_r), _bf(w_k), _bf(w_v),
              pad_c(w1, lw), pad_r(w2, lw), pad_c(a1, lw), pad_r(a2, lw), pad_c(g1, lg), pad_r(g2, lg),
              jnp.pad(jnp.stack([w0, a0, k_k, k_a, r_k]), ((0, 3), (0, 0)))) + _head_pool(D, RWKV_HEAD)
    full = lambda a: pl.BlockSpec(a.shape, lambda b, i: (0,) * a.ndim)
    tile = pl.BlockSpec((1, tm, D), lambda b, i: (b, i, 0))
    return pl.pallas_call(
        _rwkv_proj_kernel,
        grid=(B, S // tm),
        in_specs=[tile, pl.BlockSpec((1, 8, D), lambda b, i: (b, jnp.maximum(i * (tm // 8) - 1, 0), 0))]
        + [full(a) for a in consts],
        out_specs=[tile] * 8,
        out_shape=[jax.ShapeDtypeStruct((B, S, D), F32)] * 8,
        compiler_params=_params(("arbitrary", "arbitrary")),
        name="rwkv_proj",
    )(x, x, *consts)


def _rwkv_chunk_kernel(r_ref, wl_ref, k_ref, v_ref, kk_ref, b_ref, m_ref, g0_ref, rq_ref, y0_ref, *, cpb):
    C = RWKV_CHUNK
    lane = lax.broadcasted_iota(jnp.int32, (C, LANES), 1)
    head0 = lane < RWKV_HEAD
    ti = lax.broadcasted_iota(jnp.int32, (2 * C, 2 * C), 0) & (C - 1)
    tj = lax.broadcasted_iota(jnp.int32, (2 * C, 2 * C), 1) & (C - 1)
    strict = ti > tj
    incl = ti >= tj
    eye = lax.broadcasted_iota(jnp.int32, (LANES, LANES), 0) == lax.broadcasted_iota(jnp.int32, (LANES, LANES), 1)

    def stack(x):
        return jnp.concatenate([jnp.where(head0, x, 0.0), jnp.where(head0, 0.0, x)], axis=0)

    chunks = range(cpb)
    rows = [slice(c * C, (c + 1) * C) for c in chunks]
    wl = [wl_ref[0, rows[c], :] for c in chunks]
    row = lax.broadcasted_iota(jnp.int32, (C, LANES), 0)
    cum = []
    for c in chunks:
        acc = wl[c]
        for step in (1, 2, 4, 8, 16, 32):
            acc = acc + jnp.where(row >= step, pltpu.roll(acc, step, 0), 0.0)
        cum.append(acc)
    total = [cum[c][C - 1:C, :] for c in chunks]
    lhs_kk, lhs_r, vs, a = [], [], [], []
    for c in chunks:
        p_inv = jnp.exp(-cum[c])
        lhs_kk.append(stack(kk_ref[0, rows[c], :] * jnp.exp(cum[c] - wl[c])))
        lhs_r.append(stack(r_ref[0, rows[c], :] * jnp.exp(cum[c])))
        vs.append(_bf(stack(v_ref[0, rows[c], :])))
        a.append(_dot_nt(_bf(jnp.concatenate([lhs_kk[c], lhs_r[c]], axis=0)),
                         _bf(jnp.concatenate([stack(b_ref[0, rows[c], :] * p_inv),
                                              stack(k_ref[0, rows[c], :] * p_inv)], axis=0))))
    npow = [_bf(jnp.where(strict, a[c][0:2 * C, 0:2 * C], 0.0)) for c in chunks]
    a_k = [_bf(jnp.where(strict, a[c][0:2 * C, 2 * C:4 * C], 0.0)) for c in chunks]
    a_rb = [_bf(jnp.where(incl, a[c][2 * C:4 * C, 0:2 * C], 0.0)) for c in chunks]
    a_rk = [_bf(jnp.where(incl, a[c][2 * C:4 * C, 2 * C:4 * C], 0.0)) for c in chunks]
    x = [jnp.concatenate([_dot(a_k[c], vs[c]), lhs_kk[c]], axis=1) for c in chunks]
    x = [x[c] - _dot(npow[c], _bf(x[c])) for c in chunks]
    for _ in range(5):
        npow = [_bf(_dot(npow[c], npow[c])) for c in chunks]
        x = [x[c] + _dot(npow[c], _bf(x[c])) for c in chunks]
    uw = [_bf(-x[c]) for c in chunks]
    for c in chunks:
        t = _dot(a_rb[c], uw[c])
        y0_ref[0, 0, c] = _bf(_dot(a_rk[c], vs[c]) + t[:, 0:LANES])
        rq_ref[0, 0, c] = _bf(lhs_r[c] + t[:, LANES:2 * LANES])
    for c in chunks:
        p_end = jnp.exp(total[c] - cum[c])
        bc = _bf(stack(b_ref[0, rows[c], :] * p_end))
        kc = _bf(stack(k_ref[0, rows[c], :] * p_end))
        t = _dot_tn(bc, uw[c])
        g0_ref[0, 0, c] = _bf(_dot_tn(kc, vs[c]) + t[:, 0:LANES])
        m_ref[0, 0, c] = jnp.where(eye, jnp.exp(total[c]), 0.0) + t[:, LANES:2 * LANES]


def _rwkv_chunks(r, wl, k, v, kk, b):
    B, S, D = r.shape
    C = RWKV_CHUNK
    nc = S // C
    cpb = min(8, nc)
    hp = D // LANES
    tile = pl.BlockSpec((1, cpb * C, LANES), lambda bi, h, c: (bi, c, h))
    out = pl.BlockSpec((1, 1, cpb, LANES, LANES), lambda bi, h, c: (bi, h, c, 0, 0))
    return pl.pallas_call(
        functools.partial(_rwkv_chunk_kernel, cpb=cpb),
        grid=(B, hp, nc // cpb),
        in_specs=[tile] * 6,
        out_specs=[out] * 4,
        out_shape=[jax.ShapeDtypeStruct((B, hp, nc, LANES, LANES), F32)]
        + [jax.ShapeDtypeStruct((B, hp, nc, LANES, LANES), BF16)] * 3,
        compiler_params=_params(("arbitrary", "arbitrary", "arbitrary")),
        name="rwkv_chunk_summaries",
    )(r, wl, k, v, kk, b)


def _rwkv_scan_kernel(m_ref, g0_ref, rq_ref, y0_ref, y_ref, st_ref, *, cpb, hp):
    C = RWKV_CHUNK

    @pl.when(pl.program_id(1) == 0)
    def _():
        st_ref[...] = jnp.zeros_like(st_ref)

    for c in range(cpb):
        for h in range(hp):
            st = st_ref[h]
            s_hi, s_lo = _split2(st)
            y = y0_ref[0, h, c].astype(F32) + _dot(rq_ref[0, h, c], s_hi)
            y_ref[0, c * C:(c + 1) * C, h * LANES:(h + 1) * LANES] = y[0:C] + y[C:2 * C]
            m_hi, m_lo = _split2(m_ref[0, h, c])
            st_ref[h] = g0_ref[0, h, c] + _dot(m_hi, s_hi) + _dot(m_hi, s_lo) + _dot(m_lo, s_hi)


def _rwkv_scan(m, g0, rq, y0, S):
    B, hp, nc = m.shape[:3]
    C = RWKV_CHUNK
    cpb = min(4, nc)
    blk = pl.BlockSpec((1, hp, cpb, LANES, LANES), lambda bi, c: (bi, 0, c, 0, 0))
    return pl.pallas_call(
        functools.partial(_rwkv_scan_kernel, cpb=cpb, hp=hp),
        grid=(B, nc // cpb),
        in_specs=[blk] * 4,
        out_specs=pl.BlockSpec((1, cpb * C, hp * LANES), lambda bi, c: (bi, c, 0)),
        out_shape=jax.ShapeDtypeStruct((B, S, hp * LANES), F32),
        scratch_shapes=[pltpu.VMEM((hp, LANES, LANES), F32)],
        compiler_params=_params(("arbitrary", "arbitrary")),
        name="rwkv_state_scan",
    )(m, g0, rq, y0)


def _rwkv_out_kernel(x_ref, y_ref, bonus_ref, gate_ref, lnw_ref, lnb_ref, pool_ref, expand_ref, wo_ref, o_ref):
    y = y_ref[...]
    pool = pool_ref[...]
    expand = expand_ref[...]
    mu = _head_sum(_split2(y), pool, expand) * (1.0 / RWKV_HEAD)
    dev = y - mu
    var = _head_sum(_split2(dev * dev), pool, expand) * (1.0 / RWKV_HEAD)
    yn = dev * lax.rsqrt(var + RWKV_LNX_EPS) * lnw_ref[...] + lnb_ref[...]
    z = (yn + bonus_ref[...]) * gate_ref[...]
    o_ref[...] = x_ref[...] + _dot(_bf(z), wo_ref[...])


def _rwkv_out(x2d, y, bonus, gate, ln_w, ln_b, w_o):
    T, D = x2d.shape
    tm = min(256, T)
    row = pl.BlockSpec((tm, D), lambda i: (i, 0))
    vecs = pl.BlockSpec((1, D), lambda i: (0, 0))
    full = lambda shp: pl.BlockSpec(shp, lambda i: (0, 0))
    pool, expand = _head_pool(D, RWKV_HEAD)
    return pl.pallas_call(
        _rwkv_out_kernel,
        grid=(T // tm,),
        in_specs=[row, row, row, row, vecs, vecs, full((D, LANES)), full((LANES, D)), full((D, D))],
        out_specs=row,
        out_shape=jax.ShapeDtypeStruct((T, D), F32),
        compiler_params=_params(("arbitrary",)),
        name="rwkv_out",
    )(x2d, y, bonus, gate, ln_w.reshape(1, D), ln_b.reshape(1, D), pool, expand, _bf(w_o))


def _rwkv_layer(x, norm_gain, mix, w_r, w_k, w_v, w0, w1, w2, a0, a1, a2, g1, g2, k_k, k_a, r_k, ln_w, ln_b, w_o):
    B, S, D = x.shape
    r, wl, k, v, kk, b, bonus, gate = _rwkv_proj(x, norm_gain, mix, w_r, w_k, w_v, w0, w1, w2, a0, a1, a2,
                                                 g1, g2, k_k, k_a, r_k)
    m, g0, rq, y0 = _rwkv_chunks(r, wl, k, v, kk, b)
    y = _rwkv_scan(m, g0, rq, y0, S)
    T = B * S
    out = _rwkv_out(x.reshape(T, D), y.reshape(T, D), bonus.reshape(T, D), gate.reshape(T, D), ln_w, ln_b, w_o)
    return out.reshape(B, S, D)


def _router_kernel(x_ref, g_ref, whi_ref, wlo_ref, b_ref, xn_ref, route_ref):
    xn = _rms(x_ref[...], g_ref[...])
    xn_ref[...] = xn
    hi, lo = _split2(xn)
    logits = _dot(hi, whi_ref[...]) + _dot(hi, wlo_ref[...]) + _dot(lo, whi_ref[...]) + b_ref[...]
    lane = lax.broadcasted_iota(jnp.int32, logits.shape, 1)
    lane_f = lane.astype(F32)

    def top(vals):
        mx = jnp.max(vals, axis=-1, keepdims=True)
        return mx, jnp.min(jnp.where(vals == mx, lane_f, float(LANES)), axis=-1, keepdims=True)

    glog = jnp.where(lane < N_GROUPS, logits, NEG_INF)
    gmax, gidx = top(glog)
    gsum = jnp.sum(jnp.where(lane < N_GROUPS, jnp.exp(glog - gmax), 0.0), axis=-1, keepdims=True)
    grp_p = 1.0 / gsum
    first = float(N_GROUPS) + EXPERTS_PER_GROUP * gidx
    elog = jnp.where(lane_f >= first, jnp.where(lane_f < first + EXPERTS_PER_GROUP, logits, NEG_INF), NEG_INF)
    v1, i1 = top(elog)
    v2, i2 = top(jnp.where(lane_f == i1, NEG_INF, elog))
    e2 = jnp.exp(v2 - v1)
    gate1 = grp_p / (1.0 + e2)
    gate2 = grp_p * e2 / (1.0 + e2)
    route_ref[...] = jnp.where(lane == 0, i1 - N_GROUPS, jnp.where(lane == 1, i2 - N_GROUPS,
                               jnp.where(lane == 2, gate1, jnp.where(lane == 3, gate2, 0.0))))


def _router(x2d, gain, wg, bg, we, be):
    T, D = x2d.shape
    tm = min(256, T)
    w = jnp.pad(jnp.concatenate([wg, we], axis=1), ((0, 0), (0, LANES - N_GROUPS - N_EXPERTS)))
    w_hi = _bf(w)
    w_lo = _bf(w - w_hi.astype(F32))
    bias = jnp.pad(jnp.concatenate([bg, be]), (0, LANES - N_GROUPS - N_EXPERTS)).reshape(1, LANES)
    row = lambda n: pl.BlockSpec((tm, n), lambda i: (i, 0))
    full = lambda a: pl.BlockSpec(a.shape, lambda i: (0, 0))
    consts = (gain.reshape(1, D), w_hi, w_lo, bias)
    return pl.pallas_call(
        _router_kernel,
        grid=(T // tm,),
        in_specs=[row(D)] + [full(a) for a in consts],
        out_specs=[row(D), row(LANES)],
        out_shape=[jax.ShapeDtypeStruct((T, D), F32), jax.ShapeDtypeStruct((T, LANES), F32)],
        compiler_params=_params(("arbitrary",)),
        name="moe_router",
    )(x2d, *consts)


def _slot_tokens_kernel(dest_ref, slot_ref):
    def clear(s, carry):
        slot_ref[s] = 0
        return carry
    lax.fori_loop(0, slot_ref.shape[0], clear, 0, unroll=16)

    def place(p, carry):
        slot_ref[dest_ref[p]] = lax.shift_right_logical(p, 1)
        return carry
    lax.fori_loop(0, dest_ref.shape[0], place, 0, unroll=16)


def _slot_tokens(dest, n_slots):
    return pl.pallas_call(
        _slot_tokens_kernel,
        in_specs=[pl.BlockSpec(memory_space=pltpu.SMEM)],
        out_specs=pl.BlockSpec(memory_space=pltpu.SMEM),
        out_shape=jax.ShapeDtypeStruct((n_slots,), jnp.int32),
        name="moe_slot_tokens",
    )(dest)


def _row_gather_start(src_hbm, idx_ref, idx_base, idx_stride, dst, sem, rows, first=0):
    for r in range(first, first + rows):
        src_row = idx_ref[idx_base + r * idx_stride]
        pltpu.make_async_copy(src_hbm.at[pl.ds(src_row, 1)], dst.at[pl.ds(r, 1)], sem).start(priority=r % 2)


def _row_gather_wait(src_hbm, dst, sem, rows):
    pltpu.make_async_copy(src_hbm.at[pl.ds(0, rows)], dst, sem).wait()


def _expert_kernel(be_ref, nu_ref, tok_ref, xn_hbm, wg_ref, wu_ref, wd_ref, o_ref, xbuf, sem):
    i = pl.program_id(0)
    nu = nu_ref[0]
    last = pl.num_programs(0) - 1
    cur = lax.rem(i, MOE_BUFFERS)
    nxt1 = lax.rem(i + 1, MOE_BUFFERS)
    nxt2 = lax.rem(i + 2, MOE_BUFFERS)

    @pl.when(i == 0)
    def _():
        for blk in range(2):
            _row_gather_start(xn_hbm, tok_ref, blk * MOE_ROWS, 1, xbuf.at[blk], sem.at[blk], MOE_ROWS)

    @pl.when(i < nu)
    def _():
        _row_gather_wait(xn_hbm, xbuf.at[cur], sem.at[cur], MOE_ROWS)
        ahead = jnp.minimum(i + 2, last)
        d_model = xbuf.shape[-1]
        pieces = 2 * (d_model // MXU_TILE) + d_model // MXU_TILE
        per_piece = -(-MOE_ROWS // pieces)
        started = [0]

        def start_some():
            rows = min(per_piece, MOE_ROWS - started[0])
            _row_gather_start(xn_hbm, tok_ref, ahead * MOE_ROWS, 1, xbuf.at[nxt2], sem.at[nxt2], rows, started[0])
            started[0] += rows

        gate = jnp.zeros((MOE_ROWS, D_EXPERT), F32)
        up = jnp.zeros((MOE_ROWS, D_EXPERT), F32)
        for c in range(d_model // MXU_TILE):
            cols = slice(c * MXU_TILE, (c + 1) * MXU_TILE)
            xc = _bf(xbuf[cur, :, cols])
            gate = gate + _dot(xc, _bf(wg_ref[0, 0, cols, :]))
            start_some()
            up = up + _dot(xc, _bf(wu_ref[0, 0, cols, :]))
            start_some()
        hid = _bf(jax.nn.silu(gate) * up)
        for c in range(d_model // MXU_TILE):
            cols = slice(c * MXU_TILE, (c + 1) * MXU_TILE)
            y = _dot(hid, _bf(wd_ref[0, 0, :, cols]))
            start_some()
            o_ref[:, cols] = y
        assert started[0] == MOE_ROWS

        @pl.when(i == nu - 1)
        def _():
            _row_gather_wait(xn_hbm, xbuf.at[nxt1], sem.at[nxt1], MOE_ROWS)
            _row_gather_wait(xn_hbm, xbuf.at[nxt2], sem.at[nxt2], MOE_ROWS)

    @pl.when(i >= nu)
    def _():
        o_ref[...] = jnp.zeros_like(o_ref)


def _experts(xn, slot_tok, block_e, n_used, layer, e_gate, e_up, e_down):
    T, D = xn.shape
    n_blocks = slot_tok.shape[0] // MOE_ROWS
    wspec = lambda shp: pl.BlockSpec((1, 1) + shp, lambda i, be, nu, tok: (layer, be[i], 0, 0))
    return pl.pallas_call(
        _expert_kernel,
        grid_spec=pltpu.PrefetchScalarGridSpec(
            num_scalar_prefetch=3,
            grid=(n_blocks,),
            in_specs=[pl.BlockSpec(memory_space=pl.ANY), wspec((D, D_EXPERT)), wspec((D, D_EXPERT)),
                      wspec((D_EXPERT, D))],
            out_specs=pl.BlockSpec((MOE_ROWS, D), lambda i, be, nu, tok: (i, 0)),
            scratch_shapes=[pltpu.VMEM((MOE_BUFFERS, MOE_ROWS, D), F32), pltpu.SemaphoreType.DMA((MOE_BUFFERS,))],
        ),
        out_shape=jax.ShapeDtypeStruct((n_blocks * MOE_ROWS, D), F32),
        compiler_params=_params(("arbitrary",)),
        name="moe_experts",
    )(block_e, n_used, slot_tok, xn, e_gate, e_up, e_down)


def _combine_kernel(dest_ref, x_ref, route_ref, ys_hbm, o_ref, buf, sem, *, tm):
    i = pl.program_id(0)
    cur = i & 1

    def start(tile, par):
        for k in range(2):
            _row_gather_start(ys_hbm, dest_ref, tile * (2 * tm) + k, 2, buf.at[par, k], sem.at[par], tm)

    @pl.when(i == 0)
    def _():
        start(0, 0)

    @pl.when(i + 1 < pl.num_programs(0))
    def _():
        start(i + 1, 1 - cur)

    for k in range(2):
        _row_gather_wait(ys_hbm, buf.at[cur, k], sem.at[cur], tm)
    route = route_ref[...]
    o_ref[...] = x_ref[...] + route[:, 2:3] * buf[cur, 0] + route[:, 3:4] * buf[cur, 1]


def _combine(x2d, y_slots, dest, route):
    T, D = x2d.shape
    tm = min(256, T)
    row = lambda n: pl.BlockSpec((tm, n), lambda i, dest: (i, 0))
    return pl.pallas_call(
        functools.partial(_combine_kernel, tm=tm),
        grid_spec=pltpu.PrefetchScalarGridSpec(
            num_scalar_prefetch=1,
            grid=(T // tm,),
            in_specs=[row(D), row(LANES), pl.BlockSpec(memory_space=pl.ANY)],
            out_specs=row(D),
            scratch_shapes=[pltpu.VMEM((2, 2, tm, D), F32), pltpu.SemaphoreType.DMA((2,))],
        ),
        out_shape=jax.ShapeDtypeStruct((T, D), F32),
        compiler_params=_params(("arbitrary",)),
        name="moe_combine",
    )(dest, x2d, route, y_slots)


def _moe_layer(x, gain, wg, bg, we, be, layer, e_gate, e_up, e_down):
    B, S, D = x.shape
    T = B * S
    x2d = x.reshape(T, D)
    xn, route = _router(x2d, gain, wg, bg, we, be)
    flat_e = route[:, 0:2].astype(jnp.int32).reshape(-1)
    n_pairs = 2 * T
    onehot = (flat_e[:, None] == jnp.arange(N_EXPERTS, dtype=jnp.int32)[None, :]).astype(jnp.int32)
    csum = jnp.cumsum(onehot, axis=0)
    rank = jnp.take_along_axis(csum, flat_e[:, None], axis=1)[:, 0] - 1
    counts = csum[-1]
    padded = (counts + MOE_ROWS - 1) // MOE_ROWS * MOE_ROWS
    pad_end = jnp.cumsum(padded)
    dest = (pad_end - padded)[flat_e] + rank
    n_blocks = -(-n_pairs // MOE_ROWS) + N_EXPERTS
    dest = dest.astype(jnp.int32)
    slot_tok = _slot_tokens(dest, n_blocks * MOE_ROWS)
    block_start = jnp.arange(n_blocks, dtype=jnp.int32) * MOE_ROWS
    block_e = jnp.minimum(jnp.sum((pad_end[None, :] <= block_start[:, None]).astype(jnp.int32), axis=1),
                          N_EXPERTS - 1)
    n_used = (pad_end[-1:] // MOE_ROWS).astype(jnp.int32)
    y_slots = _experts(xn, slot_tok, block_e, n_used, layer, e_gate, e_up, e_down)
    return _combine(x2d, y_slots, dest, route).reshape(B, S, D)


def kernel(x, mix_norm, attn_w_in, attn_gate_bias, diff_q_norm, diff_k_norm, diff_lambda, diff_subln, nsa_q_norm, nsa_k_norm, nsa_cmp_pe, nsa_cmp_w1, nsa_cmp_w2, attn_w_out, rwkv_mix, rwkv_w_r, rwkv_w_k, rwkv_w_v, rwkv_decay_w0, rwkv_decay_w1, rwkv_decay_w2, rwkv_iclr_a0, rwkv_iclr_a1, rwkv_iclr_a2, rwkv_gate_g1, rwkv_gate_g2, rwkv_k_k, rwkv_k_a, rwkv_r_k, rwkv_ln_w, rwkv_ln_b, rwkv_w_o, ffn_norm, router_group_w, router_group_b, router_expert_w, router_expert_b, expert_w_gate, expert_w_up, expert_w_down):
    depth = mix_norm.shape[0]
    for layer in range(depth):
        i = layer // 2
        if layer % 2 == 0:
            x = _hybrid_attention_layer(
                x, layer, mix_norm[layer], attn_w_in[i], attn_gate_bias[i], diff_q_norm[i], diff_k_norm[i],
                diff_lambda[i], diff_subln[i], nsa_q_norm[i], nsa_k_norm[i], nsa_cmp_pe[i], nsa_cmp_w1[i],
                nsa_cmp_w2[i], attn_w_out[i])
        else:
            x = _rwkv_layer(
                x, mix_norm[layer], rwkv_mix[i], rwkv_w_r[i], rwkv_w_k[i], rwkv_w_v[i], rwkv_decay_w0[i],
                rwkv_decay_w1[i], rwkv_decay_w2[i], rwkv_iclr_a0[i], rwkv_iclr_a1[i], rwkv_iclr_a2[i],
                rwkv_gate_g1[i], rwkv_gate_g2[i], rwkv_k_k[i], rwkv_k_a[i], rwkv_r_k[i], rwkv_ln_w[i],
                rwkv_ln_b[i], rwkv_w_o[i])
        x = _moe_layer(x, ffn_norm[layer], router_group_w[layer], router_group_b[layer], router_expert_w[layer],
                       router_expert_b[layer], layer, expert_w_gate, expert_w_up, expert_w_down)
    return x
```

```python
import functools
import math

import jax
import jax.numpy as jnp
from jax import lax
from jax.experimental import pallas as pl
from jax.experimental.pallas import tpu as pltpu

F32 = jnp.float32
BF16 = jnp.bfloat16

D_MODEL = 1024
HEAD_DIM = 64
NORM_EPS = 1e-6
NEG_INF = -1e30
FORCE_SCORE = 1e4

DIFF_HEADS = 8
DIFF_QK_DIM = 32
NSA_HEADS = 8
NSA_KV_GROUPS = 2
NSA_HPG = 4
NSA_CMP_BLOCK = 32
NSA_CMP_STRIDE = 16
NSA_SEL_BLOCK = 64
NSA_TOP_N = 16
NSA_WINDOW = 512
IN_COLS = 2840
IN_COLS_PAD = 2944
GATE_COLS = 24

RWKV_HEAD = 64
RWKV_LNX_EPS = 64e-5
RWKV_CHUNK = 64

N_GROUPS = 4
EXPERTS_PER_GROUP = 8
N_EXPERTS = 32
D_EXPERT = 256
MOE_ROWS = 256
MOE_BUFFERS = 3

LANES = 128
MXU_TILE = 256
VMEM_LIMIT = 56 * 1024 * 1024


def _bf(x):
    return x.astype(BF16)


def _dot(a, b):
    return jnp.dot(a, b, preferred_element_type=F32)


def _dot_nt(a, b):
    return lax.dot_general(a, b, (((1,), (1,)), ((), ())), preferred_element_type=F32)


def _dot_tn(a, b):
    return lax.dot_general(a, b, (((0,), (0,)), ((), ())), preferred_element_type=F32)


def _split2(x):
    hi = _bf(x)
    lo = _bf(x - hi.astype(F32))
    return hi, lo


def _params(sem):
    return pltpu.CompilerParams(dimension_semantics=sem, vmem_limit_bytes=VMEM_LIMIT)


def _block_diag(n, group):
    r = jnp.arange(n) // group
    return ((r[:, None] == r[None, :]).astype(F32) / group).astype(BF16)


def _rms(x, gain):
    return x * lax.rsqrt(jnp.mean(x * x, axis=-1, keepdims=True) + NORM_EPS) * gain


def _inproj_kernel(x_ref, g_ref, wa_ref, wbt_ref, b_ref, bd32_ref, bd64_ref, gk_ref, gnk_ref, gq_ref, gnq_ref,
                   qd_ref, kd_ref, vd_ref, qn_ref, cmp_ref, ks_ref, vs_ref, kw_ref, vw_ref, gt_ref):
    xn = _bf(_rms(x_ref[0], g_ref[...]))
    ha = _dot(xn, wa_ref[...])
    hb = _dot_nt(wbt_ref[...], xn)
    bd32 = bd32_ref[...]
    bd64 = bd64_ref[...]

    def gnorm(seg, bd, gain):
        hi, lo = _split2(seg * seg)
        return seg * lax.rsqrt(_dot(hi, bd) + _dot(lo, bd) + NORM_EPS) * gain

    def gnorm_t(seg, bd, gain):
        hi, lo = _split2(seg * seg)
        return seg * lax.rsqrt(_dot(bd, hi) + _dot(bd, lo) + NORM_EPS) * gain

    kd_ref[0] = _bf(gnorm(ha[:, 0:512], bd32, gk_ref[...]))
    cmp_ref[0, 0] = ha[:, 512:640]
    cmp_ref[0, 1] = ha[:, 640:768]
    ksw = gnorm(ha[:, 768:1024], bd64[0:256, 0:256], gnk_ref[...])
    ks_ref[0] = _bf(ksw[:, 0:128])
    kw_ref[0] = _bf(ksw[:, 128:256])
    gt_ref[0] = jax.nn.sigmoid(ha[:, 1024:1152] + b_ref[...])
    qd_ref[0, 0] = _bf(gnorm_t(hb[0:512], bd32, gq_ref[...]))
    vd_ref[0, 0] = _bf(hb[512:1024])
    qn_ref[0, 0] = _bf(gnorm_t(hb[1024:1536], bd64, gnq_ref[...]))
    vs_ref[0, 0] = _bf(hb[1536:1664])
    vw_ref[0, 0] = _bf(hb[1664:1792])


def _attn_inproj(x, gain, w_in, gate_bias, dq_gain, dk_gain, nq_gain, nk_gain, tm):
    B, S, D = x.shape
    n = S // tm
    c = lambda lo, hi: w_in[:, lo:hi]
    gate_w = jnp.pad(c(2816, IN_COLS), ((0, 0), (0, LANES - GATE_COLS)))
    wa = _bf(jnp.concatenate([c(512, 1024), c(2048, 2304), c(2304, 2432), c(2560, 2688), gate_w], axis=1))
    wbt = _bf(jnp.concatenate([c(0, 512), c(1024, 1536), c(1536, 2048), c(2432, 2560), c(2688, 2816)], axis=1).T)
    bias = jnp.pad(gate_bias, (0, LANES - GATE_COLS)).reshape(1, LANES)
    log2e = math.log2(math.e)
    gk = jnp.tile(dk_gain, 16).reshape(1, 512)
    gnk = jnp.concatenate([jnp.tile(nk_gain[1], 2), jnp.tile(nk_gain[2], 2)]).reshape(1, 256)
    gq = (jnp.tile(dq_gain, 16) * (DIFF_QK_DIM ** -0.5 * log2e)).reshape(512, 1)
    gnq = (jnp.tile(nq_gain, 8) * (HEAD_DIM ** -0.5 * log2e)).reshape(512, 1)
    consts = (gain.reshape(1, D), wa, wbt, bias, _block_diag(512, 32), _block_diag(512, 64), gk, gnk, gq, gnq)
    full = lambda a: pl.BlockSpec(a.shape, lambda b, i: (0,) * a.ndim)
    tok = lambda w: pl.BlockSpec((1, tm, w), lambda b, i: (b, i, 0))
    tr = lambda r: pl.BlockSpec((1, 1, r, tm), lambda b, i: (b, i, 0, 0))
    tok_shape = lambda w, dt: jax.ShapeDtypeStruct((B, S, w), dt)
    tr_shape = lambda r: jax.ShapeDtypeStruct((B, n, r, tm), BF16)
    return pl.pallas_call(
        _inproj_kernel,
        grid=(B, n),
        in_specs=[tok(D)] + [full(a) for a in consts],
        out_specs=[tr(512), tok(512), tr(512), tr(512), pl.BlockSpec((1, 2, tm, LANES), lambda b, i: (b, 0, i, 0)),
                   tok(LANES), tr(LANES), tok(LANES), tr(LANES), tok(LANES)],
        out_shape=[tr_shape(512), tok_shape(512, BF16), tr_shape(512), tr_shape(512),
                   jax.ShapeDtypeStruct((B, 2, S, LANES), F32),
                   tok_shape(LANES, BF16), tr_shape(LANES), tok_shape(LANES, BF16), tr_shape(LANES),
                   tok_shape(LANES, F32)],
        compiler_params=_params(("arbitrary", "arbitrary")),
        name="attn_inproj",
    )(x, *consts)


SHIFT_LIMIT = 40.0


def _score_bound(q_gain, k_gain, dim):
    bound = dim * jnp.max(jnp.abs(q_gain)) * jnp.max(jnp.abs(k_gain)) * 1.02 + 0.25
    return jnp.stack([bound, (bound <= SHIFT_LIMIT).astype(F32)]).astype(F32)


def _flash_pass(lo, hi, scores, values, mask_body, mask_last, s_ref, p_ref, shift):
    tk, cols = s_ref.shape
    dv = HEAD_DIM
    p_ref[...] = jnp.zeros_like(p_ref)
    s_ref[...] = scores(lo)

    def process(j, carry, mask, s_next):
        s = s_ref[...]
        if mask is not None:
            s = mask(j, s)
        pv = values(jnp.maximum(j - 1, lo), p_ref[...])
        if shift is None:
            m, l, acc = carry
            m_new = jnp.maximum(m, jnp.max(s, axis=0, keepdims=True))
            alpha = jnp.exp2(m - m_new)
            p = jnp.exp2(s - m_new)
            carry = (m_new, alpha * l + jnp.sum(p, axis=0, keepdims=True), (acc + pv) * alpha)
        else:
            l, acc = carry
            p = jnp.exp2(s - shift)
            carry = (l + jnp.sum(p, axis=0, keepdims=True), acc + pv)
        p_ref[...] = _bf(p)
        if s_next is not None:
            s_ref[...] = s_next
        return carry

    init = (jnp.zeros((1, cols), F32), jnp.zeros((dv, cols), F32))
    if shift is None:
        init = (jnp.full((1, cols), NEG_INF, F32),) + init
    carry = lax.fori_loop(lo, hi, lambda j, c: process(j, c, mask_body, scores(j + 1)), init)
    l, acc = process(hi, carry, mask_last, None)[-2:]
    return (acc + values(hi, p_ref[...])) / l


def _flash_tiles(lo, hi, scores, values, mask_body, mask_last, s_ref, p_ref, o_ref, bound_ref):
    @pl.when(bound_ref[1] > 0.5)
    def _():
        o_ref[...] = _flash_pass(lo, hi, scores, values, mask_body, mask_last, s_ref, p_ref, bound_ref[0])

    @pl.when(bound_ref[1] <= 0.5)
    def _():
        o_ref[...] = _flash_pass(lo, hi, scores, values, mask_body, mask_last, s_ref, p_ref, None)

    return o_ref[...]


def _diff_attn_kernel(qt_ref, k_ref, vt_ref, lam_ref, sg_ref, bound_ref, o_ref, s_ref, p_ref, acc_ref, *,
                      tq, lambda_init):
    lp = lam_ref[...]
    lam = (jnp.exp(jnp.sum(lp[0:1] * lp[1:2], axis=-1, keepdims=True))
           - jnp.exp(jnp.sum(lp[2:3] * lp[3:4], axis=-1, keepdims=True)) + lambda_init)

    def causal(j, s):
        kpos = lax.broadcasted_iota(jnp.int32, (tq, 4 * tq), 0)
        qpos = lax.broadcasted_iota(jnp.int32, (tq, 4 * tq), 1) & (tq - 1)
        return jnp.where(kpos <= qpos, s, NEG_INF)

    def values(j, p):
        vt = vt_ref[0, j]
        return jnp.concatenate([_dot(vt[0:HEAD_DIM], p[:, 0:2 * tq]),
                                _dot(vt[HEAD_DIM:2 * HEAD_DIM], p[:, 2 * tq:4 * tq])], axis=1)

    def query_tile(i, carry):
        q = qt_ref[0, i]
        zero = jnp.zeros((DIFF_QK_DIM, tq), BF16)
        qb = jnp.concatenate(
            [jnp.concatenate([q[r * DIFF_QK_DIM:(r + 1) * DIFF_QK_DIM] if c == r else zero for c in range(4)],
                             axis=1) for r in range(4)], axis=0)
        o = _flash_tiles(0, i, lambda j: _dot(k_ref[0, j], qb), values, None, causal, s_ref, p_ref, acc_ref,
                         bound_ref)
        heads = []
        for h in range(2):
            oh = o[:, 2 * h * tq:(2 * h + 1) * tq] - lam * o[:, (2 * h + 1) * tq:(2 * h + 2) * tq]
            oh = oh * lax.rsqrt(jnp.mean(oh * oh, axis=0, keepdims=True) + NORM_EPS) * sg_ref[...]
            heads.append(oh * (1.0 - lambda_init))
        o_ref[0, pl.ds(pl.multiple_of(i * tq, tq), tq), :] = jnp.concatenate(heads, axis=0).T
        return carry

    lax.fori_loop(0, qt_ref.shape[1], query_tile, 0)


def _diff_attention(qd_t, kd, vd_t, lam_p, subln, bound, lambda_init, tq):
    B, n = qd_t.shape[:2]
    S = n * tq
    pairs = DIFF_HEADS // 2
    return pl.pallas_call(
        functools.partial(_diff_attn_kernel, tq=tq, lambda_init=lambda_init),
        grid=(B, pairs),
        in_specs=[
            pl.BlockSpec((1, n, LANES, tq), lambda b, h: (b, 0, h, 0)),
            pl.BlockSpec((1, n, tq, LANES), lambda b, h: (b, 0, 0, h)),
            pl.BlockSpec((1, n, LANES, tq), lambda b, h: (b, 0, h, 0)),
            pl.BlockSpec((4, DIFF_QK_DIM), lambda b, h: (0, 0)),
            pl.BlockSpec((HEAD_DIM, 1), lambda b, h: (0, 0)),
            pl.BlockSpec(memory_space=pltpu.SMEM),
        ],
        out_specs=pl.BlockSpec((1, S, LANES), lambda b, h: (b, 0, h)),
        out_shape=jax.ShapeDtypeStruct((B, S, DIFF_HEADS * HEAD_DIM), F32),
        scratch_shapes=[pltpu.VMEM((tq, 4 * tq), F32), pltpu.VMEM((tq, 4 * tq), BF16),
                        pltpu.VMEM((HEAD_DIM, 4 * tq), F32)],
        compiler_params=_params(("arbitrary", "arbitrary")),
        name="diff_attention",
    )(qd_t, kd.reshape(B, n, tq, DIFF_HEADS * HEAD_DIM), vd_t, lam_p, subln.reshape(HEAD_DIM, 1), bound)


def _nsa_compress_kernel(t_ref, pe_ref, w1_ref, w2_ref, bd_ref, kg_ref, kc_ref, vct_ref, *, nch):
    half = NSA_CMP_BLOCK // 2
    ya = jnp.zeros((nch, 2 * LANES), F32)
    yb = jnp.zeros((nch, 2 * LANES), F32)
    for l in range(half):
        rows = pl.ds(l, nch, stride=NSA_CMP_STRIDE)
        piece = jnp.concatenate([t_ref[0, 0, rows, :], t_ref[0, 1, rows, :]], axis=1)
        ya = ya + _dot(_bf(piece + pe_ref[l:l + 1]), w1_ref[l])
        yb = yb + _dot(_bf(piece + pe_ref[half + l:half + l + 1]), w1_ref[half + l])
    pre = ya + pltpu.roll(yb, nch - 1, 0)
    out = _dot(_bf(jax.nn.gelu(pre)), w2_ref[...])
    k = out[:, 0:LANES]
    hi, lo = _split2(k * k)
    ms = _dot(hi, bd_ref[...]) + _dot(lo, bd_ref[...])
    kc_ref[0] = _bf(k * lax.rsqrt(ms + NORM_EPS) * kg_ref[...])
    vct_ref[0] = _bf(out[:, LANES:2 * LANES].T)


def _nsa_compress(cmp_kv, cmp_pe, cmp_w1, cmp_w2, k_gain0):
    B, _, S, _ = cmp_kv.shape
    nch = S // NSA_CMP_STRIDE

    def over_groups(w):
        parts = [w[0], w[0], w[1], w[1]]
        zero = jnp.zeros_like(w[0])
        return jnp.concatenate(
            [jnp.concatenate([parts[r] if c == r else zero for c in range(4)], axis=-1) for r in range(4)], axis=-2)

    w1 = _bf(over_groups(cmp_w1.reshape(2, NSA_CMP_BLOCK, HEAD_DIM, HEAD_DIM)))
    w2 = _bf(over_groups(cmp_w2))
    pe = jnp.concatenate([cmp_pe[0], cmp_pe[0], cmp_pe[1], cmp_pe[1]], axis=-1)
    full = lambda a: pl.BlockSpec(a.shape, lambda b: (0,) * a.ndim)
    consts = (pe, w1, w2, _block_diag(LANES, HEAD_DIM), jnp.tile(k_gain0, 2).reshape(1, LANES))
    return pl.pallas_call(
        functools.partial(_nsa_compress_kernel, nch=nch),
        grid=(B,),
        in_specs=[pl.BlockSpec((1, 2, S, LANES), lambda b: (b, 0, 0, 0))] + [full(a) for a in consts],
        out_specs=[pl.BlockSpec((1, nch, LANES), lambda b: (b, 0, 0)), pl.BlockSpec((1, LANES, nch), lambda b: (b, 0, 0))],
        out_shape=[jax.ShapeDtypeStruct((B, nch, LANES), BF16), jax.ShapeDtypeStruct((B, LANES, nch), BF16)],
        compiler_params=_params(("arbitrary",)),
        name="nsa_compress",
    )(cmp_kv, *consts)


def _group_queries(q, g):
    qg = jnp.concatenate([q[h * HEAD_DIM:(h + 1) * HEAD_DIM] for h in range(NSA_HPG)], axis=1)
    zero = jnp.zeros_like(qg)
    return jnp.where(g == 0, jnp.concatenate([qg, zero], axis=0), jnp.concatenate([zero, qg], axis=0))


def _heads_token_major(o, tq):
    return jnp.concatenate([o[:, h * tq:(h + 1) * tq] for h in range(NSA_HPG)], axis=0).T


def _nsa_cmp_attn_kernel(q_ref, k_ref, vt_ref, o_ref, sel_ref, *, tq, nch, n_sel, top_n):
    i = pl.program_id(2)
    cols = NSA_HPG * tq
    n_cmp = nch - 1
    s = _dot(k_ref[0], _group_queries(q_ref[0, 0], pl.program_id(1)))
    pos = i * tq + (lax.broadcasted_iota(jnp.int32, (nch, cols), 1) & (tq - 1))
    c = lax.broadcasted_iota(jnp.int32, (nch, cols), 0)
    ok = jnp.where(c < n_cmp, c * NSA_CMP_STRIDE + (NSA_CMP_BLOCK - 1), 1 << 30) <= pos
    s = jnp.where(ok, s, NEG_INF)
    p = jnp.where(ok, jnp.exp2(s - jnp.max(s, axis=0, keepdims=True)), 0.0)
    l = jnp.sum(p, axis=0, keepdims=True)
    p = p / jnp.where(l > 0.0, l, 1.0)
    o_ref[0] = _heads_token_major(_dot(vt_ref[0], _bf(p)), tq)

    pg = p[:, 0:tq]
    for h in range(1, NSA_HPG):
        pg = pg + p[:, h * tq:(h + 1) * tq]
    jj = lax.broadcasted_iota(jnp.int32, (n_sel, nch), 0) * NSA_SEL_BLOCK
    cc = lax.broadcasted_iota(jnp.int32, (n_sel, nch), 1) * NSA_CMP_STRIDE
    cover = jnp.where(cc < jj + NSA_SEL_BLOCK,
                      jnp.where(cc + NSA_CMP_BLOCK > jj, jnp.where(cc < n_cmp * NSA_CMP_STRIDE, 1.0, 0.0), 0.0), 0.0)
    cover = _bf(cover)
    hi, lo = _split2(pg)
    imp = _dot(cover, hi) + _dot(cover, lo)
    blk = lax.broadcasted_iota(jnp.int32, (n_sel, tq), 0)
    cur = (i * tq + lax.broadcasted_iota(jnp.int32, (n_sel, tq), 1)) >> 6
    imp = jnp.where(blk == cur, FORCE_SCORE, jnp.where(blk == 0, FORCE_SCORE, jnp.where(blk > cur, NEG_INF, imp)))
    blk_f = blk.astype(F32)
    sel = jnp.zeros((n_sel, tq), F32)
    for _ in range(top_n):
        mx = jnp.max(imp, axis=0, keepdims=True)
        first = jnp.min(jnp.where(imp == mx, blk_f, float(n_sel)), axis=0, keepdims=True)
        hit = blk_f == first
        sel = jnp.where(hit, 1.0, sel)
        imp = jnp.where(hit, -jnp.inf, imp)
    sel_ref[0, 0, 0] = _bf(sel)


def _nsa_cmp_attention(qn_t, kc, vc_t, tq):
    B, n = qn_t.shape[:2]
    S = n * tq
    G = NSA_KV_GROUPS
    nch = S // NSA_CMP_STRIDE
    n_sel = S // NSA_SEL_BLOCK
    top_n = min(NSA_TOP_N, n_sel)
    gcols = NSA_HPG * HEAD_DIM
    return pl.pallas_call(
        functools.partial(_nsa_cmp_attn_kernel, tq=tq, nch=nch, n_sel=n_sel, top_n=top_n),
        grid=(B, G, n),
        in_specs=[
            pl.BlockSpec((1, 1, gcols, tq), lambda b, g, i: (b, i, g, 0)),
            pl.BlockSpec((1, nch, LANES), lambda b, g, i: (b, 0, 0)),
            pl.BlockSpec((1, HEAD_DIM, nch), lambda b, g, i: (b, g, 0)),
        ],
        out_specs=[pl.BlockSpec((1, tq, gcols), lambda b, g, i: (b, i, g)),
                   pl.BlockSpec((1, 1, 1, n_sel, tq), lambda b, g, i: (b, g, i, 0, 0))],
        out_shape=[jax.ShapeDtypeStruct((B, S, G * gcols), F32),
                   jax.ShapeDtypeStruct((B, G, n, n_sel, tq), BF16)],
        compiler_params=_params(("arbitrary", "arbitrary", "arbitrary")),
        name="nsa_cmp_attention",
    )(qn_t, kc, vc_t)


def _nsa_sel_win_kernel(q_ref, ks_ref, vst_ref, kw_ref, vwt_ref, sel_ref, bs_ref, bw_ref, os_ref, ow_ref,
                        s_ref, p_ref, acc_ref, *, tq, n_sel):
    g = pl.program_id(1)
    kloc = lax.broadcasted_iota(jnp.int32, (tq, tq), 0)
    qloc = lax.broadcasted_iota(jnp.int32, (tq, tq), 1)
    blk_row = lax.broadcasted_iota(jnp.int32, (tq, n_sel), 0)
    blk_col = lax.broadcasted_iota(jnp.int32, (tq, n_sel), 1)

    def masked(s, keep):
        return jnp.where(jnp.concatenate([keep] * NSA_HPG, axis=1) > 0.5, s, NEG_INF)

    def query_tile(i, carry):
        qt = _group_queries(q_ref[0, i], g)
        selm = sel_ref[0, 0, i]
        qpos = i * tq + qloc
        rows = pl.ds(pl.multiple_of(i * tq, tq), tq)

        def sel_mask(j, s):
            expand = _bf(jnp.where(blk_col == ((j * tq + blk_row) >> 6), 1.0, 0.0))
            chosen = _dot(expand, selm)
            return masked(s, jnp.where(j * tq + kloc <= qpos, chosen, 0.0))

        o = _flash_tiles(0, i, lambda j: _dot(ks_ref[0, j], qt), lambda j, p: _dot(vst_ref[0, j], p),
                         sel_mask, sel_mask, s_ref, p_ref, acc_ref, bs_ref)
        os_ref[0, rows, :] = _heads_token_major(o, tq)

        def win_mask(j, s):
            dist = qpos - (j * tq + kloc)
            return masked(s, jnp.where(dist >= 0, jnp.where(dist < NSA_WINDOW, 1.0, 0.0), 0.0))

        first = jnp.maximum(i - (NSA_WINDOW - 1 + tq - 1) // tq, 0)
        o = _flash_tiles(first, i, lambda j: _dot(kw_ref[0, j], qt), lambda j, p: _dot(vwt_ref[0, j], p),
                         win_mask, win_mask, s_ref, p_ref, acc_ref, bw_ref)
        ow_ref[0, rows, :] = _heads_token_major(o, tq)
        return carry

    lax.fori_loop(0, q_ref.shape[1], query_tile, 0)


def _nsa_sel_win(qn_t, ks, vs_t, kw, vw_t, sel, bound_sel, bound_win, tq):
    B, n = qn_t.shape[:2]
    S = n * tq
    G = NSA_KV_GROUPS
    n_sel = S // NSA_SEL_BLOCK
    gcols = NSA_HPG * HEAD_DIM
    kspec = pl.BlockSpec((1, n, tq, LANES), lambda b, g: (b, 0, 0, 0))
    vtspec = pl.BlockSpec((1, n, HEAD_DIM, tq), lambda b, g: (b, 0, g, 0))
    ospec = pl.BlockSpec((1, S, gcols), lambda b, g: (b, 0, g))
    return pl.pallas_call(
        functools.partial(_nsa_sel_win_kernel, tq=tq, n_sel=n_sel),
        grid=(B, G),
        in_specs=[pl.BlockSpec((1, n, gcols, tq), lambda b, g: (b, 0, g, 0)), kspec, vtspec, kspec, vtspec,
                  pl.BlockSpec((1, 1, n, n_sel, tq), lambda b, g: (b, g, 0, 0, 0)),
                  pl.BlockSpec(memory_space=pltpu.SMEM), pl.BlockSpec(memory_space=pltpu.SMEM)],
        out_specs=[ospec, ospec],
        out_shape=[jax.ShapeDtypeStruct((B, S, G * gcols), F32)] * 2,
        scratch_shapes=[pltpu.VMEM((tq, NSA_HPG * tq), F32), pltpu.VMEM((tq, NSA_HPG * tq), BF16),
                        pltpu.VMEM((HEAD_DIM, NSA_HPG * tq), F32)],
        compiler_params=_params(("arbitrary", "arbitrary")),
        name="nsa_sel_win_attention",
    )(qn_t, ks.reshape(B, n, tq, LANES), vs_t, kw.reshape(B, n, tq, LANES), vw_t, sel, bound_sel, bound_win)


def _attn_outproj_kernel(x_ref, d_ref, oc_ref, os_ref, ow_ref, gt_ref, ge_ref, w_ref, o_ref):
    ghi, glo = _split2(gt_ref[...])

    def gate(r):
        return _dot(ghi, ge_ref[r]) + _dot(glo, ge_ref[r])

    nsa = gate(0) * oc_ref[...] + gate(1) * os_ref[...] + gate(2) * ow_ref[...]
    y = _dot(_bf(d_ref[...]), w_ref[0:512, :]) + _dot(_bf(nsa), w_ref[512:1024, :])
    o_ref[...] = x_ref[...] + y


def _attn_outproj(x2d, diff_o, o_cmp, o_sel, o_win, gates, w_out):
    T, D = x2d.shape
    tm = min(256, T)
    col = jnp.arange(512) // HEAD_DIM
    src = jnp.arange(LANES)
    ge = jnp.stack([_bf((src[:, None] == col[None, :] * 3 + r).astype(F32)) for r in range(3)])
    row = lambda n: pl.BlockSpec((tm, n), lambda i: (i, 0))
    return pl.pallas_call(
        _attn_outproj_kernel,
        grid=(T // tm,),
        in_specs=[row(D), row(512), row(512), row(512), row(512), row(LANES),
                  pl.BlockSpec((3, LANES, 512), lambda i: (0, 0, 0)),
                  pl.BlockSpec((D, D), lambda i: (0, 0))],
        out_specs=row(D),
        out_shape=jax.ShapeDtypeStruct((T, D), F32),
        compiler_params=_params(("arbitrary",)),
        name="attn_outproj",
    )(x2d, diff_o, o_cmp, o_sel, o_win, gates, ge, _bf(w_out))


def _hybrid_attention_layer(x, layer, norm_gain, w_in, gate_bias, dq_gain, dk_gain, lam_p, subln, nq_gain, nk_gain,
                            cmp_pe, cmp_w1, cmp_w2, w_out):
    B, S, D = x.shape
    T = B * S
    tq = min(256, S)
    qd_t, kd, vd_t, qn_t, cmp_kv, ks, vs_t, kw, vw_t, gates = _attn_inproj(
        x, norm_gain, w_in, gate_bias, dq_gain, dk_gain, nq_gain, nk_gain, tq)
    lambda_init = 0.8 - 0.6 * math.exp(-0.3 * layer)
    log2e = math.log2(math.e)
    dq_scaled = dq_gain * (DIFF_QK_DIM ** -0.5 * log2e)
    nq_scaled = nq_gain * (HEAD_DIM ** -0.5 * log2e)
    diff_o = _diff_attention(qd_t, kd, vd_t, lam_p, subln, _score_bound(dq_scaled, dk_gain, DIFF_QK_DIM),
                             lambda_init, tq)
    kc, vc_t = _nsa_compress(cmp_kv, cmp_pe, cmp_w1, cmp_w2, nk_gain[0])
    o_cmp, sel = _nsa_cmp_attention(qn_t, kc, vc_t, tq)
    o_sel, o_win = _nsa_sel_win(qn_t, ks, vs_t, kw, vw_t, sel, _score_bound(nq_scaled, nk_gain[1], HEAD_DIM),
                                _score_bound(nq_scaled, nk_gain[2], HEAD_DIM), tq)
    flat = lambda a: a.reshape(T, a.shape[-1])
    out = _attn_outproj(flat(x), flat(diff_o), flat(o_cmp), flat(o_sel), flat(o_win), flat(gates), w_out)
    return out.reshape(B, S, D)


def _softplus(z):
    return jnp.maximum(z, 0.0) + jnp.log(1.0 + jnp.exp(-jnp.abs(z)))


def _head_pool(d, head):
    member = (jnp.arange(d)[:, None] // head == jnp.arange(LANES)[None, :]).astype(BF16)
    return member, member.T


def _head_sum(parts, pool, expand):
    sums = _dot(parts[0], pool)
    for part in parts[1:]:
        sums = sums + _dot(part, pool)
    hi, lo = _split2(sums)
    return _dot(hi, expand) + _dot(lo, expand)


def _rwkv_proj_kernel(x_ref, xp_ref, g_ref, mix_ref, wr_ref, wk_ref, wv_ref, w1_ref, w2_ref, a1_ref, a2_ref,
                      g1_ref, g2_ref, vec_ref, pool_ref, expand_ref,
                      r_ref, wl_ref, k_ref, v_ref, kk_ref, b_ref, bonus_ref, gate_ref):
    i = pl.program_id(1)
    gain = g_ref[...]
    xn = _rms(x_ref[0], gain)
    prev = _rms(xp_ref[0], gain)[7:8] * jnp.where(i > 0, 1.0, 0.0)
    shifted = pltpu.roll(xn, 1, 0)
    first_row = lax.broadcasted_iota(jnp.int32, xn.shape, 0) == 0
    dx = jnp.where(first_row, prev, shifted) - xn
    mix = mix_ref[...]
    xr, xw, xk, xv, xa, xg = (_bf(xn + dx * mix[j:j + 1]) for j in range(6))
    vec = vec_ref[...]
    w0, a0, k_k, k_a, r_k = (vec[j:j + 1] for j in range(5))
    r = _dot(xr, wr_ref[...])
    k = _dot(xk, wk_ref[...])
    v = _dot(xv, wv_ref[...])
    w = -_softplus(-(w0 + _dot(_bf(jnp.tanh(_dot(xw, w1_ref[...]))), w2_ref[...]))) - 0.5
    a = jax.nn.sigmoid(a0 + _dot(_bf(_dot(xa, a1_ref[...])), a2_ref[...]))
    gate_ref[0] = _dot(_bf(jax.nn.sigmoid(_dot(xg, g1_ref[...]))), g2_ref[...])
    pool = pool_ref[...]
    expand = expand_ref[...]
    kk = k * k_k
    kk = kk / jnp.maximum(jnp.sqrt(_head_sum([_bf(kk * kk)], pool, expand)), 1e-12)
    k = k * (1.0 + (a - 1.0) * k_a)
    r_ref[0] = r
    wl_ref[0] = -jnp.exp(w)
    k_ref[0] = k
    v_ref[0] = v
    kk_ref[0] = kk
    b_ref[0] = kk * a
    bonus_ref[0] = _head_sum([_bf(r * k * r_k)], pool, expand) * v


def _rwkv_proj(x, gain, mix, w_r, w_k, w_v, w0, w1, w2, a0, a1, a2, g1, g2, k_k, k_a, r_k):
    B, S, D = x.shape
    tm = min(256, S)
    pad_c = lambda m, n: _bf(jnp.pad(m, ((0, 0), (0, n - m.shape[1]))))
    pad_r = lambda m, n: _bf(jnp.pad(m, ((0, n - m.shape[0]), (0, 0))))
    lw = LANES
    lg = 2 * LANES
    consts = (gain.reshape(1, D), jnp.pad(mix, ((0, 2), (0, 0))), _bf(w_r), _bf(w_k), _bf(w_v),
              pad_c(w1, lw), pad_r(w2, lw), pad_c(a1, lw), pad_r(a2, lw), pad_c(g1, lg), pad_r(g2, lg),
              jnp.pad(jnp.stack([w0, a0, k_k, k_a, r_k]), ((0, 3), (0, 0)))) + _head_pool(D, RWKV_HEAD)
    full = lambda a: pl.BlockSpec(a.shape, lambda b, i: (0,) * a.ndim)
    tile = pl.BlockSpec((1, tm, D), lambda b, i: (b, i, 0))
    return pl.pallas_call(
        _rwkv_proj_kernel,
        grid=(B, S // tm),
        in_specs=[tile, pl.BlockSpec((1, 8, D), lambda b, i: (b, jnp.maximum(i * (tm // 8) - 1, 0), 0))]
        + [full(a) for a in consts],
        out_specs=[tile] * 8,
        out_shape=[jax.ShapeDtypeStruct((B, S, D), F32)] * 8,
        compiler_params=_params(("arbitrary", "arbitrary")),
        name="rwkv_proj",
    )(x, x, *consts)


def _rwkv_chunk_kernel(r_ref, wl_ref, k_ref, v_ref, kk_ref, b_ref, m_ref, g0_ref, rq_ref, y0_ref, *, cpb):
    C = RWKV_CHUNK
    lane = lax.broadcasted_iota(jnp.int32, (C, LANES), 1)
    head0 = lane < RWKV_HEAD
    ti = lax.broadcasted_iota(jnp.int32, (2 * C, 2 * C), 0) & (C - 1)
    tj = lax.broadcasted_iota(jnp.int32, (2 * C, 2 * C), 1) & (C - 1)
    strict = ti > tj
    incl = ti >= tj
    eye = lax.broadcasted_iota(jnp.int32, (LANES, LANES), 0) == lax.broadcasted_iota(jnp.int32, (LANES, LANES), 1)

    def stack(x):
        return jnp.concatenate([jnp.where(head0, x, 0.0), jnp.where(head0, 0.0, x)], axis=0)

    chunks = range(cpb)
    rows = [slice(c * C, (c + 1) * C) for c in chunks]
    wl = [wl_ref[0, rows[c], :] for c in chunks]
    row = lax.broadcasted_iota(jnp.int32, (C, LANES), 0)
    cum = []
    for c in chunks:
        acc = wl[c]
        for step in (1, 2, 4, 8, 16, 32):
            acc = acc + jnp.where(row >= step, pltpu.roll(acc, step, 0), 0.0)
        cum.append(acc)
    total = [cum[c][C - 1:C, :] for c in chunks]
    lhs_kk, lhs_r, vs, a = [], [], [], []
    for c in chunks:
        p_inv = jnp.exp(-cum[c])
        lhs_kk.append(stack(kk_ref[0, rows[c], :] * jnp.exp(cum[c] - wl[c])))
        lhs_r.append(stack(r_ref[0, rows[c], :] * jnp.exp(cum[c])))
        vs.append(_bf(stack(v_ref[0, rows[c], :])))
        a.append(_dot_nt(_bf(jnp.concatenate([lhs_kk[c], lhs_r[c]], axis=0)),
                         _bf(jnp.concatenate([stack(b_ref[0, rows[c], :] * p_inv),
                                              stack(k_ref[0, rows[c], :] * p_inv)], axis=0))))
    npow = [_bf(jnp.where(strict, a[c][0:2 * C, 0:2 * C], 0.0)) for c in chunks]
    a_k = [_bf(jnp.where(strict, a[c][0:2 * C, 2 * C:4 * C], 0.0)) for c in chunks]
    a_rb = [_bf(jnp.where(incl, a[c][2 * C:4 * C, 0:2 * C], 0.0)) for c in chunks]
    a_rk = [_bf(jnp.where(incl, a[c][2 * C:4 * C, 2 * C:4 * C], 0.0)) for c in chunks]
    x = [jnp.concatenate([_dot(a_k[c], vs[c]), lhs_kk[c]], axis=1) for c in chunks]
    x = [x[c] - _dot(npow[c], _bf(x[c])) for c in chunks]
    for _ in range(5):
        npow = [_bf(_dot(npow[c], npow[c])) for c in chunks]
        x = [x[c] + _dot(npow[c], _bf(x[c])) for c in chunks]
    uw = [_bf(-x[c]) for c in chunks]
    for c in chunks:
        t = _dot(a_rb[c], uw[c])
        y0_ref[0, 0, c] = _bf(_dot(a_rk[c], vs[c]) + t[:, 0:LANES])
        rq_ref[0, 0, c] = _bf(lhs_r[c] + t[:, LANES:2 * LANES])
    for c in chunks:
        p_end = jnp.exp(total[c] - cum[c])
        bc = _bf(stack(b_ref[0, rows[c], :] * p_end))
        kc = _bf(stack(k_ref[0, rows[c], :] * p_end))
        t = _dot_tn(bc, uw[c])
        g0_ref[0, 0, c] = _bf(_dot_tn(kc, vs[c]) + t[:, 0:LANES])
        m_ref[0, 0, c] = jnp.where(eye, jnp.exp(total[c]), 0.0) + t[:, LANES:2 * LANES]


def _rwkv_chunks(r, wl, k, v, kk, b):
    B, S, D = r.shape
    C = RWKV_CHUNK
    nc = S // C
    cpb = min(16, nc)
    hp = D // LANES
    tile = pl.BlockSpec((1, cpb * C, LANES), lambda bi, h, c: (bi, c, h))
    out = pl.BlockSpec((1, 1, cpb, LANES, LANES), lambda bi, h, c: (bi, h, c, 0, 0))
    return pl.pallas_call(
        functools.partial(_rwkv_chunk_kernel, cpb=cpb),
        grid=(B, hp, nc // cpb),
        in_specs=[tile] * 6,
        out_specs=[out] * 4,
        out_shape=[jax.ShapeDtypeStruct((B, hp, nc, LANES, LANES), F32)]
        + [jax.ShapeDtypeStruct((B, hp, nc, LANES, LANES), BF16)] * 3,
        compiler_params=_params(("arbitrary", "arbitrary", "arbitrary")),
        name="rwkv_chunk_summaries",
    )(r, wl, k, v, kk, b)


def _rwkv_scan_kernel(m_ref, g0_ref, rq_ref, y0_ref, y_ref, st_ref, *, cpb, hp):
    C = RWKV_CHUNK

    @pl.when(pl.program_id(1) == 0)
    def _():
        st_ref[...] = jnp.zeros_like(st_ref)

    for c in range(cpb):
        for h in range(hp):
            st = st_ref[h]
            s_hi, s_lo = _split2(st)
            y = y0_ref[0, h, c].astype(F32) + _dot(rq_ref[0, h, c], s_hi)
            y_ref[0, c * C:(c + 1) * C, h * LANES:(h + 1) * LANES] = y[0:C] + y[C:2 * C]
            m_hi, m_lo = _split2(m_ref[0, h, c])
            st_ref[h] = g0_ref[0, h, c] + _dot(m_hi, s_hi) + _dot(m_hi, s_lo) + _dot(m_lo, s_hi)


def _rwkv_scan(m, g0, rq, y0, S):
    B, hp, nc = m.shape[:3]
    C = RWKV_CHUNK
    cpb = min(4, nc)
    blk = pl.BlockSpec((1, hp, cpb, LANES, LANES), lambda bi, c: (bi, 0, c, 0, 0))
    return pl.pallas_call(
        functools.partial(_rwkv_scan_kernel, cpb=cpb, hp=hp),
        grid=(B, nc // cpb),
        in_specs=[blk] * 4,
        out_specs=pl.BlockSpec((1, cpb * C, hp * LANES), lambda bi, c: (bi, c, 0)),
        out_shape=jax.ShapeDtypeStruct((B, S, hp * LANES), F32),
        scratch_shapes=[pltpu.VMEM((hp, LANES, LANES), F32)],
        compiler_params=_params(("arbitrary", "arbitrary")),
        name="rwkv_state_scan",
    )(m, g0, rq, y0)


def _rwkv_out_kernel(x_ref, y_ref, bonus_ref, gate_ref, lnw_ref, lnb_ref, pool_ref, expand_ref, wo_ref, o_ref):
    y = y_ref[...]
    pool = pool_ref[...]
    expand = expand_ref[...]
    mu = _head_sum(_split2(y), pool, expand) * (1.0 / RWKV_HEAD)
    dev = y - mu
    var = _head_sum(_split2(dev * dev), pool, expand) * (1.0 / RWKV_HEAD)
    yn = dev * lax.rsqrt(var + RWKV_LNX_EPS) * lnw_ref[...] + lnb_ref[...]
    z = (yn + bonus_ref[...]) * gate_ref[...]
    o_ref[...] = x_ref[...] + _dot(_bf(z), wo_ref[...])


def _rwkv_out(x2d, y, bonus, gate, ln_w, ln_b, w_o):
    T, D = x2d.shape
    tm = min(256, T)
    row = pl.BlockSpec((tm, D), lambda i: (i, 0))
    vecs = pl.BlockSpec((1, D), lambda i: (0, 0))
    full = lambda shp: pl.BlockSpec(shp, lambda i: (0, 0))
    pool, expand = _head_pool(D, RWKV_HEAD)
    return pl.pallas_call(
        _rwkv_out_kernel,
        grid=(T // tm,),
        in_specs=[row, row, row, row, vecs, vecs, full((D, LANES)), full((LANES, D)), full((D, D))],
        out_specs=row,
        out_shape=jax.ShapeDtypeStruct((T, D), F32),
        compiler_params=_params(("arbitrary",)),
        name="rwkv_out",
    )(x2d, y, bonus, gate, ln_w.reshape(1, D), ln_b.reshape(1, D), pool, expand, _bf(w_o))


def _rwkv_layer(x, norm_gain, mix, w_r, w_k, w_v, w0, w1, w2, a0, a1, a2, g1, g2, k_k, k_a, r_k, ln_w, ln_b, w_o):
    B, S, D = x.shape
    r, wl, k, v, kk, b, bonus, gate = _rwkv_proj(x, norm_gain, mix, w_r, w_k, w_v, w0, w1, w2, a0, a1, a2,
                                                 g1, g2, k_k, k_a, r_k)
    m, g0, rq, y0 = _rwkv_chunks(r, wl, k, v, kk, b)
    y = _rwkv_scan(m, g0, rq, y0, S)
    T = B * S
    out = _rwkv_out(x.reshape(T, D), y.reshape(T, D), bonus.reshape(T, D), gate.reshape(T, D), ln_w, ln_b, w_o)
    return out.reshape(B, S, D)


def _router_kernel(x_ref, g_ref, whi_ref, wlo_ref, b_ref, xn_ref, route_ref):
    xn = _rms(x_ref[...], g_ref[...])
    xn_ref[...] = xn
    hi, lo = _split2(xn)
    logits = _dot(hi, whi_ref[...]) + _dot(hi, wlo_ref[...]) + _dot(lo, whi_ref[...]) + b_ref[...]
    lane = lax.broadcasted_iota(jnp.int32, logits.shape, 1)
    lane_f = lane.astype(F32)

    def top(vals):
        mx = jnp.max(vals, axis=-1, keepdims=True)
        return mx, jnp.min(jnp.where(vals == mx, lane_f, float(LANES)), axis=-1, keepdims=True)

    glog = jnp.where(lane < N_GROUPS, logits, NEG_INF)
    gmax, gidx = top(glog)
    gsum = jnp.sum(jnp.where(lane < N_GROUPS, jnp.exp(glog - gmax), 0.0), axis=-1, keepdims=True)
    grp_p = 1.0 / gsum
    first = float(N_GROUPS) + EXPERTS_PER_GROUP * gidx
    elog = jnp.where(lane_f >= first, jnp.where(lane_f < first + EXPERTS_PER_GROUP, logits, NEG_INF), NEG_INF)
    v1, i1 = top(elog)
    v2, i2 = top(jnp.where(lane_f == i1, NEG_INF, elog))
    e2 = jnp.exp(v2 - v1)
    gate1 = grp_p / (1.0 + e2)
    gate2 = grp_p * e2 / (1.0 + e2)
    route_ref[...] = jnp.where(lane == 0, i1 - N_GROUPS, jnp.where(lane == 1, i2 - N_GROUPS,
                               jnp.where(lane == 2, gate1, jnp.where(lane == 3, gate2, 0.0))))


def _router(x2d, gain, wg, bg, we, be):
    T, D = x2d.shape
    tm = min(256, T)
    w = jnp.pad(jnp.concatenate([wg, we], axis=1), ((0, 0), (0, LANES - N_GROUPS - N_EXPERTS)))
    w_hi = _bf(w)
    w_lo = _bf(w - w_hi.astype(F32))
    bias = jnp.pad(jnp.concatenate([bg, be]), (0, LANES - N_GROUPS - N_EXPERTS)).reshape(1, LANES)
    row = lambda n: pl.BlockSpec((tm, n), lambda i: (i, 0))
    full = lambda a: pl.BlockSpec(a.shape, lambda i: (0, 0))
    consts = (gain.reshape(1, D), w_hi, w_lo, bias)
    return pl.pallas_call(
        _router_kernel,
        grid=(T // tm,),
        in_specs=[row(D)] + [full(a) for a in consts],
        out_specs=[row(D), row(LANES)],
        out_shape=[jax.ShapeDtypeStruct((T, D), F32), jax.ShapeDtypeStruct((T, LANES), F32)],
        compiler_params=_params(("arbitrary",)),
        name="moe_router",
    )(x2d, *consts)


def _slot_tokens_kernel(dest_ref, slot_ref):
    def clear(s, carry):
        slot_ref[s] = 0
        return carry
    lax.fori_loop(0, slot_ref.shape[0], clear, 0, unroll=16)

    def place(p, carry):
        slot_ref[dest_ref[p]] = lax.shift_right_logical(p, 1)
        return carry
    lax.fori_loop(0, dest_ref.shape[0], place, 0, unroll=16)


def _slot_tokens(dest, n_slots):
    return pl.pallas_call(
        _slot_tokens_kernel,
        in_specs=[pl.BlockSpec(memory_space=pltpu.SMEM)],
        out_specs=pl.BlockSpec(memory_space=pltpu.SMEM),
        out_shape=jax.ShapeDtypeStruct((n_slots,), jnp.int32),
        name="moe_slot_tokens",
    )(dest)


def _row_gather_start(src_hbm, idx_ref, idx_base, idx_stride, dst, sem, rows, first=0):
    for r in range(first, first + rows):
        src_row = idx_ref[idx_base + r * idx_stride]
        pltpu.make_async_copy(src_hbm.at[pl.ds(src_row, 1)], dst.at[pl.ds(r, 1)], sem).start(priority=r % 2)


def _row_gather_wait(src_hbm, dst, sem, rows):
    pltpu.make_async_copy(src_hbm.at[pl.ds(0, rows)], dst, sem).wait()


def _expert_kernel(be_ref, nu_ref, tok_ref, xn_hbm, wg_ref, wu_ref, wd_ref, o_ref, xbuf, sem):
    i = pl.program_id(0)
    nu = nu_ref[0]
    last = pl.num_programs(0) - 1
    cur = lax.rem(i, MOE_BUFFERS)
    nxt1 = lax.rem(i + 1, MOE_BUFFERS)
    nxt2 = lax.rem(i + 2, MOE_BUFFERS)

    @pl.when(i == 0)
    def _():
        for blk in range(2):
            _row_gather_start(xn_hbm, tok_ref, blk * MOE_ROWS, 1, xbuf.at[blk], sem.at[blk], MOE_ROWS)

    @pl.when(i < nu)
    def _():
        _row_gather_wait(xn_hbm, xbuf.at[cur], sem.at[cur], MOE_ROWS)
        ahead = jnp.minimum(i + 2, last)
        d_model = xbuf.shape[-1]
        pieces = 2 * (d_model // MXU_TILE) + d_model // MXU_TILE
        per_piece = -(-MOE_ROWS // pieces)
        started = [0]

        def start_some():
            rows = min(per_piece, MOE_ROWS - started[0])
            _row_gather_start(xn_hbm, tok_ref, ahead * MOE_ROWS, 1, xbuf.at[nxt2], sem.at[nxt2], rows, started[0])
            started[0] += rows

        gate = jnp.zeros((MOE_ROWS, D_EXPERT), F32)
        up = jnp.zeros((MOE_ROWS, D_EXPERT), F32)
        for c in range(d_model // MXU_TILE):
            cols = slice(c * MXU_TILE, (c + 1) * MXU_TILE)
            xc = _bf(xbuf[cur, :, cols])
            gate = gate + _dot(xc, _bf(wg_ref[0, 0, cols, :]))
            start_some()
            up = up + _dot(xc, _bf(wu_ref[0, 0, cols, :]))
            start_some()
        hid = _bf(jax.nn.silu(gate) * up)
        for c in range(d_model // MXU_TILE):
            cols = slice(c * MXU_TILE, (c + 1) * MXU_TILE)
            y = _dot(hid, _bf(wd_ref[0, 0, :, cols]))
            start_some()
            o_ref[:, cols] = y
        assert started[0] == MOE_ROWS

        @pl.when(i == nu - 1)
        def _():
            _row_gather_wait(xn_hbm, xbuf.at[nxt1], sem.at[nxt1], MOE_ROWS)
            _row_gather_wait(xn_hbm, xbuf.at[nxt2], sem.at[nxt2], MOE_ROWS)

    @pl.when(i >= nu)
    def _():
        o_ref[...] = jnp.zeros_like(o_ref)


def _experts(xn, slot_tok, block_e, n_used, layer, e_gate, e_up, e_down):
    T, D = xn.shape
    n_blocks = slot_tok.shape[0] // MOE_ROWS
    wspec = lambda shp: pl.BlockSpec((1, 1) + shp, lambda i, be, nu, tok: (layer, be[i], 0, 0))
    return pl.pallas_call(
        _expert_kernel,
        grid_spec=pltpu.PrefetchScalarGridSpec(
            num_scalar_prefetch=3,
            grid=(n_blocks,),
            in_specs=[pl.BlockSpec(memory_space=pl.ANY), wspec((D, D_EXPERT)), wspec((D, D_EXPERT)),
                      wspec((D_EXPERT, D))],
            out_specs=pl.BlockSpec((MOE_ROWS, D), lambda i, be, nu, tok: (i, 0)),
            scratch_shapes=[pltpu.VMEM((MOE_BUFFERS, MOE_ROWS, D), F32), pltpu.SemaphoreType.DMA((MOE_BUFFERS,))],
        ),
        out_shape=jax.ShapeDtypeStruct((n_blocks * MOE_ROWS, D), F32),
        compiler_params=_params(("arbitrary",)),
        name="moe_experts",
    )(block_e, n_used, slot_tok, xn, e_gate, e_up, e_down)


def _combine_kernel(dest_ref, x_ref, route_ref, ys_hbm, o_ref, buf, sem, *, tm):
    i = pl.program_id(0)
    cur = i & 1

    def start(tile, par):
        for k in range(2):
            _row_gather_start(ys_hbm, dest_ref, tile * (2 * tm) + k, 2, buf.at[par, k], sem.at[par], tm)

    @pl.when(i == 0)
    def _():
        start(0, 0)

    @pl.when(i + 1 < pl.num_programs(0))
    def _():
        start(i + 1, 1 - cur)

    for k in range(2):
        _row_gather_wait(ys_hbm, buf.at[cur, k], sem.at[cur], tm)
    route = route_ref[...]
    o_ref[...] = x_ref[...] + route[:, 2:3] * buf[cur, 0] + route[:, 3:4] * buf[cur, 1]


def _combine(x2d, y_slots, dest, route):
    T, D = x2d.shape
    tm = min(256, T)
    row = lambda n: pl.BlockSpec((tm, n), lambda i, dest: (i, 0))
    return pl.pallas_call(
        functools.partial(_combine_kernel, tm=tm),
        grid_spec=pltpu.PrefetchScalarGridSpec(
            num_scalar_prefetch=1,
            grid=(T // tm,),
            in_specs=[row(D), row(LANES), pl.BlockSpec(memory_space=pl.ANY)],
            out_specs=row(D),
            scratch_shapes=[pltpu.VMEM((2, 2, tm, D), F32), pltpu.SemaphoreType.DMA((2,))],
        ),
        out_shape=jax.ShapeDtypeStruct((T, D), F32),
        compiler_params=_params(("arbitrary",)),
        name="moe_combine",
    )(dest, x2d, route, y_slots)


def _moe_layer(x, gain, wg, bg, we, be, layer, e_gate, e_up, e_down):
    B, S, D = x.shape
    T = B * S
    x2d = x.reshape(T, D)
    xn, route = _router(x2d, gain, wg, bg, we, be)
    flat_e = route[:, 0:2].astype(jnp.int32).reshape(-1)
    n_pairs = 2 * T
    onehot = (flat_e[:, None] == jnp.arange(N_EXPERTS, dtype=jnp.int32)[None, :]).astype(jnp.int32)
    csum = jnp.cumsum(onehot, axis=0)
    rank = jnp.take_along_axis(csum, flat_e[:, None], axis=1)[:, 0] - 1
    counts = csum[-1]
    padded = (counts + MOE_ROWS - 1) // MOE_ROWS * MOE_ROWS
    pad_end = jnp.cumsum(padded)
    dest = (pad_end - padded)[flat_e] + rank
    n_blocks = -(-n_pairs // MOE_ROWS) + N_EXPERTS
    dest = dest.astype(jnp.int32)
    slot_tok = _slot_tokens(dest, n_blocks * MOE_ROWS)
    block_start = jnp.arange(n_blocks, dtype=jnp.int32) * MOE_ROWS
    block_e = jnp.minimum(jnp.sum((pad_end[None, :] <= block_start[:, None]).astype(jnp.int32), axis=1),
                          N_EXPERTS - 1)
    n_used = (pad_end[-1:] // MOE_ROWS).astype(jnp.int32)
    y_slots = _experts(xn, slot_tok, block_e, n_used, layer, e_gate, e_up, e_down)
    return _combine(x2d, y_slots, dest, route).reshape(B, S, D)


def kernel(x, mix_norm, attn_w_in, attn_gate_bias, diff_q_norm, diff_k_norm, diff_lambda, diff_subln, nsa_q_norm, nsa_k_norm, nsa_cmp_pe, nsa_cmp_w1, nsa_cmp_w2, attn_w_out, rwkv_mix, rwkv_w_r, rwkv_w_k, rwkv_w_v, rwkv_decay_w0, rwkv_decay_w1, rwkv_decay_w2, rwkv_iclr_a0, rwkv_iclr_a1, rwkv_iclr_a2, rwkv_gate_g1, rwkv_gate_g2, rwkv_k_k, rwkv_k_a, rwkv_r_k, rwkv_ln_w, rwkv_ln_b, rwkv_w_o, ffn_norm, router_group_w, router_group_b, router_expert_w, router_expert_b, expert_w_gate, expert_w_up, expert_w_down):
    depth = mix_norm.shape[0]
    for layer in range(depth):
        i = layer // 2
        if layer % 2 == 0:
            x = _hybrid_attention_layer(
                x, layer, mix_norm[layer], attn_w_in[i], attn_gate_bias[i], diff_q_norm[i], diff_k_norm[i],
                diff_lambda[i], diff_subln[i], nsa_q_norm[i], nsa_k_norm[i], nsa_cmp_pe[i], nsa_cmp_w1[i],
                nsa_cmp_w2[i], attn_w_out[i])
        else:
            x = _rwkv_layer(
                x, mix_norm[layer], rwkv_mix[i], rwkv_w_r[i], rwkv_w_k[i], rwkv_w_v[i], rwkv_decay_w0[i],
                rwkv_decay_w1[i], rwkv_decay_w2[i], rwkv_iclr_a0[i], rwkv_iclr_a1[i], rwkv_iclr_a2[i],
                rwkv_gate_g1[i], rwkv_gate_g2[i], rwkv_k_k[i], rwkv_k_a[i], rwkv_r_k[i], rwkv_ln_w[i],
                rwkv_ln_b[i], rwkv_w_o[i])
        x = _moe_layer(x, ffn_norm[layer], router_group_w[layer], router_group_b[layer], router_expert_w[layer],
                       router_expert_b[layer], layer, expert_w_gate, expert_w_up, expert_w_down)
    return x
```

```python
import functools
import math

import jax
import jax.numpy as jnp
from jax import lax
from jax.experimental import pallas as pl
from jax.experimental.pallas import tpu as pltpu

F32 = jnp.float32
BF16 = jnp.bfloat16

D_MODEL = 1024
HEAD_DIM = 64
NORM_EPS = 1e-6
NEG_INF = -1e30
FORCE_SCORE = 1e4

DIFF_HEADS = 8
DIFF_QK_DIM = 32
NSA_HEADS = 8
NSA_KV_GROUPS = 2
NSA_HPG = 4
NSA_CMP_BLOCK = 32
NSA_CMP_STRIDE = 16
NSA_SEL_BLOCK = 64
NSA_TOP_N = 16
NSA_WINDOW = 512
IN_COLS = 2840
IN_COLS_PAD = 2944
GATE_COLS = 24

RWKV_HEAD = 64
RWKV_LNX_EPS = 64e-5
RWKV_CHUNK = 64

N_GROUPS = 4
EXPERTS_PER_GROUP = 8
N_EXPERTS = 32
D_EXPERT = 256
MOE_ROWS = 256
MOE_BUFFERS = 3

LANES = 128
MXU_TILE = 256
VMEM_LIMIT = 56 * 1024 * 1024


def _bf(x):
    return x.astype(BF16)


def _dot(a, b):
    return jnp.dot(a, b, preferred_element_type=F32)


def _dot_nt(a, b):
    return lax.dot_general(a, b, (((1,), (1,)), ((), ())), preferred_element_type=F32)


def _dot_tn(a, b):
    return lax.dot_general(a, b, (((0,), (0,)), ((), ())), preferred_element_type=F32)


def _split2(x):
    hi = _bf(x)
    lo = _bf(x - hi.astype(F32))
    return hi, lo


def _params(sem):
    return pltpu.CompilerParams(dimension_semantics=sem, vmem_limit_bytes=VMEM_LIMIT)


def _block_diag(n, group):
    r = jnp.arange(n) // group
    return ((r[:, None] == r[None, :]).astype(F32) / group).astype(BF16)


def _rms(x, gain):
    return x * lax.rsqrt(jnp.mean(x * x, axis=-1, keepdims=True) + NORM_EPS) * gain


def _inproj_kernel(x_ref, g_ref, wa_ref, wbt_ref, b_ref, bd32_ref, bd64_ref, gk_ref, gnk_ref, gq_ref, gnq_ref,
                   qd_ref, kd_ref, vd_ref, qn_ref, cmp_ref, ks_ref, vs_ref, kw_ref, vw_ref, gt_ref):
    xn = _bf(_rms(x_ref[0], g_ref[...]))
    ha = _dot(xn, wa_ref[...])
    hb = _dot_nt(wbt_ref[...], xn)
    bd32 = bd32_ref[...]
    bd64 = bd64_ref[...]

    def gnorm(seg, bd, gain):
        hi, lo = _split2(seg * seg)
        return seg * lax.rsqrt(_dot(hi, bd) + _dot(lo, bd) + NORM_EPS) * gain

    def gnorm_t(seg, bd, gain):
        hi, lo = _split2(seg * seg)
        return seg * lax.rsqrt(_dot(bd, hi) + _dot(bd, lo) + NORM_EPS) * gain

    kd_ref[0] = _bf(gnorm(ha[:, 0:512], bd32, gk_ref[...]))
    cmp_ref[0, 0] = ha[:, 512:640]
    cmp_ref[0, 1] = ha[:, 640:768]
    ksw = gnorm(ha[:, 768:1024], bd64[0:256, 0:256], gnk_ref[...])
    ks_ref[0] = _bf(ksw[:, 0:128])
    kw_ref[0] = _bf(ksw[:, 128:256])
    gt_ref[0] = jax.nn.sigmoid(ha[:, 1024:1152] + b_ref[...])
    qd_ref[0, 0] = _bf(gnorm_t(hb[0:512], bd32, gq_ref[...]))
    vd_ref[0, 0] = _bf(hb[512:1024])
    qn_ref[0, 0] = _bf(gnorm_t(hb[1024:1536], bd64, gnq_ref[...]))
    vs_ref[0, 0] = _bf(hb[1536:1664])
    vw_ref[0, 0] = _bf(hb[1664:1792])


def _attn_inproj(x, gain, w_in, gate_bias, dq_gain, dk_gain, nq_gain, nk_gain, tm):
    B, S, D = x.shape
    n = S // tm
    c = lambda lo, hi: w_in[:, lo:hi]
    gate_w = jnp.pad(c(2816, IN_COLS), ((0, 0), (0, LANES - GATE_COLS)))
    wa = _bf(jnp.concatenate([c(512, 1024), c(2048, 2304), c(2304, 2432), c(2560, 2688), gate_w], axis=1))
    wbt = _bf(jnp.concatenate([c(0, 512), c(1024, 1536), c(1536, 2048), c(2432, 2560), c(2688, 2816)], axis=1).T)
    bias = jnp.pad(gate_bias, (0, LANES - GATE_COLS)).reshape(1, LANES)
    log2e = math.log2(math.e)
    gk = jnp.tile(dk_gain, 16).reshape(1, 512)
    gnk = jnp.concatenate([jnp.tile(nk_gain[1], 2), jnp.tile(nk_gain[2], 2)]).reshape(1, 256)
    gq = (jnp.tile(dq_gain, 16) * (DIFF_QK_DIM ** -0.5 * log2e)).reshape(512, 1)
    gnq = (jnp.tile(nq_gain, 8) * (HEAD_DIM ** -0.5 * log2e)).reshape(512, 1)
    consts = (gain.reshape(1, D), wa, wbt, bias, _block_diag(512, 32), _block_diag(512, 64), gk, gnk, gq, gnq)
    full = lambda a: pl.BlockSpec(a.shape, lambda b, i: (0,) * a.ndim)
    tok = lambda w: pl.BlockSpec((1, tm, w), lambda b, i: (b, i, 0))
    tr = lambda r: pl.BlockSpec((1, 1, r, tm), lambda b, i: (b, i, 0, 0))
    tok_shape = lambda w, dt: jax.ShapeDtypeStruct((B, S, w), dt)
    tr_shape = lambda r: jax.ShapeDtypeStruct((B, n, r, tm), BF16)
    return pl.pallas_call(
        _inproj_kernel,
        grid=(B, n),
        in_specs=[tok(D)] + [full(a) for a in consts],
        out_specs=[tr(512), tok(512), tr(512), tr(512), pl.BlockSpec((1, 2, tm, LANES), lambda b, i: (b, 0, i, 0)),
                   tok(LANES), tr(LANES), tok(LANES), tr(LANES), tok(LANES)],
        out_shape=[tr_shape(512), tok_shape(512, BF16), tr_shape(512), tr_shape(512),
                   jax.ShapeDtypeStruct((B, 2, S, LANES), F32),
                   tok_shape(LANES, BF16), tr_shape(LANES), tok_shape(LANES, BF16), tr_shape(LANES),
                   tok_shape(LANES, F32)],
        compiler_params=_params(("arbitrary", "arbitrary")),
        name="attn_inproj",
    )(x, *consts)


SHIFT_LIMIT = 40.0


def _score_bound(q_gain, k_gain, dim):
    bound = dim * jnp.max(jnp.abs(q_gain)) * jnp.max(jnp.abs(k_gain)) * 1.02 + 0.25
    return jnp.stack([bound, (bound <= SHIFT_LIMIT).astype(F32)]).astype(F32)


def _flash_pass(lo, hi, scores, values, mask_body, mask_last, s_ref, p_ref, shift):
    tk, cols = s_ref.shape
    dv = HEAD_DIM
    p_ref[...] = jnp.zeros_like(p_ref)
    s_ref[...] = scores(lo)

    def process(j, carry, mask, s_next):
        s = s_ref[...]
        if mask is not None:
            s = mask(j, s)
        pv = values(jnp.maximum(j - 1, lo), p_ref[...])
        if shift is None:
            m, l, acc = carry
            m_new = jnp.maximum(m, jnp.max(s, axis=0, keepdims=True))
            alpha = jnp.exp2(m - m_new)
            p = jnp.exp2(s - m_new)
            carry = (m_new, alpha * l + jnp.sum(p, axis=0, keepdims=True), (acc + pv) * alpha)
        else:
            l, acc = carry
            p = jnp.exp2(s - shift)
            carry = (l + jnp.sum(p, axis=0, keepdims=True), acc + pv)
        p_ref[...] = _bf(p)
        if s_next is not None:
            s_ref[...] = s_next
        return carry

    init = (jnp.zeros((1, cols), F32), jnp.zeros((dv, cols), F32))
    if shift is None:
        init = (jnp.full((1, cols), NEG_INF, F32),) + init
    carry = lax.fori_loop(lo, hi, lambda j, c: process(j, c, mask_body, scores(j + 1)), init)
    l, acc = process(hi, carry, mask_last, None)[-2:]
    return (acc + values(hi, p_ref[...])) / l


def _flash_tiles(lo, hi, scores, values, mask_body, mask_last, s_ref, p_ref, o_ref, bound_ref):
    @pl.when(bound_ref[1] > 0.5)
    def _():
        o_ref[...] = _flash_pass(lo, hi, scores, values, mask_body, mask_last, s_ref, p_ref, bound_ref[0])

    @pl.when(bound_ref[1] <= 0.5)
    def _():
        o_ref[...] = _flash_pass(lo, hi, scores, values, mask_body, mask_last, s_ref, p_ref, None)

    return o_ref[...]


def _diff_attn_kernel(qt_ref, k_ref, vt_ref, lam_ref, sg_ref, bound_ref, o_ref, s_ref, p_ref, acc_ref, *,
                      tq, lambda_init):
    lp = lam_ref[...]
    lam = (jnp.exp(jnp.sum(lp[0:1] * lp[1:2], axis=-1, keepdims=True))
           - jnp.exp(jnp.sum(lp[2:3] * lp[3:4], axis=-1, keepdims=True)) + lambda_init)

    def causal(j, s):
        kpos = lax.broadcasted_iota(jnp.int32, (tq, 4 * tq), 0)
        qpos = lax.broadcasted_iota(jnp.int32, (tq, 4 * tq), 1) & (tq - 1)
        return jnp.where(kpos <= qpos, s, NEG_INF)

    def values(j, p):
        vt = vt_ref[0, j]
        return jnp.concatenate([_dot(vt[0:HEAD_DIM], p[:, 0:2 * tq]),
                                _dot(vt[HEAD_DIM:2 * HEAD_DIM], p[:, 2 * tq:4 * tq])], axis=1)

    def query_tile(i, carry):
        q = qt_ref[0, i]
        zero = jnp.zeros((DIFF_QK_DIM, tq), BF16)
        qb = jnp.concatenate(
            [jnp.concatenate([q[r * DIFF_QK_DIM:(r + 1) * DIFF_QK_DIM] if c == r else zero for c in range(4)],
                             axis=1) for r in range(4)], axis=0)
        o = _flash_tiles(0, i, lambda j: _dot(k_ref[0, j], qb), values, None, causal, s_ref, p_ref, acc_ref,
                         bound_ref)
        heads = []
        for h in range(2):
            oh = o[:, 2 * h * tq:(2 * h + 1) * tq] - lam * o[:, (2 * h + 1) * tq:(2 * h + 2) * tq]
            oh = oh * lax.rsqrt(jnp.mean(oh * oh, axis=0, keepdims=True) + NORM_EPS) * sg_ref[...]
            heads.append(oh * (1.0 - lambda_init))
        o_ref[0, pl.ds(pl.multiple_of(i * tq, tq), tq), :] = jnp.concatenate(heads, axis=0).T
        return carry

    lax.fori_loop(0, qt_ref.shape[1], query_tile, 0)


def _diff_attention(qd_t, kd, vd_t, lam_p, subln, bound, lambda_init, tq):
    B, n = qd_t.shape[:2]
    S = n * tq
    pairs = DIFF_HEADS // 2
    return pl.pallas_call(
        functools.partial(_diff_attn_kernel, tq=tq, lambda_init=lambda_init),
        grid=(B, pairs),
        in_specs=[
            pl.BlockSpec((1, n, LANES, tq), lambda b, h: (b, 0, h, 0)),
            pl.BlockSpec((1, n, tq, LANES), lambda b, h: (b, 0, 0, h)),
            pl.BlockSpec((1, n, LANES, tq), lambda b, h: (b, 0, h, 0)),
            pl.BlockSpec((4, DIFF_QK_DIM), lambda b, h: (0, 0)),
            pl.BlockSpec((HEAD_DIM, 1), lambda b, h: (0, 0)),
            pl.BlockSpec(memory_space=pltpu.SMEM),
        ],
        out_specs=pl.BlockSpec((1, S, LANES), lambda b, h: (b, 0, h)),
        out_shape=jax.ShapeDtypeStruct((B, S, DIFF_HEADS * HEAD_DIM), F32),
        scratch_shapes=[pltpu.VMEM((tq, 4 * tq), F32), pltpu.VMEM((tq, 4 * tq), BF16),
                        pltpu.VMEM((HEAD_DIM, 4 * tq), F32)],
        compiler_params=_params(("arbitrary", "arbitrary")),
        name="diff_attention",
    )(qd_t, kd.reshape(B, n, tq, DIFF_HEADS * HEAD_DIM), vd_t, lam_p, subln.reshape(HEAD_DIM, 1), bound)


def _nsa_compress_kernel(t_ref, pe_ref, w1_ref, w2_ref, bd_ref, kg_ref, kc_ref, vct_ref, *, nch):
    half = NSA_CMP_BLOCK // 2
    ya = jnp.zeros((nch, 2 * LANES), F32)
    yb = jnp.zeros((nch, 2 * LANES), F32)
    for l in range(half):
        rows = pl.ds(l, nch, stride=NSA_CMP_STRIDE)
        piece = jnp.concatenate([t_ref[0, 0, rows, :], t_ref[0, 1, rows, :]], axis=1)
        ya = ya + _dot(_bf(piece + pe_ref[l:l + 1]), w1_ref[l])
        yb = yb + _dot(_bf(piece + pe_ref[half + l:half + l + 1]), w1_ref[half + l])
    pre = ya + pltpu.roll(yb, nch - 1, 0)
    out = _dot(_bf(jax.nn.gelu(pre)), w2_ref[...])
    k = out[:, 0:LANES]
    hi, lo = _split2(k * k)
    ms = _dot(hi, bd_ref[...]) + _dot(lo, bd_ref[...])
    kc_ref[0] = _bf(k * lax.rsqrt(ms + NORM_EPS) * kg_ref[...])
    vct_ref[0] = _bf(out[:, LANES:2 * LANES].T)


def _nsa_compress(cmp_kv, cmp_pe, cmp_w1, cmp_w2, k_gain0):
    B, _, S, _ = cmp_kv.shape
    nch = S // NSA_CMP_STRIDE

    def over_groups(w):
        parts = [w[0], w[0], w[1], w[1]]
        zero = jnp.zeros_like(w[0])
        return jnp.concatenate(
            [jnp.concatenate([parts[r] if c == r else zero for c in range(4)], axis=-1) for r in range(4)], axis=-2)

    w1 = _bf(over_groups(cmp_w1.reshape(2, NSA_CMP_BLOCK, HEAD_DIM, HEAD_DIM)))
    w2 = _bf(over_groups(cmp_w2))
    pe = jnp.concatenate([cmp_pe[0], cmp_pe[0], cmp_pe[1], cmp_pe[1]], axis=-1)
    full = lambda a: pl.BlockSpec(a.shape, lambda b: (0,) * a.ndim)
    consts = (pe, w1, w2, _block_diag(LANES, HEAD_DIM), jnp.tile(k_gain0, 2).reshape(1, LANES))
    return pl.pallas_call(
        functools.partial(_nsa_compress_kernel, nch=nch),
        grid=(B,),
        in_specs=[pl.BlockSpec((1, 2, S, LANES), lambda b: (b, 0, 0, 0))] + [full(a) for a in consts],
        out_specs=[pl.BlockSpec((1, nch, LANES), lambda b: (b, 0, 0)), pl.BlockSpec((1, LANES, nch), lambda b: (b, 0, 0))],
        out_shape=[jax.ShapeDtypeStruct((B, nch, LANES), BF16), jax.ShapeDtypeStruct((B, LANES, nch), BF16)],
        compiler_params=_params(("arbitrary",)),
        name="nsa_compress",
    )(cmp_kv, *consts)


def _group_queries(q, g):
    qg = jnp.concatenate([q[h * HEAD_DIM:(h + 1) * HEAD_DIM] for h in range(NSA_HPG)], axis=1)
    zero = jnp.zeros_like(qg)
    return jnp.where(g == 0, jnp.concatenate([qg, zero], axis=0), jnp.concatenate([zero, qg], axis=0))


def _heads_token_major(o, tq):
    return jnp.concatenate([o[:, h * tq:(h + 1) * tq] for h in range(NSA_HPG)], axis=0).T


def _nsa_cmp_attn_kernel(q_ref, k_ref, vt_ref, o_ref, sel_ref, *, tq, nch, n_sel, top_n):
    i = pl.program_id(2)
    cols = NSA_HPG * tq
    n_cmp = nch - 1
    s = _dot(k_ref[0], _group_queries(q_ref[0, 0], pl.program_id(1)))
    pos = i * tq + (lax.broadcasted_iota(jnp.int32, (nch, cols), 1) & (tq - 1))
    c = lax.broadcasted_iota(jnp.int32, (nch, cols), 0)
    ok = jnp.where(c < n_cmp, c * NSA_CMP_STRIDE + (NSA_CMP_BLOCK - 1), 1 << 30) <= pos
    s = jnp.where(ok, s, NEG_INF)
    p = jnp.where(ok, jnp.exp2(s - jnp.max(s, axis=0, keepdims=True)), 0.0)
    l = jnp.sum(p, axis=0, keepdims=True)
    p = p / jnp.where(l > 0.0, l, 1.0)
    o_ref[0] = _heads_token_major(_dot(vt_ref[0], _bf(p)), tq)

    pg = p[:, 0:tq]
    for h in range(1, NSA_HPG):
        pg = pg + p[:, h * tq:(h + 1) * tq]
    jj = lax.broadcasted_iota(jnp.int32, (n_sel, nch), 0) * NSA_SEL_BLOCK
    cc = lax.broadcasted_iota(jnp.int32, (n_sel, nch), 1) * NSA_CMP_STRIDE
    cover = jnp.where(cc < jj + NSA_SEL_BLOCK,
                      jnp.where(cc + NSA_CMP_BLOCK > jj, jnp.where(cc < n_cmp * NSA_CMP_STRIDE, 1.0, 0.0), 0.0), 0.0)
    cover = _bf(cover)
    hi, lo = _split2(pg)
    imp = _dot(cover, hi) + _dot(cover, lo)
    blk = lax.broadcasted_iota(jnp.int32, (n_sel, tq), 0)
    cur = (i * tq + lax.broadcasted_iota(jnp.int32, (n_sel, tq), 1)) >> 6
    imp = jnp.where(blk == cur, FORCE_SCORE, jnp.where(blk == 0, FORCE_SCORE, jnp.where(blk > cur, NEG_INF, imp)))
    blk_f = blk.astype(F32)
    sel = jnp.zeros((n_sel, tq), F32)
    for _ in range(top_n):
        mx = jnp.max(imp, axis=0, keepdims=True)
        first = jnp.min(jnp.where(imp == mx, blk_f, float(n_sel)), axis=0, keepdims=True)
        hit = blk_f == first
        sel = jnp.where(hit, 1.0, sel)
        imp = jnp.where(hit, -jnp.inf, imp)
    sel_ref[0, 0, 0] = _bf(sel)


def _nsa_cmp_attention(qn_t, kc, vc_t, tq):
    B, n = qn_t.shape[:2]
    S = n * tq
    G = NSA_KV_GROUPS
    nch = S // NSA_CMP_STRIDE
    n_sel = S // NSA_SEL_BLOCK
    top_n = min(NSA_TOP_N, n_sel)
    gcols = NSA_HPG * HEAD_DIM
    return pl.pallas_call(
        functools.partial(_nsa_cmp_attn_kernel, tq=tq, nch=nch, n_sel=n_sel, top_n=top_n),
        grid=(B, G, n),
        in_specs=[
            pl.BlockSpec((1, 1, gcols, tq), lambda b, g, i: (b, i, g, 0)),
            pl.BlockSpec((1, nch, LANES), lambda b, g, i: (b, 0, 0)),
            pl.BlockSpec((1, HEAD_DIM, nch), lambda b, g, i: (b, g, 0)),
        ],
        out_specs=[pl.BlockSpec((1, tq, gcols), lambda b, g, i: (b, i, g)),
                   pl.BlockSpec((1, 1, 1, n_sel, tq), lambda b, g, i: (b, g, i, 0, 0))],
        out_shape=[jax.ShapeDtypeStruct((B, S, G * gcols), F32),
                   jax.ShapeDtypeStruct((B, G, n, n_sel, tq), BF16)],
        compiler_params=_params(("arbitrary", "arbitrary", "arbitrary")),
        name="nsa_cmp_attention",
    )(qn_t, kc, vc_t)


def _nsa_sel_win_kernel(q_ref, ks_ref, vst_ref, kw_ref, vwt_ref, sel_ref, bs_ref, bw_ref, os_ref, ow_ref,
                        s_ref, p_ref, acc_ref, *, tq, n_sel):
    g = pl.program_id(1)
    kloc = lax.broadcasted_iota(jnp.int32, (tq, tq), 0)
    qloc = lax.broadcasted_iota(jnp.int32, (tq, tq), 1)
    blk_row = lax.broadcasted_iota(jnp.int32, (tq, n_sel), 0)
    blk_col = lax.broadcasted_iota(jnp.int32, (tq, n_sel), 1)

    def masked(s, keep):
        return jnp.where(jnp.concatenate([keep] * NSA_HPG, axis=1) > 0.5, s, NEG_INF)

    def query_tile(i, carry):
        qt = _group_queries(q_ref[0, i], g)
        selm = sel_ref[0, 0, i]
        qpos = i * tq + qloc
        rows = pl.ds(pl.multiple_of(i * tq, tq), tq)

        def sel_mask(j, s):
            expand = _bf(jnp.where(blk_col == ((j * tq + blk_row) >> 6), 1.0, 0.0))
            chosen = _dot(expand, selm)
            return masked(s, jnp.where(j * tq + kloc <= qpos, chosen, 0.0))

        o = _flash_tiles(0, i, lambda j: _dot(ks_ref[0, j], qt), lambda j, p: _dot(vst_ref[0, j], p),
                         sel_mask, sel_mask, s_ref, p_ref, acc_ref, bs_ref)
        os_ref[0, rows, :] = _heads_token_major(o, tq)

        def win_mask(j, s):
            dist = qpos - (j * tq + kloc)
            return masked(s, jnp.where(dist >= 0, jnp.where(dist < NSA_WINDOW, 1.0, 0.0), 0.0))

        first = jnp.maximum(i - (NSA_WINDOW - 1 + tq - 1) // tq, 0)
        o = _flash_tiles(first, i, lambda j: _dot(kw_ref[0, j], qt), lambda j, p: _dot(vwt_ref[0, j], p),
                         win_mask, win_mask, s_ref, p_ref, acc_ref, bw_ref)
        ow_ref[0, rows, :] = _heads_token_major(o, tq)
        return carry

    lax.fori_loop(0, q_ref.shape[1], query_tile, 0)


def _nsa_sel_win(qn_t, ks, vs_t, kw, vw_t, sel, bound_sel, bound_win, tq):
    B, n = qn_t.shape[:2]
    S = n * tq
    G = NSA_KV_GROUPS
    n_sel = S // NSA_SEL_BLOCK
    gcols = NSA_HPG * HEAD_DIM
    kspec = pl.BlockSpec((1, n, tq, LANES), lambda b, g: (b, 0, 0, 0))
    vtspec = pl.BlockSpec((1, n, HEAD_DIM, tq), lambda b, g: (b, 0, g, 0))
    ospec = pl.BlockSpec((1, S, gcols), lambda b, g: (b, 0, g))
    return pl.pallas_call(
        functools.partial(_nsa_sel_win_kernel, tq=tq, n_sel=n_sel),
        grid=(B, G),
        in_specs=[pl.BlockSpec((1, n, gcols, tq), lambda b, g: (b, 0, g, 0)), kspec, vtspec, kspec, vtspec,
                  pl.BlockSpec((1, 1, n, n_sel, tq), lambda b, g: (b, g, 0, 0, 0)),
                  pl.BlockSpec(memory_space=pltpu.SMEM), pl.BlockSpec(memory_space=pltpu.SMEM)],
        out_specs=[ospec, ospec],
        out_shape=[jax.ShapeDtypeStruct((B, S, G * gcols), F32)] * 2,
        scratch_shapes=[pltpu.VMEM((tq, NSA_HPG * tq), F32), pltpu.VMEM((tq, NSA_HPG * tq), BF16),
                        pltpu.VMEM((HEAD_DIM, NSA_HPG * tq), F32)],
        compiler_params=_params(("arbitrary", "arbitrary")),
        name="nsa_sel_win_attention",
    )(qn_t, ks.reshape(B, n, tq, LANES), vs_t, kw.reshape(B, n, tq, LANES), vw_t, sel, bound_sel, bound_win)


def _attn_outproj_kernel(x_ref, d_ref, oc_ref, os_ref, ow_ref, gt_ref, ge_ref, w_ref, o_ref):
    ghi, glo = _split2(gt_ref[...])

    def gate(r):
        return _dot(ghi, ge_ref[r]) + _dot(glo, ge_ref[r])

    nsa = gate(0) * oc_ref[...] + gate(1) * os_ref[...] + gate(2) * ow_ref[...]
    y = _dot(_bf(d_ref[...]), w_ref[0:512, :]) + _dot(_bf(nsa), w_ref[512:1024, :])
    o_ref[...] = x_ref[...] + y


def _attn_outproj(x2d, diff_o, o_cmp, o_sel, o_win, gates, w_out):
    T, D = x2d.shape
    tm = min(512, T)
    col = jnp.arange(512) // HEAD_DIM
    src = jnp.arange(LANES)
    ge = jnp.stack([_bf((src[:, None] == col[None, :] * 3 + r).astype(F32)) for r in range(3)])
    row = lambda n: pl.BlockSpec((tm, n), lambda i: (i, 0))
    return pl.pallas_call(
        _attn_outproj_kernel,
        grid=(T // tm,),
        in_specs=[row(D), row(512), row(512), row(512), row(512), row(LANES),
                  pl.BlockSpec((3, LANES, 512), lambda i: (0, 0, 0)),
                  pl.BlockSpec((D, D), lambda i: (0, 0))],
        out_specs=row(D),
        out_shape=jax.ShapeDtypeStruct((T, D), F32),
        compiler_params=_params(("arbitrary",)),
        name="attn_outproj",
    )(x2d, diff_o, o_cmp, o_sel, o_win, gates, ge, _bf(w_out))


def _hybrid_attention_layer(x, layer, norm_gain, w_in, gate_bias, dq_gain, dk_gain, lam_p, subln, nq_gain, nk_gain,
                            cmp_pe, cmp_w1, cmp_w2, w_out):
    B, S, D = x.shape
    T = B * S
    tq = min(256, S)
    qd_t, kd, vd_t, qn_t, cmp_kv, ks, vs_t, kw, vw_t, gates = _attn_inproj(
        x, norm_gain, w_in, gate_bias, dq_gain, dk_gain, nq_gain, nk_gain, tq)
    lambda_init = 0.8 - 0.6 * math.exp(-0.3 * layer)
    log2e = math.log2(math.e)
    dq_scaled = dq_gain * (DIFF_QK_DIM ** -0.5 * log2e)
    nq_scaled = nq_gain * (HEAD_DIM ** -0.5 * log2e)
    diff_o = _diff_attention(qd_t, kd, vd_t, lam_p, subln, _score_bound(dq_scaled, dk_gain, DIFF_QK_DIM),
                             lambda_init, tq)
    kc, vc_t = _nsa_compress(cmp_kv, cmp_pe, cmp_w1, cmp_w2, nk_gain[0])
    o_cmp, sel = _nsa_cmp_attention(qn_t, kc, vc_t, tq)
    o_sel, o_win = _nsa_sel_win(qn_t, ks, vs_t, kw, vw_t, sel, _score_bound(nq_scaled, nk_gain[1], HEAD_DIM),
                                _score_bound(nq_scaled, nk_gain[2], HEAD_DIM), tq)
    flat = lambda a: a.reshape(T, a.shape[-1])
    out = _attn_outproj(flat(x), flat(diff_o), flat(o_cmp), flat(o_sel), flat(o_win), flat(gates), w_out)
    return out.reshape(B, S, D)


def _softplus(z):
    return jnp.maximum(z, 0.0) + jnp.log(1.0 + jnp.exp(-jnp.abs(z)))


def _head_pool(d, head):
    member = (jnp.arange(d)[:, None] // head == jnp.arange(LANES)[None, :]).astype(BF16)
    return member, member.T


def _head_sum(parts, pool, expand):
    sums = _dot(parts[0], pool)
    for part in parts[1:]:
        sums = sums + _dot(part, pool)
    hi, lo = _split2(sums)
    return _dot(hi, expand) + _dot(lo, expand)


def _rwkv_proj_kernel(x_ref, xp_ref, g_ref, mix_ref, wr_ref, wk_ref, wv_ref, w1_ref, w2_ref, a1_ref, a2_ref,
                      g1_ref, g2_ref, vec_ref, pool_ref, expand_ref,
                      r_ref, wl_ref, k_ref, v_ref, kk_ref, b_ref, bonus_ref, gate_ref):
    i = pl.program_id(1)
    gain = g_ref[...]
    xn = _rms(x_ref[0], gain)
    prev = _rms(xp_ref[0], gain)[7:8] * jnp.where(i > 0, 1.0, 0.0)
    shifted = pltpu.roll(xn, 1, 0)
    first_row = lax.broadcasted_iota(jnp.int32, xn.shape, 0) == 0
    dx = jnp.where(first_row, prev, shifted) - xn
    mix = mix_ref[...]
    xr, xw, xk, xv, xa, xg = (_bf(xn + dx * mix[j:j + 1]) for j in range(6))
    vec = vec_ref[...]
    w0, a0, k_k, k_a, r_k = (vec[j:j + 1] for j in range(5))
    r = _dot(xr, wr_ref[...])
    k = _dot(xk, wk_ref[...])
    v = _dot(xv, wv_ref[...])
    w = -_softplus(-(w0 + _dot(_bf(jnp.tanh(_dot(xw, w1_ref[...]))), w2_ref[...]))) - 0.5
    a = jax.nn.sigmoid(a0 + _dot(_bf(_dot(xa, a1_ref[...])), a2_ref[...]))
    gate_ref[0] = _dot(_bf(jax.nn.sigmoid(_dot(xg, g1_ref[...]))), g2_ref[...])
    pool = pool_ref[...]
    expand = expand_ref[...]
    kk = k * k_k
    kk = kk / jnp.maximum(jnp.sqrt(_head_sum([_bf(kk * kk)], pool, expand)), 1e-12)
    k = k * (1.0 + (a - 1.0) * k_a)
    r_ref[0] = r
    wl_ref[0] = -jnp.exp(w)
    k_ref[0] = k
    v_ref[0] = v
    kk_ref[0] = kk
    b_ref[0] = kk * a
    bonus_ref[0] = _head_sum([_bf(r * k * r_k)], pool, expand) * v


def _rwkv_proj(x, gain, mix, w_r, w_k, w_v, w0, w1, w2, a0, a1, a2, g1, g2, k_k, k_a, r_k):
    B, S, D = x.shape
    tm = min(256, S)
    pad_c = lambda m, n: _bf(jnp.pad(m, ((0, 0), (0, n - m.shape[1]))))
    pad_r = lambda m, n: _bf(jnp.pad(m, ((0, n - m.shape[0]), (0, 0))))
    lw = LANES
    lg = 2 * LANES
    consts = (gain.reshape(1, D), jnp.pad(mix, ((0, 2), (0, 0))), _bf(w_r), _bf(w_k), _bf(w_v),
              pad_c(w1, lw), pad_r(w2, lw), pad_c(a1, lw), pad_r(a2, lw), pad_c(g1, lg), pad_r(g2, lg),
              jnp.pad(jnp.stack([w0, a0, k_k, k_a, r_k]), ((0, 3), (0, 0)))) + _head_pool(D, RWKV_HEAD)
    full = lambda a: pl.BlockSpec(a.shape, lambda b, i: (0,) * a.ndim)
    tile = pl.BlockSpec((1, tm, D), lambda b, i: (b, i, 0))
    return pl.pallas_call(
        _rwkv_proj_kernel,
        grid=(B, S // tm),
        in_specs=[tile, pl.BlockSpec((1, 8, D), lambda b, i: (b, jnp.maximum(i * (tm // 8) - 1, 0), 0))]
        + [full(a) for a in consts],
        out_specs=[tile] * 8,
        out_shape=[jax.ShapeDtypeStruct((B, S, D), F32)] * 8,
        compiler_params=_params(("arbitrary", "arbitrary")),
        name="rwkv_proj",
    )(x, x, *consts)


def _rwkv_chunk_kernel(r_ref, wl_ref, k_ref, v_ref, kk_ref, b_ref, m_ref, g0_ref, rq_ref, y0_ref, *, cpb):
    C = RWKV_CHUNK
    lane = lax.broadcasted_iota(jnp.int32, (C, LANES), 1)
    head0 = lane < RWKV_HEAD
    ti = lax.broadcasted_iota(jnp.int32, (2 * C, 2 * C), 0) & (C - 1)
    tj = lax.broadcasted_iota(jnp.int32, (2 * C, 2 * C), 1) & (C - 1)
    strict = ti > tj
    incl = ti >= tj
    eye = lax.broadcasted_iota(jnp.int32, (LANES, LANES), 0) == lax.broadcasted_iota(jnp.int32, (LANES, LANES), 1)

    def stack(x):
        return jnp.concatenate([jnp.where(head0, x, 0.0), jnp.where(head0, 0.0, x)], axis=0)

    chunks = range(cpb)
    rows = [slice(c * C, (c + 1) * C) for c in chunks]
    wl = [wl_ref[0, rows[c], :] for c in chunks]
    row = lax.broadcasted_iota(jnp.int32, (C, LANES), 0)
    cum = []
    for c in chunks:
        acc = wl[c]
        for step in (1, 2, 4, 8, 16, 32):
            acc = acc + jnp.where(row >= step, pltpu.roll(acc, step, 0), 0.0)
        cum.append(acc)
    total = [cum[c][C - 1:C, :] for c in chunks]
    lhs_kk, lhs_r, vs, a = [], [], [], []
    for c in chunks:
        p_inv = jnp.exp(-cum[c])
        lhs_kk.append(stack(kk_ref[0, rows[c], :] * jnp.exp(cum[c] - wl[c])))
        lhs_r.append(stack(r_ref[0, rows[c], :] * jnp.exp(cum[c])))
        vs.append(_bf(stack(v_ref[0, rows[c], :])))
        a.append(_dot_nt(_bf(jnp.concatenate([lhs_kk[c], lhs_r[c]], axis=0)),
                         _bf(jnp.concatenate([stack(b_ref[0, rows[c], :] * p_inv),
                                              stack(k_ref[0, rows[c], :] * p_inv)], axis=0))))
    npow = [_bf(jnp.where(strict, a[c][0:2 * C, 0:2 * C], 0.0)) for c in chunks]
    a_k = [_bf(jnp.where(strict, a[c][0:2 * C, 2 * C:4 * C], 0.0)) for c in chunks]
    a_rb = [_bf(jnp.where(incl, a[c][2 * C:4 * C, 0:2 * C], 0.0)) for c in chunks]
    a_rk = [_bf(jnp.where(incl, a[c][2 * C:4 * C, 2 * C:4 * C], 0.0)) for c in chunks]
    x = [jnp.concatenate([_dot(a_k[c], vs[c]), lhs_kk[c]], axis=1) for c in chunks]
    x = [x[c] - _dot(npow[c], _bf(x[c])) for c in chunks]
    for _ in range(5):
        npow = [_bf(_dot(npow[c], npow[c])) for c in chunks]
        x = [x[c] + _dot(npow[c], _bf(x[c])) for c in chunks]
    uw = [_bf(-x[c]) for c in chunks]
    for c in chunks:
        t = _dot(a_rb[c], uw[c])
        y0_ref[0, 0, c] = _bf(_dot(a_rk[c], vs[c]) + t[:, 0:LANES])
        rq_ref[0, 0, c] = _bf(lhs_r[c] + t[:, LANES:2 * LANES])
    for c in chunks:
        p_end = jnp.exp(total[c] - cum[c])
        bc = _bf(stack(b_ref[0, rows[c], :] * p_end))
        kc = _bf(stack(k_ref[0, rows[c], :] * p_end))
        t = _dot_tn(bc, uw[c])
        g0_ref[0, 0, c] = _bf(_dot_tn(kc, vs[c]) + t[:, 0:LANES])
        m_ref[0, 0, c] = jnp.where(eye, jnp.exp(total[c]), 0.0) + t[:, LANES:2 * LANES]


def _rwkv_chunks(r, wl, k, v, kk, b):
    B, S, D = r.shape
    C = RWKV_CHUNK
    nc = S // C
    cpb = min(16, nc)
    hp = D // LANES
    tile = pl.BlockSpec((1, cpb * C, LANES), lambda bi, h, c: (bi, c, h))
    out = pl.BlockSpec((1, 1, cpb, LANES, LANES), lambda bi, h, c: (bi, h, c, 0, 0))
    return pl.pallas_call(
        functools.partial(_rwkv_chunk_kernel, cpb=cpb),
        grid=(B, hp, nc // cpb),
        in_specs=[tile] * 6,
        out_specs=[out] * 4,
        out_shape=[jax.ShapeDtypeStruct((B, hp, nc, LANES, LANES), F32)]
        + [jax.ShapeDtypeStruct((B, hp, nc, LANES, LANES), BF16)] * 3,
        compiler_params=_params(("arbitrary", "arbitrary", "arbitrary")),
        name="rwkv_chunk_summaries",
    )(r, wl, k, v, kk, b)


def _rwkv_scan_kernel(m_ref, g0_ref, rq_ref, y0_ref, y_ref, st_ref, *, cpb, hp):
    C = RWKV_CHUNK

    @pl.when(pl.program_id(1) == 0)
    def _():
        st_ref[...] = jnp.zeros_like(st_ref)

    for c in range(cpb):
        for h in range(hp):
            st = st_ref[h]
            s_hi, s_lo = _split2(st)
            y = y0_ref[0, h, c].astype(F32) + _dot(rq_ref[0, h, c], s_hi)
            y_ref[0, c * C:(c + 1) * C, h * LANES:(h + 1) * LANES] = y[0:C] + y[C:2 * C]
            m_hi, m_lo = _split2(m_ref[0, h, c])
            st_ref[h] = g0_ref[0, h, c] + _dot(m_hi, s_hi) + _dot(m_hi, s_lo) + _dot(m_lo, s_hi)


def _rwkv_scan(m, g0, rq, y0, S):
    B, hp, nc = m.shape[:3]
    C = RWKV_CHUNK
    cpb = min(4, nc)
    blk = pl.BlockSpec((1, hp, cpb, LANES, LANES), lambda bi, c: (bi, 0, c, 0, 0))
    return pl.pallas_call(
        functools.partial(_rwkv_scan_kernel, cpb=cpb, hp=hp),
        grid=(B, nc // cpb),
        in_specs=[blk] * 4,
        out_specs=pl.BlockSpec((1, cpb * C, hp * LANES), lambda bi, c: (bi, c, 0)),
        out_shape=jax.ShapeDtypeStruct((B, S, hp * LANES), F32),
        scratch_shapes=[pltpu.VMEM((hp, LANES, LANES), F32)],
        compiler_params=_params(("arbitrary", "arbitrary")),
        name="rwkv_state_scan",
    )(m, g0, rq, y0)


def _rwkv_out_kernel(x_ref, y_ref, bonus_ref, gate_ref, lnw_ref, lnb_ref, pool_ref, expand_ref, wo_ref, o_ref):
    y = y_ref[...]
    pool = pool_ref[...]
    expand = expand_ref[...]
    mu = _head_sum(_split2(y), pool, expand) * (1.0 / RWKV_HEAD)
    dev = y - mu
    var = _head_sum(_split2(dev * dev), pool, expand) * (1.0 / RWKV_HEAD)
    yn = dev * lax.rsqrt(var + RWKV_LNX_EPS) * lnw_ref[...] + lnb_ref[...]
    z = (yn + bonus_ref[...]) * gate_ref[...]
    o_ref[...] = x_ref[...] + _dot(_bf(z), wo_ref[...])


def _rwkv_out(x2d, y, bonus, gate, ln_w, ln_b, w_o):
    T, D = x2d.shape
    tm = min(256, T)
    row = pl.BlockSpec((tm, D), lambda i: (i, 0))
    vecs = pl.BlockSpec((1, D), lambda i: (0, 0))
    full = lambda shp: pl.BlockSpec(shp, lambda i: (0, 0))
    pool, expand = _head_pool(D, RWKV_HEAD)
    return pl.pallas_call(
        _rwkv_out_kernel,
        grid=(T // tm,),
        in_specs=[row, row, row, row, vecs, vecs, full((D, LANES)), full((LANES, D)), full((D, D))],
        out_specs=row,
        out_shape=jax.ShapeDtypeStruct((T, D), F32),
        compiler_params=_params(("arbitrary",)),
        name="rwkv_out",
    )(x2d, y, bonus, gate, ln_w.reshape(1, D), ln_b.reshape(1, D), pool, expand, _bf(w_o))


def _rwkv_layer(x, norm_gain, mix, w_r, w_k, w_v, w0, w1, w2, a0, a1, a2, g1, g2, k_k, k_a, r_k, ln_w, ln_b, w_o):
    B, S, D = x.shape
    r, wl, k, v, kk, b, bonus, gate = _rwkv_proj(x, norm_gain, mix, w_r, w_k, w_v, w0, w1, w2, a0, a1, a2,
                                                 g1, g2, k_k, k_a, r_k)
    m, g0, rq, y0 = _rwkv_chunks(r, wl, k, v, kk, b)
    y = _rwkv_scan(m, g0, rq, y0, S)
    T = B * S
    out = _rwkv_out(x.reshape(T, D), y.reshape(T, D), bonus.reshape(T, D), gate.reshape(T, D), ln_w, ln_b, w_o)
    return out.reshape(B, S, D)


def _router_kernel(x_ref, g_ref, whi_ref, wlo_ref, b_ref, xn_ref, route_ref):
    xn = _rms(x_ref[...], g_ref[...])
    xn_ref[...] = xn
    hi, lo = _split2(xn)
    logits = _dot(hi, whi_ref[...]) + _dot(hi, wlo_ref[...]) + _dot(lo, whi_ref[...]) + b_ref[...]
    lane = lax.broadcasted_iota(jnp.int32, logits.shape, 1)
    lane_f = lane.astype(F32)

    def top(vals):
        mx = jnp.max(vals, axis=-1, keepdims=True)
        return mx, jnp.min(jnp.where(vals == mx, lane_f, float(LANES)), axis=-1, keepdims=True)

    glog = jnp.where(lane < N_GROUPS, logits, NEG_INF)
    gmax, gidx = top(glog)
    gsum = jnp.sum(jnp.where(lane < N_GROUPS, jnp.exp(glog - gmax), 0.0), axis=-1, keepdims=True)
    grp_p = 1.0 / gsum
    first = float(N_GROUPS) + EXPERTS_PER_GROUP * gidx
    elog = jnp.where(lane_f >= first, jnp.where(lane_f < first + EXPERTS_PER_GROUP, logits, NEG_INF), NEG_INF)
    v1, i1 = top(elog)
    v2, i2 = top(jnp.where(lane_f == i1, NEG_INF, elog))
    e2 = jnp.exp(v2 - v1)
    gate1 = grp_p / (1.0 + e2)
    gate2 = grp_p * e2 / (1.0 + e2)
    route_ref[...] = jnp.where(lane == 0, i1 - N_GROUPS, jnp.where(lane == 1, i2 - N_GROUPS,
                               jnp.where(lane == 2, gate1, jnp.where(lane == 3, gate2, 0.0))))


def _router(x2d, gain, wg, bg, we, be):
    T, D = x2d.shape
    tm = min(512, T)
    w = jnp.pad(jnp.concatenate([wg, we], axis=1), ((0, 0), (0, LANES - N_GROUPS - N_EXPERTS)))
    w_hi = _bf(w)
    w_lo = _bf(w - w_hi.astype(F32))
    bias = jnp.pad(jnp.concatenate([bg, be]), (0, LANES - N_GROUPS - N_EXPERTS)).reshape(1, LANES)
    row = lambda n: pl.BlockSpec((tm, n), lambda i: (i, 0))
    full = lambda a: pl.BlockSpec(a.shape, lambda i: (0, 0))
    consts = (gain.reshape(1, D), w_hi, w_lo, bias)
    return pl.pallas_call(
        _router_kernel,
        grid=(T // tm,),
        in_specs=[row(D)] + [full(a) for a in consts],
        out_specs=[row(D), row(LANES)],
        out_shape=[jax.ShapeDtypeStruct((T, D), F32), jax.ShapeDtypeStruct((T, LANES), F32)],
        compiler_params=_params(("arbitrary",)),
        name="moe_router",
    )(x2d, *consts)


def _slot_tokens_kernel(dest_ref, slot_ref):
    def clear(s, carry):
        slot_ref[s] = 0
        return carry
    lax.fori_loop(0, slot_ref.shape[0], clear, 0, unroll=16)

    def place(p, carry):
        slot_ref[dest_ref[p]] = lax.shift_right_logical(p, 1)
        return carry
    lax.fori_loop(0, dest_ref.shape[0], place, 0, unroll=16)


def _slot_tokens(dest, n_slots):
    return pl.pallas_call(
        _slot_tokens_kernel,
        in_specs=[pl.BlockSpec(memory_space=pltpu.SMEM)],
        out_specs=pl.BlockSpec(memory_space=pltpu.SMEM),
        out_shape=jax.ShapeDtypeStruct((n_slots,), jnp.int32),
        name="moe_slot_tokens",
    )(dest)


def _row_gather_start(src_hbm, idx_ref, idx_base, idx_stride, dst, sem, rows, first=0):
    for r in range(first, first + rows):
        src_row = idx_ref[idx_base + r * idx_stride]
        pltpu.make_async_copy(src_hbm.at[pl.ds(src_row, 1)], dst.at[pl.ds(r, 1)], sem).start(priority=r % 2)


def _row_gather_wait(src_hbm, dst, sem, rows):
    pltpu.make_async_copy(src_hbm.at[pl.ds(0, rows)], dst, sem).wait()


def _expert_kernel(be_ref, nu_ref, tok_ref, xn_hbm, wg_ref, wu_ref, wd_ref, o_ref, xbuf, sem):
    i = pl.program_id(0)
    nu = nu_ref[0]
    last = pl.num_programs(0) - 1
    cur = lax.rem(i, MOE_BUFFERS)
    nxt1 = lax.rem(i + 1, MOE_BUFFERS)
    nxt2 = lax.rem(i + 2, MOE_BUFFERS)

    @pl.when(i == 0)
    def _():
        for blk in range(2):
            _row_gather_start(xn_hbm, tok_ref, blk * MOE_ROWS, 1, xbuf.at[blk], sem.at[blk], MOE_ROWS)

    @pl.when(i < nu)
    def _():
        _row_gather_wait(xn_hbm, xbuf.at[cur], sem.at[cur], MOE_ROWS)
        ahead = jnp.minimum(i + 2, last)
        d_model = xbuf.shape[-1]
        pieces = 2 * (d_model // MXU_TILE) + d_model // MXU_TILE
        per_piece = -(-MOE_ROWS // pieces)
        started = [0]

        def start_some():
            rows = min(per_piece, MOE_ROWS - started[0])
            _row_gather_start(xn_hbm, tok_ref, ahead * MOE_ROWS, 1, xbuf.at[nxt2], sem.at[nxt2], rows, started[0])
            started[0] += rows

        gate = jnp.zeros((MOE_ROWS, D_EXPERT), F32)
        up = jnp.zeros((MOE_ROWS, D_EXPERT), F32)
        for c in range(d_model // MXU_TILE):
            cols = slice(c * MXU_TILE, (c + 1) * MXU_TILE)
            xc = _bf(xbuf[cur, :, cols])
            gate = gate + _dot(xc, _bf(wg_ref[0, 0, cols, :]))
            start_some()
            up = up + _dot(xc, _bf(wu_ref[0, 0, cols, :]))
            start_some()
        hid = _bf(jax.nn.silu(gate) * up)
        for c in range(d_model // MXU_TILE):
            cols = slice(c * MXU_TILE, (c + 1) * MXU_TILE)
            y = _dot(hid, _bf(wd_ref[0, 0, :, cols]))
            start_some()
            o_ref[:, cols] = y
        assert started[0] == MOE_ROWS

        @pl.when(i == nu - 1)
        def _():
            _row_gather_wait(xn_hbm, xbuf.at[nxt1], sem.at[nxt1], MOE_ROWS)
            _row_gather_wait(xn_hbm, xbuf.at[nxt2], sem.at[nxt2], MOE_ROWS)

    @pl.when(i >= nu)
    def _():
        o_ref[...] = jnp.zeros_like(o_ref)


def _experts(xn, slot_tok, block_e, n_used, layer, e_gate, e_up, e_down):
    T, D = xn.shape
    n_blocks = slot_tok.shape[0] // MOE_ROWS
    wspec = lambda shp: pl.BlockSpec((1, 1) + shp, lambda i, be, nu, tok: (layer, be[i], 0, 0))
    return pl.pallas_call(
        _expert_kernel,
        grid_spec=pltpu.PrefetchScalarGridSpec(
            num_scalar_prefetch=3,
            grid=(n_blocks,),
            in_specs=[pl.BlockSpec(memory_space=pl.ANY), wspec((D, D_EXPERT)), wspec((D, D_EXPERT)),
                      wspec((D_EXPERT, D))],
            out_specs=pl.BlockSpec((MOE_ROWS, D), lambda i, be, nu, tok: (i, 0)),
            scratch_shapes=[pltpu.VMEM((MOE_BUFFERS, MOE_ROWS, D), F32), pltpu.SemaphoreType.DMA((MOE_BUFFERS,))],
        ),
        out_shape=jax.ShapeDtypeStruct((n_blocks * MOE_ROWS, D), F32),
        compiler_params=_params(("arbitrary",)),
        name="moe_experts",
    )(block_e, n_used, slot_tok, xn, e_gate, e_up, e_down)


def _combine_kernel(dest_ref, x_ref, route_ref, ys_hbm, o_ref, buf, sem, *, tm):
    i = pl.program_id(0)
    cur = i & 1

    def start(tile, par):
        for k in range(2):
            _row_gather_start(ys_hbm, dest_ref, tile * (2 * tm) + k, 2, buf.at[par, k], sem.at[par], tm)

    @pl.when(i == 0)
    def _():
        start(0, 0)

    @pl.when(i + 1 < pl.num_programs(0))
    def _():
        start(i + 1, 1 - cur)

    for k in range(2):
        _row_gather_wait(ys_hbm, buf.at[cur, k], sem.at[cur], tm)
    route = route_ref[...]
    o_ref[...] = x_ref[...] + route[:, 2:3] * buf[cur, 0] + route[:, 3:4] * buf[cur, 1]


def _combine(x2d, y_slots, dest, route):
    T, D = x2d.shape
    tm = min(256, T)
    row = lambda n: pl.BlockSpec((tm, n), lambda i, dest: (i, 0))
    return pl.pallas_call(
        functools.partial(_combine_kernel, tm=tm),
        grid_spec=pltpu.PrefetchScalarGridSpec(
            num_scalar_prefetch=1,
            grid=(T // tm,),
            in_specs=[row(D), row(LANES), pl.BlockSpec(memory_space=pl.ANY)],
            out_specs=row(D),
            scratch_shapes=[pltpu.VMEM((2, 2, tm, D), F32), pltpu.SemaphoreType.DMA((2,))],
        ),
        out_shape=jax.ShapeDtypeStruct((T, D), F32),
        compiler_params=_params(("arbitrary",)),
        name="moe_combine",
    )(dest, x2d, route, y_slots)


def _moe_layer(x, gain, wg, bg, we, be, layer, e_gate, e_up, e_down):
    B, S, D = x.shape
    T = B * S
    x2d = x.reshape(T, D)
    xn, route = _router(x2d, gain, wg, bg, we, be)
    flat_e = route[:, 0:2].astype(jnp.int32).reshape(-1)
    n_pairs = 2 * T
    onehot = (flat_e[:, None] == jnp.arange(N_EXPERTS, dtype=jnp.int32)[None, :]).astype(jnp.int32)
    csum = jnp.cumsum(onehot, axis=0)
    rank = jnp.take_along_axis(csum, flat_e[:, None], axis=1)[:, 0] - 1
    counts = csum[-1]
    padded = (counts + MOE_ROWS - 1) // MOE_ROWS * MOE_ROWS
    pad_end = jnp.cumsum(padded)
    dest = (pad_end - padded)[flat_e] + rank
    n_blocks = -(-n_pairs // MOE_ROWS) + N_EXPERTS
    dest = dest.astype(jnp.int32)
    slot_tok = _slot_tokens(dest, n_blocks * MOE_ROWS)
    block_start = jnp.arange(n_blocks, dtype=jnp.int32) * MOE_ROWS
    block_e = jnp.minimum(jnp.sum((pad_end[None, :] <= block_start[:, None]).astype(jnp.int32), axis=1),
                          N_EXPERTS - 1)
    n_used = (pad_end[-1:] // MOE_ROWS).astype(jnp.int32)
    y_slots = _experts(xn, slot_tok, block_e, n_used, layer, e_gate, e_up, e_down)
    return _combine(x2d, y_slots, dest, route).reshape(B, S, D)


def kernel(x, mix_norm, attn_w_in, attn_gate_bias, diff_q_norm, diff_k_norm, diff_lambda, diff_subln, nsa_q_norm, nsa_k_norm, nsa_cmp_pe, nsa_cmp_w1, nsa_cmp_w2, attn_w_out, rwkv_mix, rwkv_w_r, rwkv_w_k, rwkv_w_v, rwkv_decay_w0, rwkv_decay_w1, rwkv_decay_w2, rwkv_iclr_a0, rwkv_iclr_a1, rwkv_iclr_a2, rwkv_gate_g1, rwkv_gate_g2, rwkv_k_k, rwkv_k_a, rwkv_r_k, rwkv_ln_w, rwkv_ln_b, rwkv_w_o, ffn_norm, router_group_w, router_group_b, router_expert_w, router_expert_b, expert_w_gate, expert_w_up, expert_w_down):
    depth = mix_norm.shape[0]
    for layer in range(depth):
        i = layer // 2
        if layer % 2 == 0:
            x = _hybrid_attention_layer(
                x, layer, mix_norm[layer], attn_w_in[i], attn_gate_bias[i], diff_q_norm[i], diff_k_norm[i],
                diff_lambda[i], diff_subln[i], nsa_q_norm[i], nsa_k_norm[i], nsa_cmp_pe[i], nsa_cmp_w1[i],
                nsa_cmp_w2[i], attn_w_out[i])
        else:
            x = _rwkv_layer(
                x, mix_norm[layer], rwkv_mix[i], rwkv_w_r[i], rwkv_w_k[i], rwkv_w_v[i], rwkv_decay_w0[i],
                rwkv_decay_w1[i], rwkv_decay_w2[i], rwkv_iclr_a0[i], rwkv_iclr_a1[i], rwkv_iclr_a2[i],
                rwkv_gate_g1[i], rwkv_gate_g2[i], rwkv_k_k[i], rwkv_k_a[i], rwkv_r_k[i], rwkv_ln_w[i],
                rwkv_ln_b[i], rwkv_w_o[i])
        x = _moe_layer(x, ffn_norm[layer], router_group_w[layer], router_group_b[layer], router_expert_w[layer],
                       router_expert_b[layer], layer, expert_w_gate, expert_w_up, expert_w_down)
    return x
```
